```python
import jax, jax.numpy as jnp
from jax import lax
import numpy as np

D_MODEL = 1024
BATCH = 2
SEQ = 8192
DEPTH = 2

N_META = 16
CHUNK = 64
PAD = CHUNK - N_META
D_FF = 2816
EPS = 1e-6
N_MIXERS = 2
N_RET = (DEPTH + 1) // 2
N_GLA = DEPTH // 2

RET_HEADS = 4
RET_DK = D_MODEL // RET_HEADS
RET_DV = 2 * D_MODEL // RET_HEADS
RET_IN = 2 * D_MODEL + 4 * D_MODEL
ROPE_BASE = 10000.0

GLA_HEADS = 4
GLA_DK = D_MODEL // 2 // GLA_HEADS
GLA_DV = D_MODEL // GLA_HEADS
GLA_RANK = 16
GLA_TAU = 16.0
GLA_HK = GLA_HEADS * GLA_DK
GLA_HV = GLA_HEADS * GLA_DV
GLA_IN = 2 * GLA_HK + 2 * GLA_HV + GLA_RANK

kernel_name = "hybrid_retnet_gla_macaron_meta"


def rmsnorm(x, g):
    xf = x.astype(jnp.float32)
    y = xf * lax.rsqrt(jnp.mean(xf * xf, axis=-1, keepdims=True) + EPS)
    return (y * g).astype(x.dtype)


def swiglu(x, w_in, w_out):
    gate, up = jnp.split(x @ w_in, 2, axis=-1)
    return (jax.nn.silu(gate) * up) @ w_out


def to_chunks(t):
    b, T, h, d = t.shape
    n = (T + PAD) // CHUNK
    t = jnp.pad(t, ((0, 0), (PAD, 0), (0, 0), (0, 0)))
    return t.reshape(b, n, CHUNK, h, d).transpose(1, 0, 3, 2, 4)


def from_chunks(t):
    n, b, h, c, d = t.shape
    return t.transpose(1, 0, 3, 2, 4).reshape(b, n * c, h, d)[:, PAD:]


def rotary(t, pos):
    half = t.shape[-1] // 2
    inv = 1.0 / (ROPE_BASE ** jnp.linspace(0.0, 1.0, half, dtype=jnp.float32))
    ang = pos.astype(jnp.float32)[:, None] * inv[None, :]
    cos = jnp.cos(ang)[None, :, None, :].astype(t.dtype)
    sin = jnp.sin(ang)[None, :, None, :].astype(t.dtype)
    t1, t2 = t[..., :half], t[..., half:]
    return jnp.concatenate([t1 * cos - t2 * sin, t1 * sin + t2 * cos], axis=-1)


def retention(h, w_in, head_norm, w_out):
    b, T, _ = h.shape
    q, k, v, g = jnp.split(h @ w_in, [D_MODEL, 2 * D_MODEL, 4 * D_MODEL], axis=-1)
    pos = jnp.arange(T)
    q = rotary(q.reshape(b, T, RET_HEADS, RET_DK), pos)
    k = rotary(k.reshape(b, T, RET_HEADS, RET_DK), pos) * (RET_DK ** -0.5)
    v = v.reshape(b, T, RET_HEADS, RET_DV)

    log_gamma = jnp.log1p(-2.0 ** (-5.0 - jnp.arange(RET_HEADS, dtype=jnp.float32)))
    idx = jnp.arange(CHUNK, dtype=jnp.float32)
    rel = idx[:, None] - idx[None, :]
    decay_intra = jnp.where(rel >= 0, jnp.exp(log_gamma[:, None, None] * jnp.maximum(rel, 0.0)), 0.0)
    decay_q = jnp.exp(log_gamma[:, None] * (idx + 1.0))[..., None]
    decay_k = jnp.exp(log_gamma[:, None] * (CHUNK - 1.0 - idx))[..., None]
    decay_chunk = jnp.exp(log_gamma * CHUNK)[:, None, None]

    def step(S, inp):
        qi, ki, vi = inp
        scores = jnp.einsum('bhid,bhjd->bhij', qi, ki) * decay_intra
        o = (jnp.einsum('bhij,bhjv->bhiv', scores, vi)
             + jnp.einsum('bhid,bhdv->bhiv', qi * decay_q, S))
        S = S * decay_chunk + jnp.einsum('bhjd,bhjv->bhdv', ki * decay_k, vi)
        return S, o

    S0 = jnp.zeros((b, RET_HEADS, RET_DK, RET_DV), jnp.float32)
    _, o = lax.scan(step, S0, (to_chunks(q), to_chunks(k), to_chunks(v)))
    o = rmsnorm(from_chunks(o), head_norm)
    o = o.reshape(b, T, RET_HEADS * RET_DV) * jax.nn.silu(g)
    return o @ w_out


def gla(h, w_in, w_gate, b_gate, head_norm, w_out):
    b, T, _ = h.shape
    q, k, v, g, z = jnp.split(h @ w_in, [GLA_HK, 2 * GLA_HK, 2 * GLA_HK + GLA_HV, 2 * GLA_HK + 2 * GLA_HV], axis=-1)
    q = q.reshape(b, T, GLA_HEADS, GLA_DK) * (GLA_DK ** -0.5)
    k = k.reshape(b, T, GLA_HEADS, GLA_DK)
    v = v.reshape(b, T, GLA_HEADS, GLA_DV)
    log_a = jax.nn.log_sigmoid((z @ w_gate + b_gate).astype(jnp.float32)) / GLA_TAU
    log_a = log_a.reshape(b, T, GLA_HEADS, GLA_DK)
    causal = jnp.tril(jnp.ones((CHUNK, CHUNK), dtype=bool))[:, :, None]

    def step(S, inp):
        qi, ki, vi, ai = inp
        bcum = jnp.cumsum(ai, axis=2)
        diff = bcum[:, :, :, None, :] - bcum[:, :, None, :, :]
        dec = jnp.exp(jnp.where(causal, diff, -jnp.inf))
        scores = jnp.einsum('bhid,bhijd,bhjd->bhij', qi, dec, ki)
        o = (jnp.einsum('bhij,bhjv->bhiv', scores, vi)
             + jnp.einsum('bhid,bhdv->bhiv', qi * jnp.exp(bcum), S))
        btot = bcum[:, :, -1:, :]
        S = (S * jnp.exp(btot)[:, :, 0, :, None]
             + jnp.einsum('bhjd,bhjv->bhdv', ki * jnp.exp(btot - bcum), vi))
        return S, o

    S0 = jnp.zeros((b, GLA_HEADS, GLA_DK, GLA_DV), jnp.float32)
    _, o = lax.scan(step, S0, (to_chunks(q), to_chunks(k), to_chunks(v), to_chunks(log_a)))
    o = rmsnorm(from_chunks(o), head_norm)
    o = o.reshape(b, T, GLA_HV) * jax.nn.silu(g)
    return o @ w_out


def setup_inputs(seed: int = 0) -> dict:
    key = jax.random.key(seed)
    ks = jax.random.split(key, 20)
    nrm = lambda k, shape, fan_in: jax.random.normal(k, shape, jnp.float32) * (fan_in ** -0.5)
    gain = lambda k, shape: 1.0 + 0.01 * jax.random.normal(k, shape, jnp.float32)
    return {
        "x": jax.random.normal(ks[0], (BATCH, SEQ, D_MODEL), jnp.float32),
        "meta_tokens": jax.random.normal(ks[1], (N_META, D_MODEL), jnp.float32),
        "norm_ffn1": gain(ks[2], (DEPTH, D_MODEL)),
        "ffn1_w_in": nrm(ks[3], (DEPTH, D_MODEL, 2 * D_FF), D_MODEL),
        "ffn1_w_out": nrm(ks[4], (DEPTH, D_FF, D_MODEL), D_FF),
        "norm_mix": gain(ks[5], (DEPTH, D_MODEL)),
        "norm_ffn2": gain(ks[6], (DEPTH, D_MODEL)),
        "ffn2_w_in": nrm(ks[7], (DEPTH, D_MODEL, 2 * D_FF), D_MODEL),
        "ffn2_w_out": nrm(ks[8], (DEPTH, D_FF, D_MODEL), D_FF),
        "ret_w_in": nrm(ks[9], (N_RET, D_MODEL, RET_IN), D_MODEL),
        "ret_head_norm": gain(ks[10], (N_RET, RET_HEADS, RET_DV)),
        "ret_w_out": nrm(ks[11], (N_RET, RET_HEADS * RET_DV, D_MODEL), RET_HEADS * RET_DV),
        "gla_w_in": nrm(ks[12], (N_GLA, D_MODEL, GLA_IN), D_MODEL),
        "gla_w_gate": nrm(ks[13], (N_GLA, GLA_RANK, GLA_HK), GLA_RANK),
        "gla_b_gate": 0.1 * jax.random.normal(ks[14], (N_GLA, GLA_HK), jnp.float32),
        "gla_head_norm": gain(ks[15], (N_GLA, GLA_HEADS, GLA_DV)),
        "gla_w_out": nrm(ks[16], (N_GLA, GLA_HV, D_MODEL), GLA_HV),
        "final_norm": gain(ks[17], (D_MODEL,)),
    }


def reference(x, meta_tokens, norm_ffn1, ffn1_w_in, ffn1_w_out, norm_mix, norm_ffn2,
              ffn2_w_in, ffn2_w_out, ret_w_in, ret_head_norm, ret_w_out,
              gla_w_in, gla_w_gate, gla_b_gate, gla_head_norm, gla_w_out, final_norm):
    b = x.shape[0]
    meta = jnp.broadcast_to(meta_tokens[None].astype(x.dtype), (b, N_META, D_MODEL))
    h = jnp.concatenate([meta, x], axis=1)
    for i in range(DEPTH):
        j = i // N_MIXERS
        h = h + 0.5 * swiglu(rmsnorm(h, norm_ffn1[i]), ffn1_w_in[i], ffn1_w_out[i])
        hn = rmsnorm(h, norm_mix[i])
        if i % N_MIXERS == 0:
            mix = retention(hn, ret_w_in[j], ret_head_norm[j], ret_w_out[j])
        else:
            mix = gla(hn, gla_w_in[j], gla_w_gate[j], gla_b_gate[j], gla_head_norm[j], gla_w_out[j])
        h = h + mix
        h = h + 0.5 * swiglu(rmsnorm(h, norm_ffn2[i]), ffn2_w_in[i], ffn2_w_out[i])
    h = rmsnorm(h, final_norm)
    return h[:, N_META:]
```

```python
import functools
import math

import jax
import jax.numpy as jnp
from jax import lax
from jax.experimental import pallas as pl
from jax.experimental.pallas import tpu as pltpu

F32 = jnp.float32
BF16 = jnp.bfloat16

EPS = 1e-6
N_META = 16
LEAD = 256
LEAD_ZERO = LEAD - N_META
ROPE_BASE = 10000.0
RET_HEADS = 4
GLA_HEADS = 4
GLA_RANK = 16
GLA_TAU = 16.0
RET_CHUNK = 256
GLA_CHUNK = 128
GLA_SUB = 8
LANES = 128
VMEM_LIMIT = 56 * 1024 * 1024

RET_LOG_GAMMA = tuple(math.log1p(-2.0 ** (-5.0 - h)) for h in range(RET_HEADS))


def _const_spec(shape):
    nd = len(shape)
    return pl.BlockSpec(shape, lambda *_: (0,) * nd, pipeline_mode=pl.Buffered(1))


def _rms(x, w):
    ms = jnp.mean(x * x, axis=-1, keepdims=True)
    return x * lax.rsqrt(ms + EPS) * w


def _silu(x):
    return x * jax.nn.sigmoid(x)


def _dot(a, b):
    return jnp.dot(a, b, preferred_element_type=F32)


def _dot_nt(a, b):
    return lax.dot_general(a, b, (((1,), (1,)), ((), ())), preferred_element_type=F32)


def _dot_tn(a, b):
    return lax.dot_general(a, b, (((0,), (0,)), ((), ())), preferred_element_type=F32)


def _ffn_kernel(x_ref, nw_ref, wg_ref, wu_ref, wo_ref, *rest, final):
    if final:
        fw_ref, o_ref = rest
    else:
        (o_ref,) = rest
    x = x_ref[...]
    xn = _rms(x, nw_ref[...]).astype(BF16)
    g = _dot(xn, wg_ref[...])
    u = _dot(xn, wu_ref[...])
    hid = (_silu(g) * u).astype(BF16)
    y = x + 0.5 * _dot(hid, wo_ref[...])
    if final:
        y = _rms(y, fw_ref[...])
    o_ref[...] = y


def _ffn(h, norm_w, w_in, w_out, final_w=None, *, tm):
    rows, d = h.shape
    dff = w_out.shape[0]
    final = final_w is not None
    in_specs = [
        pl.BlockSpec((tm, d), lambda i: (i, 0)),
        _const_spec((1, d)),
        pl.BlockSpec((d, dff), lambda i: (0, 0), pipeline_mode=pl.Buffered(1)),
        pl.BlockSpec((d, dff), lambda i: (0, 1), pipeline_mode=pl.Buffered(1)),
        _const_spec((dff, d)),
    ]
    args = [h, norm_w.reshape(1, d), w_in, w_in, w_out]
    if final:
        in_specs.append(_const_spec((1, d)))
        args.append(final_w.reshape(1, d))
    return pl.pallas_call(
        functools.partial(_ffn_kernel, final=final),
        grid=(rows // tm,),
        in_specs=in_specs,
        out_specs=pl.BlockSpec((tm, d), lambda i: (i, 0)),
        out_shape=jax.ShapeDtypeStruct((rows, d), F32),
        compiler_params=pltpu.CompilerParams(
            dimension_semantics=("arbitrary",), vmem_limit_bytes=VMEM_LIMIT),
        name="ffn_final" if final else "ffn",
    )(*args)


def _ret_proj_kernel(h_ref, nw_ref, w_ref, inv_ref, o_ref, cosr_ref, sinr_ref, *, tm, rows_per_batch):
    i = pl.program_id(0)
    d = h_ref.shape[1]
    half = LANES

    @pl.when(i == 0)
    def _():
        r = lax.broadcasted_iota(jnp.int32, (tm, half), 0).astype(F32)
        ang = r * inv_ref[...]
        cosr_ref[...] = jnp.cos(ang)
        sinr_ref[...] = jnp.sin(ang)

    base = ((i * tm) % rows_per_batch - LEAD_ZERO).astype(F32)
    ang0 = base * inv_ref[...]
    cb, sb = jnp.cos(ang0), jnp.sin(ang0)
    cos = cb * cosr_ref[...] - sb * sinr_ref[...]
    sin = sb * cosr_ref[...] + cb * sinr_ref[...]

    hn = _rms(h_ref[...], nw_ref[...]).astype(BF16)
    n_chunks = w_ref.shape[1] // d
    for j in range(n_chunks):
        y = _dot(hn, w_ref[:, j * d:(j + 1) * d])
        if j < 2:
            scale = 1.0 if j == 0 else (2 * half) ** -0.5
            for hd in range(d // (2 * half)):
                lo = hd * 2 * half
                t1 = y[:, lo:lo + half]
                t2 = y[:, lo + half:lo + 2 * half]
                o_ref[:, j * d + lo:j * d + lo + half] = ((t1 * cos - t2 * sin) * scale).astype(BF16)
                o_ref[:, j * d + lo + half:j * d + lo + 2 * half] = ((t1 * sin + t2 * cos) * scale).astype(BF16)
        else:
            o_ref[:, j * d:(j + 1) * d] = y.astype(BF16)


def _ret_proj(h, norm_w, w_in, inv, *, tm, rows_per_batch):
    rows, d = h.shape
    n = w_in.shape[1]
    return pl.pallas_call(
        functools.partial(_ret_proj_kernel, tm=tm, rows_per_batch=rows_per_batch),
        grid=(rows // tm,),
        in_specs=[
            pl.BlockSpec((tm, d), lambda i: (i, 0)),
            _const_spec((1, d)),
            _const_spec((d, n)),
            _const_spec((1, LANES)),
        ],
        out_specs=pl.BlockSpec((tm, n), lambda i: (i, 0)),
        out_shape=jax.ShapeDtypeStruct((rows, n), BF16),
        scratch_shapes=[pltpu.VMEM((tm, LANES), F32), pltpu.VMEM((tm, LANES), F32)],
        compiler_params=pltpu.CompilerParams(
            dimension_semantics=("arbitrary",), vmem_limit_bytes=VMEM_LIMIT),
        name="ret_proj",
    )(h, norm_w.reshape(1, d), w_in, inv.reshape(1, LANES))


def _ret_core_kernel(p_ref, h_ref, hn_ref, wo_ref, o_ref,
                     s_ref, dec_ref, dq_ref, dk_ref, obuf_ref, *, dk, dv):
    c = pl.program_id(1)
    C = p_ref.shape[0]
    nh = RET_HEADS
    k0, v0, g0 = nh * dk, 2 * nh * dk, 2 * nh * dk + nh * dv

    @pl.when(c == 0)
    def _():
        s_ref[...] = jnp.zeros_like(s_ref)
        row = lax.broadcasted_iota(jnp.int32, (C, C), 0)
        col = lax.broadcasted_iota(jnp.int32, (C, C), 1)
        rel = (row - col).astype(F32)
        rowl = lax.broadcasted_iota(jnp.int32, (C, LANES), 0).astype(F32)
        for hd in range(nh):
            lg = RET_LOG_GAMMA[hd]
            dec_ref[hd] = jnp.where(rel >= 0, jnp.exp(lg * jnp.maximum(rel, 0.0)), 0.0)
            dq_ref[hd] = jnp.exp(lg * (rowl + 1.0))
            dk_ref[hd] = jnp.exp(lg * (C - 1.0 - rowl))

    for hd in range(nh):
        q = p_ref[:, hd * dk:(hd + 1) * dk]
        k = p_ref[:, k0 + hd * dk:k0 + (hd + 1) * dk]
        v = p_ref[:, v0 + hd * dv:v0 + (hd + 1) * dv]
        g = p_ref[:, g0 + hd * dv:g0 + (hd + 1) * dv]
        s = (_dot_nt(q, k) * dec_ref[hd]).astype(BF16)
        st = s_ref[hd]
        dq = jnp.concatenate([dq_ref[hd]] * (dv // LANES), axis=1)
        o = _dot(s, v) + dq * _dot(q, st.astype(BF16))
        dkc = jnp.concatenate([dk_ref[hd]] * (dk // LANES), axis=1)
        kd = (k.astype(F32) * dkc).astype(BF16)
        s_ref[hd] = st * math.exp(RET_LOG_GAMMA[hd] * C) + _dot_tn(kd, v)
        on = _rms(o, hn_ref[hd:hd + 1, :]) * _silu(g.astype(F32))
        obuf_ref[:, hd * dv:(hd + 1) * dv] = on.astype(BF16)

    o_ref[...] = h_ref[...] + _dot(obuf_ref[...], wo_ref[...])


def _ret_core(p, h, head_norm, w_out, *, batch):
    rows, d = h.shape
    n = p.shape[1]
    nh, dv = head_norm.shape
    dk = (n - 2 * nh * dv) // (2 * nh)
    C = RET_CHUNK
    nc = rows // batch // C
    return pl.pallas_call(
        functools.partial(_ret_core_kernel, dk=dk, dv=dv),
        grid=(batch, nc),
        in_specs=[
            pl.BlockSpec((C, n), lambda b, c: (b * nc + c, 0)),
            pl.BlockSpec((C, d), lambda b, c: (b * nc + c, 0)),
            _const_spec((nh, dv)),
            _const_spec((nh * dv, d)),
        ],
        out_specs=pl.BlockSpec((C, d), lambda b, c: (b * nc + c, 0)),
        out_shape=jax.ShapeDtypeStruct((rows, d), F32),
        scratch_shapes=[
            pltpu.VMEM((nh, dk, dv), F32),
            pltpu.VMEM((nh, C, C), F32),
            pltpu.VMEM((nh, C, LANES), F32),
            pltpu.VMEM((nh, C, LANES), F32),
            pltpu.VMEM((C, nh * dv), BF16),
        ],
        compiler_params=pltpu.CompilerParams(
            dimension_semantics=("arbitrary", "arbitrary"), vmem_limit_bytes=VMEM_LIMIT),
        name="ret_core",
    )(p, h, head_norm, w_out)


def _gla_proj_kernel(h_ref, nw_ref, w_ref, wg_ref, bg_ref, o_ref, la_ref, *, tm, rows_per_batch, n_main):
    i = pl.program_id(0)
    hk = wg_ref.shape[1]
    dk = hk // GLA_HEADS
    hn = _rms(h_ref[...], nw_ref[...]).astype(BF16)
    step = 512
    for lo in range(0, n_main, step):
        y = _dot(hn, w_ref[:, lo:lo + step])
        if lo < hk:
            y = y * dk ** -0.5
        o_ref[:, lo:lo + step] = y.astype(BF16)
    z = _dot(hn, w_ref[:, n_main:n_main + GLA_RANK])
    xg = _dot(z.astype(BF16), wg_ref[...]) + bg_ref[...]
    ls = jnp.minimum(xg, 0.0) - jnp.log1p(jnp.exp(-jnp.abs(xg)))
    rb = (i * tm) % rows_per_batch + lax.broadcasted_iota(jnp.int32, (tm, hk), 0)
    la_ref[...] = jnp.where(rb >= LEAD_ZERO, ls * (1.0 / GLA_TAU), 0.0)


def _gla_proj(h, norm_w, w_in, w_gate, b_gate, *, tm, rows_per_batch):
    rows, d = h.shape
    n = w_in.shape[1]
    n_main = n - GLA_RANK
    hk = w_gate.shape[1]
    return pl.pallas_call(
        functools.partial(_gla_proj_kernel, tm=tm, rows_per_batch=rows_per_batch, n_main=n_main),
        grid=(rows // tm,),
        in_specs=[
            pl.BlockSpec((tm, d), lambda i: (i, 0)),
            _const_spec((1, d)),
            _const_spec((d, n)),
            _const_spec((GLA_RANK, hk)),
            _const_spec((1, hk)),
        ],
        out_specs=[pl.BlockSpec((tm, n_main), lambda i: (i, 0)),
                   pl.BlockSpec((tm, hk), lambda i: (i, 0))],
        out_shape=[jax.ShapeDtypeStruct((rows, n_main), BF16),
                   jax.ShapeDtypeStruct((rows, hk), F32)],
        compiler_params=pltpu.CompilerParams(
            dimension_semantics=("arbitrary",), vmem_limit_bytes=VMEM_LIMIT),
        name="gla_proj",
    )(h, norm_w.reshape(1, d), w_in, w_gate, b_gate.reshape(1, hk))


def _gla_core_kernel(p_ref, la_ref, h_ref, hn_ref, wo_ref, o_ref, st_ref, b_ref, obuf_ref, *, dk, dv):
    c = pl.program_id(1)
    C = p_ref.shape[0]
    nh = GLA_HEADS
    k0, v0, g0 = nh * dk, 2 * nh * dk, 2 * nh * dk + nh * dv

    @pl.when(c == 0)
    def _():
        st_ref[...] = jnp.zeros_like(st_ref)

    rowl = lax.broadcasted_iota(jnp.int32, (C, dk), 0)
    row = lax.broadcasted_iota(jnp.int32, (C, C), 0)
    col = lax.broadcasted_iota(jnp.int32, (C, C), 1)
    ones = jnp.ones((dk, C), BF16)

    for hd in range(nh):
        a = la_ref[:, hd * dk:(hd + 1) * dk]
        q = p_ref[:, hd * dk:(hd + 1) * dk].astype(F32)
        k = p_ref[:, k0 + hd * dk:k0 + (hd + 1) * dk].astype(F32)
        v = p_ref[:, v0 + hd * dv:v0 + (hd + 1) * dv]
        g = p_ref[:, g0 + hd * dv:g0 + (hd + 1) * dv]

        b = a
        sh = 1
        while sh < C:
            b = b + jnp.where(rowl >= sh, pltpu.roll(b, sh, axis=0), 0.0)
            sh *= 2
        b_ref[...] = b

        scores = jnp.where(col == row, _dot((q * k).astype(BF16), ones), 0.0)
        for r in range(1, GLA_SUB):
            valid = (rowl % GLA_SUB) >= r
            w = jnp.exp(jnp.minimum(b - pltpu.roll(b, r, axis=0), 0.0))
            t = jnp.where(valid, q * pltpu.roll(k, r, axis=0) * w, 0.0)
            scores = scores + jnp.where(col == row - r, _dot(t.astype(BF16), ones), 0.0)

        s = GLA_SUB
        while s < C:
            be = jnp.concatenate(
                [jnp.broadcast_to(b_ref[e:e + 1, :], (2 * s, dk)) for e in range(s - 1, C, 2 * s)], axis=0)
            w = jnp.exp(-jnp.abs(b - be))
            right = (rowl % (2 * s)) >= s
            qt = jnp.where(right, q * w, 0.0).astype(BF16)
            kt = jnp.where(right, 0.0, k * w).astype(BF16)
            sc = _dot_nt(qt, kt)
            if 2 * s < C:
                sc = jnp.where((row // (2 * s)) == (col // (2 * s)), sc, 0.0)
            scores = scores + sc
            s *= 2

        btot = b_ref[C - 1:C, :]
        st = st_ref[hd]
        o = _dot(scores.astype(BF16), v) + _dot_nt((q * jnp.exp(b)).astype(BF16), st.astype(BF16))
        kt = (k * jnp.exp(btot - b)).astype(BF16)
        st_ref[hd] = st * jnp.exp(btot) + _dot_tn(v, kt)
        on = _rms(o, hn_ref[hd:hd + 1, :]) * _silu(g.astype(F32))
        obuf_ref[:, hd * dv:(hd + 1) * dv] = on.astype(BF16)

    o_ref[...] = h_ref[...] + _dot(obuf_ref[...], wo_ref[...])


def _gla_core(p, la, h, head_norm, w_out, *, batch):
    rows, d = h.shape
    n = p.shape[1]
    nh, dv = head_norm.shape
    hk = la.shape[1]
    dk = hk // nh
    C = GLA_CHUNK
    nc = rows // batch // C
    return pl.pallas_call(
        functools.partial(_gla_core_kernel, dk=dk, dv=dv),
        grid=(batch, nc),
        in_specs=[
            pl.BlockSpec((C, n), lambda b, c: (b * nc + c, 0)),
            pl.BlockSpec((C, hk), lambda b, c: (b * nc + c, 0)),
            pl.BlockSpec((C, d), lambda b, c: (b * nc + c, 0)),
            _const_spec((nh, dv)),
            _const_spec((nh * dv, d)),
        ],
        out_specs=pl.BlockSpec((C, d), lambda b, c: (b * nc + c, 0)),
        out_shape=jax.ShapeDtypeStruct((rows, d), F32),
        scratch_shapes=[
            pltpu.VMEM((nh, dv, dk), F32),
            pltpu.VMEM((C, dk), F32),
            pltpu.VMEM((C, nh * dv), BF16),
        ],
        compiler_params=pltpu.CompilerParams(
            dimension_semantics=("arbitrary", "arbitrary"), vmem_limit_bytes=VMEM_LIMIT),
        name="gla_core",
    )(p, la, h, head_norm, w_out)


def _pick_tile(rows, candidates):
    for t in candidates:
        if rows % t == 0:
            return t
    raise ValueError(f"no tile for {rows} rows")


def kernel(x, meta_tokens, norm_ffn1, ffn1_w_in, ffn1_w_out, norm_mix, norm_ffn2, ffn2_w_in, ffn2_w_out, ret_w_in, ret_head_norm, ret_w_out, gla_w_in, gla_w_gate, gla_b_gate, gla_head_norm, gla_w_out, final_norm):
    batch, seq, d = x.shape
    depth = norm_ffn1.shape[0]
    rpb = LEAD + seq
    assert meta_tokens.shape == (N_META, d) and rpb % RET_CHUNK == 0 and rpb % GLA_CHUNK == 0
    rows = batch * rpb
    tm_ffn = _pick_tile(rows, (512, 256, 128))
    tm_proj = _pick_tile(rpb, (768, 256, 128))

    lead = jnp.zeros((batch, LEAD_ZERO, d), x.dtype)
    meta = jnp.broadcast_to(meta_tokens[None].astype(x.dtype), (batch, N_META, d))
    h = jnp.concatenate([lead, meta, x], axis=1).reshape(rows, d)

    half = ret_w_in.shape[2] // 6 // RET_HEADS // 2
    assert half == LANES
    inv = 1.0 / (ROPE_BASE ** jnp.linspace(0.0, 1.0, half, dtype=F32))
    bf = lambda w: w.astype(BF16)

    for i in range(depth):
        j = i // 2
        h = _ffn(h, norm_ffn1[i], bf(ffn1_w_in[i]), bf(ffn1_w_out[i]), tm=tm_ffn)
        if i % 2 == 0:
            p = _ret_proj(h, norm_mix[i], bf(ret_w_in[j]), inv, tm=tm_proj, rows_per_batch=rpb)
            h = _ret_core(p, h, ret_head_norm[j], bf(ret_w_out[j]), batch=batch)
        else:
            p, la = _gla_proj(h, norm_mix[i], bf(gla_w_in[j]), bf(gla_w_gate[j]), gla_b_gate[j],
                              tm=tm_proj, rows_per_batch=rpb)
            h = _gla_core(p, la, h, gla_head_norm[j], bf(gla_w_out[j]), batch=batch)
        last = i == depth - 1
        h = _ffn(h, norm_ffn2[i], bf(ffn2_w_in[i]), bf(ffn2_w_out[i]),
                 final_norm if last else None, tm=tm_ffn)
    return h.reshape(batch, rpb, d)[:, LEAD:]
```

```python
import functools
import math

import jax
import jax.numpy as jnp
from jax import lax
from jax.experimental import pallas as pl
from jax.experimental.pallas import tpu as pltpu

F32 = jnp.float32
BF16 = jnp.bfloat16

EPS = 1e-6
N_META = 16
LEAD = 256
LEAD_ZERO = LEAD - N_META
ROW_TILE = 512
ROPE_BASE = 10000.0
RET_HEADS = 4
GLA_HEADS = 4
GLA_RANK = 16
GLA_TAU = 16.0
RET_CHUNK = 256
GLA_CHUNK = 128
GLA_SUB = 8
LANES = 128
VMEM_LIMIT = 56 * 1024 * 1024

RET_LOG_GAMMA = tuple(math.log1p(-2.0 ** (-5.0 - h)) for h in range(RET_HEADS))


def _const_spec(shape):
    nd = len(shape)
    return pl.BlockSpec(shape, lambda *_: (0,) * nd, pipeline_mode=pl.Buffered(1))


def _rms(x, w):
    ms = jnp.mean(x * x, axis=-1, keepdims=True)
    return x * lax.rsqrt(ms + EPS) * w


def _silu(x):
    return x * jax.nn.sigmoid(x)


def _dot(a, b):
    return jnp.dot(a, b, preferred_element_type=F32)


def _dot_nt(a, b):
    return lax.dot_general(a, b, (((1,), (1,)), ((), ())), preferred_element_type=F32)


def _dot_tn(a, b):
    return lax.dot_general(a, b, (((0,), (0,)), ((), ())), preferred_element_type=F32)


def _ffn_kernel(x_ref, *rest, n_x_tiles, first, final):
    rest = list(rest)
    tail_ref = rest.pop(0) if first else None
    nw_ref, wg_ref, wu_ref, wo_ref = rest[:4]
    fw_ref = rest[4] if final else None
    o_ref = rest[-1]
    x = x_ref[...]
    if first:
        x = jnp.where(pl.program_id(0) >= n_x_tiles, tail_ref[...], x)
    xn = _rms(x, nw_ref[...]).astype(BF16)
    g = _dot(xn, wg_ref[...])
    u = _dot(xn, wu_ref[...])
    hid = (_silu(g) * u).astype(BF16)
    y = x + 0.5 * _dot(hid, wo_ref[...])
    if final:
        y = _rms(y, fw_ref[...])
    o_ref[...] = y


def _ffn(h, norm_w, w_in, w_out, *, tm, tail=None, final_w=None):
    d = h.shape[1]
    dff = w_out.shape[0]
    first, final = tail is not None, final_w is not None
    n_x_tiles = (h.shape[0] if first else h.shape[0] - tm) // tm
    n_tiles = n_x_tiles if final else n_x_tiles + 1
    in_specs = [pl.BlockSpec((tm, d), lambda i: (jnp.minimum(i, n_x_tiles - 1), 0) if first else (i, 0))]
    args = [h]
    if first:
        in_specs.append(_const_spec((tm, d)))
        args.append(tail)
    in_specs += [
        _const_spec((1, d)),
        pl.BlockSpec((d, dff), lambda i: (0, 0), pipeline_mode=pl.Buffered(1)),
        pl.BlockSpec((d, dff), lambda i: (0, 1), pipeline_mode=pl.Buffered(1)),
        _const_spec((dff, d)),
    ]
    args += [norm_w.reshape(1, d), w_in, w_in, w_out]
    if final:
        in_specs.append(_const_spec((1, d)))
        args.append(final_w.reshape(1, d))
    return pl.pallas_call(
        functools.partial(_ffn_kernel, n_x_tiles=n_x_tiles, first=first, final=final),
        grid=(n_tiles,),
        in_specs=in_specs,
        out_specs=pl.BlockSpec((tm, d), lambda i: (i, 0)),
        out_shape=jax.ShapeDtypeStruct((n_tiles * tm, d), F32),
        compiler_params=pltpu.CompilerParams(
            dimension_semantics=("arbitrary",), vmem_limit_bytes=VMEM_LIMIT),
        name="ffn_first" if first else "ffn_final" if final else "ffn",
    )(*args)


def _tile_first_pos(i, tm, seq, n_x_tiles):
    return jnp.where(i >= n_x_tiles, -LEAD_ZERO, (i * tm) % seq + N_META)


def _ret_proj_kernel(h_ref, nw_ref, w_ref, inv_ref, o_ref, cosr_ref, sinr_ref, *, tm, seq, n_x_tiles):
    i = pl.program_id(0)
    d = h_ref.shape[1]
    half = LANES

    @pl.when(i == 0)
    def _():
        r = lax.broadcasted_iota(jnp.int32, (tm, half), 0).astype(F32)
        ang = r * inv_ref[...]
        cosr_ref[...] = jnp.cos(ang)
        sinr_ref[...] = jnp.sin(ang)

    base = _tile_first_pos(i, tm, seq, n_x_tiles).astype(F32)
    ang0 = base * inv_ref[...]
    cb, sb = jnp.cos(ang0), jnp.sin(ang0)
    cos = cb * cosr_ref[...] - sb * sinr_ref[...]
    sin = sb * cosr_ref[...] + cb * sinr_ref[...]

    hn = _rms(h_ref[...], nw_ref[...]).astype(BF16)
    n_chunks = w_ref.shape[1] // d
    for j in range(n_chunks):
        y = _dot(hn, w_ref[:, j * d:(j + 1) * d])
        if j < 2:
            scale = 1.0 if j == 0 else (2 * half) ** -0.5
            for hd in range(d // (2 * half)):
                lo = hd * 2 * half
                t1 = y[:, lo:lo + half]
                t2 = y[:, lo + half:lo + 2 * half]
                o_ref[:, j * d + lo:j * d + lo + half] = ((t1 * cos - t2 * sin) * scale).astype(BF16)
                o_ref[:, j * d + lo + half:j * d + lo + 2 * half] = ((t1 * sin + t2 * cos) * scale).astype(BF16)
        else:
            o_ref[:, j * d:(j + 1) * d] = y.astype(BF16)


def _ret_proj(h, norm_w, w_in, inv, *, tm, seq):
    rows, d = h.shape
    n = w_in.shape[1]
    return pl.pallas_call(
        functools.partial(_ret_proj_kernel, tm=tm, seq=seq, n_x_tiles=rows // tm - 1),
        grid=(rows // tm,),
        in_specs=[
            pl.BlockSpec((tm, d), lambda i: (i, 0)),
            _const_spec((1, d)),
            _const_spec((d, n)),
            _const_spec((1, LANES)),
        ],
        out_specs=pl.BlockSpec((tm, n), lambda i: (i, 0)),
        out_shape=jax.ShapeDtypeStruct((rows, n), BF16),
        scratch_shapes=[pltpu.VMEM((tm, LANES), F32), pltpu.VMEM((tm, LANES), F32)],
        compiler_params=pltpu.CompilerParams(
            dimension_semantics=("arbitrary",), vmem_limit_bytes=VMEM_LIMIT),
        name="ret_proj",
    )(h, norm_w.reshape(1, d), w_in, inv.reshape(1, LANES))


def _chunk_block(b, c, *, chunks_per_batch, lead_block):
    first = jnp.where(b == 0, lead_block, b * chunks_per_batch)
    return jnp.where(c == 0, first, b * chunks_per_batch + c - 1)


def _recurrence_specs(C, widths, d, *, batch, seq):
    cpb = seq // C
    assert seq % C == 0 and LEAD_ZERO // C == (LEAD - 1) // C
    lead_block = batch * cpb + LEAD_ZERO // C
    idx = lambda b, c: (_chunk_block(b, c, chunks_per_batch=cpb, lead_block=lead_block), 0)
    return (batch, cpb + 1), [pl.BlockSpec((C, w), idx) for w in widths], pl.BlockSpec((C, d), idx)


def _ret_core_kernel(p_ref, h_ref, hn_ref, wo_ref, o_ref,
                     s_ref, slead_ref, dec_ref, dq_ref, dk_ref, obuf_ref, *, dk, dv):
    b, c = pl.program_id(0), pl.program_id(1)
    C = p_ref.shape[0]
    nh = RET_HEADS
    k0, v0, g0 = nh * dk, 2 * nh * dk, 2 * nh * dk + nh * dv

    @pl.when((b == 0) & (c == 0))
    def _():
        s_ref[...] = jnp.zeros_like(s_ref)
        row = lax.broadcasted_iota(jnp.int32, (C, C), 0)
        col = lax.broadcasted_iota(jnp.int32, (C, C), 1)
        rel = (row - col).astype(F32)
        rowl = lax.broadcasted_iota(jnp.int32, (C, LANES), 0).astype(F32)
        for hd in range(nh):
            lg = RET_LOG_GAMMA[hd]
            dec_ref[hd] = jnp.where(rel >= 0, jnp.exp(lg * jnp.maximum(rel, 0.0)), 0.0)
            dq_ref[hd] = jnp.exp(lg * (rowl + 1.0))
            dk_ref[hd] = jnp.exp(lg * (C - 1.0 - rowl))

    @pl.when((b > 0) & (c == 0))
    def _():
        s_ref[...] = slead_ref[...]

    @pl.when((b == 0) | (c > 0))
    def _():
        for hd in range(nh):
            q = p_ref[:, hd * dk:(hd + 1) * dk]
            k = p_ref[:, k0 + hd * dk:k0 + (hd + 1) * dk]
            v = p_ref[:, v0 + hd * dv:v0 + (hd + 1) * dv]
            g = p_ref[:, g0 + hd * dv:g0 + (hd + 1) * dv]
            s = (_dot_nt(q, k) * dec_ref[hd]).astype(BF16)
            st = s_ref[hd]
            dq = jnp.concatenate([dq_ref[hd]] * (dv // LANES), axis=1)
            o = _dot(s, v) + dq * _dot(q, st.astype(BF16))
            dkc = jnp.concatenate([dk_ref[hd]] * (dk // LANES), axis=1)
            kd = (k.astype(F32) * dkc).astype(BF16)
            s_ref[hd] = st * math.exp(RET_LOG_GAMMA[hd] * C) + _dot_tn(kd, v)
            on = _rms(o, hn_ref[hd:hd + 1, :]) * _silu(g.astype(F32))
            obuf_ref[:, hd * dv:(hd + 1) * dv] = on.astype(BF16)
        o_ref[...] = h_ref[...] + _dot(obuf_ref[...], wo_ref[...])

    @pl.when((b == 0) & (c == 0))
    def _():
        slead_ref[...] = s_ref[...]


def _ret_core(p, h, head_norm, w_out, *, batch, seq):
    rows, d = h.shape
    n = p.shape[1]
    nh, dv = head_norm.shape
    dk = (n - 2 * nh * dv) // (2 * nh)
    C = RET_CHUNK
    grid, (p_spec, h_spec), o_spec = _recurrence_specs(C, (n, d), d, batch=batch, seq=seq)
    return pl.pallas_call(
        functools.partial(_ret_core_kernel, dk=dk, dv=dv),
        grid=grid,
        in_specs=[p_spec, h_spec, _const_spec((nh, dv)), _const_spec((nh * dv, d))],
        out_specs=o_spec,
        out_shape=jax.ShapeDtypeStruct((rows, d), F32),
        input_output_aliases={1: 0},
        scratch_shapes=[
            pltpu.VMEM((nh, dk, dv), F32),
            pltpu.VMEM((nh, dk, dv), F32),
            pltpu.VMEM((nh, C, C), F32),
            pltpu.VMEM((nh, C, LANES), F32),
            pltpu.VMEM((nh, C, LANES), F32),
            pltpu.VMEM((C, nh * dv), BF16),
        ],
        compiler_params=pltpu.CompilerParams(
            dimension_semantics=("arbitrary", "arbitrary"), vmem_limit_bytes=VMEM_LIMIT),
        name="ret_core",
    )(p, h, head_norm, w_out)


def _gla_proj_kernel(h_ref, nw_ref, w_ref, wg_ref, bg_ref, o_ref, la_ref, *, tm, seq, n_x_tiles, n_main):
    i = pl.program_id(0)
    hk = wg_ref.shape[1]
    dk = hk // GLA_HEADS
    hn = _rms(h_ref[...], nw_ref[...]).astype(BF16)
    step = 512
    for lo in range(0, n_main, step):
        y = _dot(hn, w_ref[:, lo:lo + step])
        if lo < hk:
            y = y * dk ** -0.5
        o_ref[:, lo:lo + step] = y.astype(BF16)
    z = _dot(hn, w_ref[:, n_main:n_main + GLA_RANK])
    xg = _dot(z.astype(BF16), wg_ref[...]) + bg_ref[...]
    ls = jnp.minimum(xg, 0.0) - jnp.log1p(jnp.exp(-jnp.abs(xg)))
    pos = _tile_first_pos(i, tm, seq, n_x_tiles) + lax.broadcasted_iota(jnp.int32, (tm, hk), 0)
    la_ref[...] = jnp.where(pos >= 0, ls * (1.0 / GLA_TAU), 0.0)


def _gla_proj(h, norm_w, w_in, w_gate, b_gate, *, tm, seq):
    rows, d = h.shape
    n = w_in.shape[1]
    n_main = n - GLA_RANK
    hk = w_gate.shape[1]
    return pl.pallas_call(
        functools.partial(_gla_proj_kernel, tm=tm, seq=seq, n_x_tiles=rows // tm - 1, n_main=n_main),
        grid=(rows // tm,),
        in_specs=[
            pl.BlockSpec((tm, d), lambda i: (i, 0)),
            _const_spec((1, d)),
            _const_spec((d, n)),
            _const_spec((GLA_RANK, hk)),
            _const_spec((1, hk)),
        ],
        out_specs=[pl.BlockSpec((tm, n_main), lambda i: (i, 0)),
                   pl.BlockSpec((tm, hk), lambda i: (i, 0))],
        out_shape=[jax.ShapeDtypeStruct((rows, n_main), BF16),
                   jax.ShapeDtypeStruct((rows, hk), F32)],
        compiler_params=pltpu.CompilerParams(
            dimension_semantics=("arbitrary",), vmem_limit_bytes=VMEM_LIMIT),
        name="gla_proj",
    )(h, norm_w.reshape(1, d), w_in, w_gate, b_gate.reshape(1, hk))


def _gla_core_kernel(p_ref, la_ref, h_ref, hn_ref, wo_ref, o_ref,
                     st_ref, stlead_ref, ball_ref, obuf_ref, *, dk, dv):
    bi, c = pl.program_id(0), pl.program_id(1)

    @pl.when((bi == 0) & (c == 0))
    def _():
        st_ref[...] = jnp.zeros_like(st_ref)

    @pl.when((bi > 0) & (c == 0))
    def _():
        st_ref[...] = stlead_ref[...]

    @pl.when((bi == 0) | (c > 0))
    def _():
        _gla_chunk(p_ref, la_ref, h_ref, hn_ref, wo_ref, o_ref, st_ref, ball_ref, obuf_ref, dk=dk, dv=dv)

    @pl.when((bi == 0) & (c == 0))
    def _():
        stlead_ref[...] = st_ref[...]


def _gla_chunk(p_ref, la_ref, h_ref, hn_ref, wo_ref, o_ref, st_ref, ball_ref, obuf_ref, *, dk, dv):
    C = p_ref.shape[0]
    nh = GLA_HEADS
    k0, v0, g0 = nh * dk, 2 * nh * dk, 2 * nh * dk + nh * dv

    rowl = lax.broadcasted_iota(jnp.int32, (C, dk), 0)
    row = lax.broadcasted_iota(jnp.int32, (C, C), 0)
    col = lax.broadcasted_iota(jnp.int32, (C, C), 1)
    ones = jnp.ones((dk, C), BF16)

    a = la_ref[...]
    a1 = a.astype(BF16)
    r1 = a - a1.astype(F32)
    a2 = r1.astype(BF16)
    a3 = (r1 - a2.astype(F32)).astype(BF16)
    tri = (row >= col).astype(BF16)
    ball_ref[...] = _dot(tri, a1) + _dot(tri, a2) + _dot(tri, a3)

    for hd in range(nh):
        qb = p_ref[:, hd * dk:(hd + 1) * dk]
        kb = p_ref[:, k0 + hd * dk:k0 + (hd + 1) * dk]
        q = qb.astype(F32)
        k = kb.astype(F32)
        v = p_ref[:, v0 + hd * dv:v0 + (hd + 1) * dv]
        g = p_ref[:, g0 + hd * dv:g0 + (hd + 1) * dv]
        b_ref = ball_ref.at[:, hd * dk:(hd + 1) * dk]
        b = b_ref[...]

        scores = jnp.where(col == row, _dot_nt(qb, kb), 0.0)
        for r in range(1, GLA_SUB):
            valid = (rowl % GLA_SUB) >= r
            w = jnp.exp(jnp.minimum(b - pltpu.roll(b, r, axis=0), 0.0))
            t = jnp.where(valid, q * pltpu.roll(k, r, axis=0) * w, 0.0)
            scores = scores + jnp.where(col == row - r, _dot(t.astype(BF16), ones), 0.0)

        s = GLA_SUB
        while s < C:
            be = jnp.concatenate(
                [jnp.broadcast_to(b_ref[e:e + 1, :], (2 * s, dk)) for e in range(s - 1, C, 2 * s)], axis=0)
            w = jnp.exp(-jnp.abs(b - be))
            right = (rowl % (2 * s)) >= s
            qt = jnp.where(right, q * w, 0.0).astype(BF16)
            kt = jnp.where(right, 0.0, k * w).astype(BF16)
            sc = _dot_nt(qt, kt)
            if 2 * s < C:
                sc = jnp.where((row // (2 * s)) == (col // (2 * s)), sc, 0.0)
            scores = scores + sc
            s *= 2

        btot = b_ref[C - 1:C, :]
        st = st_ref[hd]
        o = _dot(scores.astype(BF16), v) + _dot_nt((q * jnp.exp(b)).astype(BF16), st.astype(BF16))
        kt = (k * jnp.exp(btot - b)).astype(BF16)
        st_ref[hd] = st * jnp.exp(btot) + _dot_tn(v, kt)
        on = _rms(o, hn_ref[hd:hd + 1, :]) * _silu(g.astype(F32))
        obuf_ref[:, hd * dv:(hd + 1) * dv] = on.astype(BF16)

    o_ref[...] = h_ref[...] + _dot(obuf_ref[...], wo_ref[...])


def _gla_core(p, la, h, head_norm, w_out, *, batch, seq):
    rows, d = h.shape
    n = p.shape[1]
    nh, dv = head_norm.shape
    hk = la.shape[1]
    dk = hk // nh
    C = GLA_CHUNK
    grid, (p_spec, la_spec, h_spec), o_spec = _recurrence_specs(C, (n, hk, d), d, batch=batch, seq=seq)
    return pl.pallas_call(
        functools.partial(_gla_core_kernel, dk=dk, dv=dv),
        grid=grid,
        in_specs=[p_spec, la_spec, h_spec, _const_spec((nh, dv)), _const_spec((nh * dv, d))],
        out_specs=o_spec,
        out_shape=jax.ShapeDtypeStruct((rows, d), F32),
        input_output_aliases={2: 0},
        scratch_shapes=[
            pltpu.VMEM((nh, dv, dk), F32),
            pltpu.VMEM((nh, dv, dk), F32),
            pltpu.VMEM((C, hk), F32),
            pltpu.VMEM((C, nh * dv), BF16),
        ],
        compiler_params=pltpu.CompilerParams(
            dimension_semantics=("arbitrary", "arbitrary"), vmem_limit_bytes=VMEM_LIMIT),
        name="gla_core",
    )(p, la, h, head_norm, w_out)


def kernel(x, meta_tokens, norm_ffn1, ffn1_w_in, ffn1_w_out, norm_mix, norm_ffn2, ffn2_w_in, ffn2_w_out, ret_w_in, ret_head_norm, ret_w_out, gla_w_in, gla_w_gate, gla_b_gate, gla_head_norm, gla_w_out, final_norm):
    batch, seq, d = x.shape
    depth = norm_ffn1.shape[0]
    tm = ROW_TILE
    assert meta_tokens.shape == (N_META, d) and seq % tm == 0 and tm >= LEAD

    tail = jnp.zeros((tm, d), x.dtype).at[LEAD_ZERO:LEAD].set(meta_tokens.astype(x.dtype))
    h = x.reshape(batch * seq, d)

    half = ret_w_in.shape[2] // 6 // RET_HEADS // 2
    assert half == LANES
    inv = 1.0 / (ROPE_BASE ** jnp.linspace(0.0, 1.0, half, dtype=F32))
    bf = lambda w: w.astype(BF16)

    for i in range(depth):
        j = i // 2
        h = _ffn(h, norm_ffn1[i], bf(ffn1_w_in[i]), bf(ffn1_w_out[i]), tm=tm,
                 tail=tail if i == 0 else None)
        if i % 2 == 0:
            p = _ret_proj(h, norm_mix[i], bf(ret_w_in[j]), inv, tm=tm, seq=seq)
            h = _ret_core(p, h, ret_head_norm[j], bf(ret_w_out[j]), batch=batch, seq=seq)
        else:
            p, la = _gla_proj(h, norm_mix[i], bf(gla_w_in[j]), bf(gla_w_gate[j]), gla_b_gate[j],
                              tm=tm, seq=seq)
            h = _gla_core(p, la, h, gla_head_norm[j], bf(gla_w_out[j]), batch=batch, seq=seq)
        h = _ffn(h, norm_ffn2[i], bf(ffn2_w_in[i]), bf(ffn2_w_out[i]), tm=tm,
                 final_w=final_norm if i == depth - 1 else None)
    return h.reshape(batch, seq, d)
```

```python
import functools
import math

import jax
import jax.numpy as jnp
from jax import lax
from jax.experimental import pallas as pl
from jax.experimental.pallas import tpu as pltpu

F32 = jnp.float32
BF16 = jnp.bfloat16

EPS = 1e-6
N_META = 16
LEAD = 256
LEAD_ZERO = LEAD - N_META
ROW_TILE = 512
ROPE_BASE = 10000.0
RET_HEADS = 4
GLA_HEADS = 4
GLA_RANK = 16
GLA_TAU = 16.0
RET_CHUNK = 256
GLA_CHUNK = 128
LANES = 128
VMEM_LIMIT = 56 * 1024 * 1024

RET_LOG_GAMMA = tuple(math.log1p(-2.0 ** (-5.0 - h)) for h in range(RET_HEADS))


def _const_spec(shape):
    nd = len(shape)
    return pl.BlockSpec(shape, lambda *_: (0,) * nd, pipeline_mode=pl.Buffered(1))


def _rms(x, w):
    ms = jnp.mean(x * x, axis=-1, keepdims=True)
    return x * lax.rsqrt(ms + EPS) * w


def _silu(x):
    return x * jax.nn.sigmoid(x)


def _dot(a, b):
    return jnp.dot(a, b, preferred_element_type=F32)


def _dot_nt(a, b):
    return lax.dot_general(a, b, (((1,), (1,)), ((), ())), preferred_element_type=F32)


def _dot_tn(a, b):
    return lax.dot_general(a, b, (((0,), (0,)), ((), ())), preferred_element_type=F32)


def _ffn_kernel(x_ref, *rest, n_x_tiles, first, final):
    rest = list(rest)
    tail_ref = rest.pop(0) if first else None
    nw_ref, wg_ref, wu_ref, wo_ref = rest[:4]
    fw_ref = rest[4] if final else None
    o_ref = rest[-1]
    x = x_ref[...]
    if first:
        x = jnp.where(pl.program_id(0) >= n_x_tiles, tail_ref[...], x)
    xn = _rms(x, nw_ref[...]).astype(BF16)
    g = _dot(xn, wg_ref[...])
    u = _dot(xn, wu_ref[...])
    hid = (_silu(g) * u).astype(BF16)
    y = x + 0.5 * _dot(hid, wo_ref[...])
    if final:
        y = _rms(y, fw_ref[...])
    o_ref[...] = y


def _ffn(h, norm_w, w_in, w_out, *, tm, tail=None, final_w=None):
    d = h.shape[1]
    dff = w_out.shape[0]
    first, final = tail is not None, final_w is not None
    n_x_tiles = (h.shape[0] if first else h.shape[0] - tm) // tm
    n_tiles = n_x_tiles if final else n_x_tiles + 1
    in_specs = [pl.BlockSpec((tm, d), lambda i: (jnp.minimum(i, n_x_tiles - 1), 0) if first else (i, 0))]
    args = [h]
    if first:
        in_specs.append(_const_spec((tm, d)))
        args.append(tail)
    in_specs += [
        _const_spec((1, d)),
        pl.BlockSpec((d, dff), lambda i: (0, 0), pipeline_mode=pl.Buffered(1)),
        pl.BlockSpec((d, dff), lambda i: (0, 1), pipeline_mode=pl.Buffered(1)),
        _const_spec((dff, d)),
    ]
    args += [norm_w.reshape(1, d), w_in, w_in, w_out]
    if final:
        in_specs.append(_const_spec((1, d)))
        args.append(final_w.reshape(1, d))
    return pl.pallas_call(
        functools.partial(_ffn_kernel, n_x_tiles=n_x_tiles, first=first, final=final),
        grid=(n_tiles,),
        in_specs=in_specs,
        out_specs=pl.BlockSpec((tm, d), lambda i: (i, 0)),
        out_shape=jax.ShapeDtypeStruct((n_tiles * tm, d), F32),
        compiler_params=pltpu.CompilerParams(
            dimension_semantics=("arbitrary",), vmem_limit_bytes=VMEM_LIMIT),
        name="ffn_first" if first else "ffn_final" if final else "ffn",
    )(*args)


def _tile_first_pos(i, tm, seq, n_x_tiles):
    return jnp.where(i >= n_x_tiles, -LEAD_ZERO, (i * tm) % seq + N_META)


def _ret_proj_kernel(h_ref, nw_ref, w_ref, inv_ref, o_ref, cosr_ref, sinr_ref, *, tm, seq, n_x_tiles):
    i = pl.program_id(0)
    d = h_ref.shape[1]
    half = LANES

    @pl.when(i == 0)
    def _():
        r = lax.broadcasted_iota(jnp.int32, (tm, half), 0).astype(F32)
        ang = r * inv_ref[...]
        cosr_ref[...] = jnp.cos(ang)
        sinr_ref[...] = jnp.sin(ang)

    base = _tile_first_pos(i, tm, seq, n_x_tiles).astype(F32)
    ang0 = base * inv_ref[...]
    cb, sb = jnp.cos(ang0), jnp.sin(ang0)
    cos = cb * cosr_ref[...] - sb * sinr_ref[...]
    sin = sb * cosr_ref[...] + cb * sinr_ref[...]

    hn = _rms(h_ref[...], nw_ref[...]).astype(BF16)
    n_chunks = w_ref.shape[1] // d
    for j in range(n_chunks):
        y = _dot(hn, w_ref[:, j * d:(j + 1) * d])
        if j < 2:
            scale = 1.0 if j == 0 else (2 * half) ** -0.5
            for hd in range(d // (2 * half)):
                lo = hd * 2 * half
                t1 = y[:, lo:lo + half]
                t2 = y[:, lo + half:lo + 2 * half]
                o_ref[:, j * d + lo:j * d + lo + half] = ((t1 * cos - t2 * sin) * scale).astype(BF16)
                o_ref[:, j * d + lo + half:j * d + lo + 2 * half] = ((t1 * sin + t2 * cos) * scale).astype(BF16)
        else:
            o_ref[:, j * d:(j + 1) * d] = y.astype(BF16)


def _ret_proj(h, norm_w, w_in, inv, *, tm, seq):
    rows, d = h.shape
    n = w_in.shape[1]
    return pl.pallas_call(
        functools.partial(_ret_proj_kernel, tm=tm, seq=seq, n_x_tiles=rows // tm - 1),
        grid=(rows // tm,),
        in_specs=[
            pl.BlockSpec((tm, d), lambda i: (i, 0)),
            _const_spec((1, d)),
            _const_spec((d, n)),
            _const_spec((1, LANES)),
        ],
        out_specs=pl.BlockSpec((tm, n), lambda i: (i, 0)),
        out_shape=jax.ShapeDtypeStruct((rows, n), BF16),
        scratch_shapes=[pltpu.VMEM((tm, LANES), F32), pltpu.VMEM((tm, LANES), F32)],
        compiler_params=pltpu.CompilerParams(
            dimension_semantics=("arbitrary",), vmem_limit_bytes=VMEM_LIMIT),
        name="ret_proj",
    )(h, norm_w.reshape(1, d), w_in, inv.reshape(1, LANES))


def _chunk_block(b, c, *, chunks_per_batch, lead_block):
    first = jnp.where(b == 0, lead_block, b * chunks_per_batch)
    return jnp.where(c == 0, first, b * chunks_per_batch + c - 1)


def _recurrence_specs(C, widths, d, *, batch, seq):
    cpb = seq // C
    assert seq % C == 0 and LEAD_ZERO // C == (LEAD - 1) // C
    lead_block = batch * cpb + LEAD_ZERO // C
    idx = lambda b, c: (_chunk_block(b, c, chunks_per_batch=cpb, lead_block=lead_block), 0)
    return (batch, cpb + 1), [pl.BlockSpec((C, w), idx) for w in widths], pl.BlockSpec((C, d), idx)


def _ret_core_kernel(p_ref, h_ref, hn_ref, wo_ref, o_ref,
                     s_ref, slead_ref, dec_ref, dq_ref, dk_ref, obuf_ref, *, dk, dv):
    b, c = pl.program_id(0), pl.program_id(1)
    C = p_ref.shape[0]
    nh = RET_HEADS
    k0, v0, g0 = nh * dk, 2 * nh * dk, 2 * nh * dk + nh * dv

    @pl.when((b == 0) & (c == 0))
    def _():
        s_ref[...] = jnp.zeros_like(s_ref)
        row = lax.broadcasted_iota(jnp.int32, (C, C), 0)
        col = lax.broadcasted_iota(jnp.int32, (C, C), 1)
        rel = (row - col).astype(F32)
        rowl = lax.broadcasted_iota(jnp.int32, (C, LANES), 0).astype(F32)
        for hd in range(nh):
            lg = RET_LOG_GAMMA[hd]
            dec_ref[hd] = jnp.where(rel >= 0, jnp.exp(lg * jnp.maximum(rel, 0.0)), 0.0)
            dq_ref[hd] = jnp.exp(lg * (rowl + 1.0))
            dk_ref[hd] = jnp.exp(lg * (C - 1.0 - rowl))

    @pl.when((b > 0) & (c == 0))
    def _():
        s_ref[...] = slead_ref[...]

    @pl.when((b == 0) | (c > 0))
    def _():
        for hd in range(nh):
            q = p_ref[:, hd * dk:(hd + 1) * dk]
            k = p_ref[:, k0 + hd * dk:k0 + (hd + 1) * dk]
            v = p_ref[:, v0 + hd * dv:v0 + (hd + 1) * dv]
            g = p_ref[:, g0 + hd * dv:g0 + (hd + 1) * dv]
            s = (_dot_nt(q, k) * dec_ref[hd]).astype(BF16)
            st = s_ref[hd]
            dq = jnp.concatenate([dq_ref[hd]] * (dv // LANES), axis=1)
            o = _dot(s, v) + dq * _dot(q, st.astype(BF16))
            dkc = jnp.concatenate([dk_ref[hd]] * (dk // LANES), axis=1)
            kd = (k.astype(F32) * dkc).astype(BF16)
            s_ref[hd] = st * math.exp(RET_LOG_GAMMA[hd] * C) + _dot_tn(kd, v)
            on = _rms(o, hn_ref[hd:hd + 1, :]) * _silu(g.astype(F32))
            obuf_ref[:, hd * dv:(hd + 1) * dv] = on.astype(BF16)
        o_ref[...] = h_ref[...] + _dot(obuf_ref[...], wo_ref[...])

    @pl.when((b == 0) & (c == 0))
    def _():
        slead_ref[...] = s_ref[...]


def _ret_core(p, h, head_norm, w_out, *, batch, seq):
    rows, d = h.shape
    n = p.shape[1]
    nh, dv = head_norm.shape
    dk = (n - 2 * nh * dv) // (2 * nh)
    C = RET_CHUNK
    grid, (p_spec, h_spec), o_spec = _recurrence_specs(C, (n, d), d, batch=batch, seq=seq)
    return pl.pallas_call(
        functools.partial(_ret_core_kernel, dk=dk, dv=dv),
        grid=grid,
        in_specs=[p_spec, h_spec, _const_spec((nh, dv)), _const_spec((nh * dv, d))],
        out_specs=o_spec,
        out_shape=jax.ShapeDtypeStruct((rows, d), F32),
        input_output_aliases={1: 0},
        scratch_shapes=[
            pltpu.VMEM((nh, dk, dv), F32),
            pltpu.VMEM((nh, dk, dv), F32),
            pltpu.VMEM((nh, C, C), F32),
            pltpu.VMEM((nh, C, LANES), F32),
            pltpu.VMEM((nh, C, LANES), F32),
            pltpu.VMEM((C, nh * dv), BF16),
        ],
        compiler_params=pltpu.CompilerParams(
            dimension_semantics=("arbitrary", "arbitrary"), vmem_limit_bytes=VMEM_LIMIT),
        name="ret_core",
    )(p, h, head_norm, w_out)


def _gla_proj_kernel(h_ref, nw_ref, w_ref, wg_ref, bg_ref, o_ref, la_ref, *, tm, seq, n_x_tiles, n_main):
    i = pl.program_id(0)
    hk = wg_ref.shape[1]
    dk = hk // GLA_HEADS
    hn = _rms(h_ref[...], nw_ref[...]).astype(BF16)
    step = 512
    for lo in range(0, n_main, step):
        y = _dot(hn, w_ref[:, lo:lo + step])
        if lo < hk:
            y = y * dk ** -0.5
        o_ref[:, lo:lo + step] = y.astype(BF16)
    z = _dot(hn, w_ref[:, n_main:n_main + GLA_RANK])
    xg = _dot(z.astype(BF16), wg_ref[...]) + bg_ref[...]
    ls = jnp.minimum(xg, 0.0) - jnp.log1p(jnp.exp(-jnp.abs(xg)))
    pos = _tile_first_pos(i, tm, seq, n_x_tiles) + lax.broadcasted_iota(jnp.int32, (tm, hk), 0)
    la_ref[...] = jnp.where(pos >= 0, ls * (1.0 / GLA_TAU), 0.0)


def _gla_proj(h, norm_w, w_in, w_gate, b_gate, *, tm, seq):
    rows, d = h.shape
    n = w_in.shape[1]
    n_main = n - GLA_RANK
    hk = w_gate.shape[1]
    return pl.pallas_call(
        functools.partial(_gla_proj_kernel, tm=tm, seq=seq, n_x_tiles=rows // tm - 1, n_main=n_main),
        grid=(rows // tm,),
        in_specs=[
            pl.BlockSpec((tm, d), lambda i: (i, 0)),
            _const_spec((1, d)),
            _const_spec((d, n)),
            _const_spec((GLA_RANK, hk)),
            _const_spec((1, hk)),
        ],
        out_specs=[pl.BlockSpec((tm, n_main), lambda i: (i, 0)),
                   pl.BlockSpec((tm, hk), lambda i: (i, 0))],
        out_shape=[jax.ShapeDtypeStruct((rows, n_main), BF16),
                   jax.ShapeDtypeStruct((rows, hk), F32)],
        compiler_params=pltpu.CompilerParams(
            dimension_semantics=("arbitrary",), vmem_limit_bytes=VMEM_LIMIT),
        name="gla_proj",
    )(h, norm_w.reshape(1, d), w_in, w_gate, b_gate.reshape(1, hk))


def _gla_core_kernel(p_ref, la_ref, h_ref, hn_ref, wo_ref, o_ref,
                     st_ref, stlead_ref, ball_ref, obuf_ref, *, dk, dv):
    bi, c = pl.program_id(0), pl.program_id(1)

    @pl.when((bi == 0) & (c == 0))
    def _():
        st_ref[...] = jnp.zeros_like(st_ref)

    @pl.when((bi > 0) & (c == 0))
    def _():
        st_ref[...] = stlead_ref[...]

    @pl.when((bi == 0) | (c > 0))
    def _():
        _gla_chunk(p_ref, la_ref, h_ref, hn_ref, wo_ref, o_ref, st_ref, ball_ref, obuf_ref, dk=dk, dv=dv)

    @pl.when((bi == 0) & (c == 0))
    def _():
        stlead_ref[...] = st_ref[...]


def _left_block_end_rows(b_ref, s):
    C, dk = b_ref.shape
    sub = 8
    bcast = lambda e, n: jnp.broadcast_to(b_ref[e:e + 1, :], (n, dk))
    if 2 * s >= sub:
        n = max(2 * s, sub)
        return jnp.concatenate([bcast(e, n) for e in range(s - 1, C, n)], axis=0)
    r = lax.broadcasted_iota(jnp.int32, (sub, dk), 0)
    tiles = []
    for t0 in range(0, C, sub):
        tile = bcast(t0 + s - 1, sub)
        for blk in range(2 * s, sub, 2 * s):
            tile = jnp.where(r >= blk, bcast(t0 + blk + s - 1, sub), tile)
        tiles.append(tile)
    return jnp.concatenate(tiles, axis=0)


def _gla_chunk(p_ref, la_ref, h_ref, hn_ref, wo_ref, o_ref, st_ref, ball_ref, obuf_ref, *, dk, dv):
    C = p_ref.shape[0]
    nh = GLA_HEADS
    k0, v0, g0 = nh * dk, 2 * nh * dk, 2 * nh * dk + nh * dv

    rowl = lax.broadcasted_iota(jnp.int32, (C, dk), 0)
    row = lax.broadcasted_iota(jnp.int32, (C, C), 0)
    col = lax.broadcasted_iota(jnp.int32, (C, C), 1)

    differ = jnp.bitwise_xor(row, col)
    level = jnp.full((C, C), -1, jnp.int32)
    for lv in range(C.bit_length() - 1):
        level = jnp.where((row > col) & (differ >= (1 << lv)), lv, level)

    a_all = la_ref[...]
    a1 = a_all.astype(BF16)
    r1 = a_all - a1.astype(F32)
    a2 = r1.astype(BF16)
    a3 = (r1 - a2.astype(F32)).astype(BF16)
    tri = (row >= col).astype(BF16)
    ball_ref[...] = _dot(tri, a1) + _dot(tri, a2) + _dot(tri, a3)

    for hd in range(nh):
        qb = p_ref[:, hd * dk:(hd + 1) * dk]
        kb = p_ref[:, k0 + hd * dk:k0 + (hd + 1) * dk]
        q = qb.astype(F32)
        k = kb.astype(F32)
        v = p_ref[:, v0 + hd * dv:v0 + (hd + 1) * dv]
        g = p_ref[:, g0 + hd * dv:g0 + (hd + 1) * dv]
        b_ref = ball_ref.at[:, hd * dk:(hd + 1) * dk]
        b = b_ref[...]

        scores = jnp.where(col == row, _dot_nt(qb, kb), 0.0)
        for lv in range(C.bit_length() - 1):
            s = 1 << lv
            right = (rowl & s) != 0
            if s == 1:
                w = jnp.where(right, jnp.exp(la_ref[:, hd * dk:(hd + 1) * dk]), 1.0)
            else:
                w = jnp.exp(-jnp.abs(b - _left_block_end_rows(b_ref, s)))
            z = (jnp.where(right, q, k) * w).astype(BF16)
            scores = jnp.where(level == lv, _dot_nt(z, z), scores)

        btot = b_ref[C - 1:C, :]
        st = st_ref[hd]
        o = _dot(scores.astype(BF16), v) + _dot_nt((q * jnp.exp(b)).astype(BF16), st.astype(BF16))
        kt = (k * jnp.exp(btot - b)).astype(BF16)
        st_ref[hd] = st * jnp.exp(btot) + _dot_tn(v, kt)
        on = _rms(o, hn_ref[hd:hd + 1, :]) * _silu(g.astype(F32))
        obuf_ref[:, hd * dv:(hd + 1) * dv] = on.astype(BF16)

    o_ref[...] = h_ref[...] + _dot(obuf_ref[...], wo_ref[...])


def _gla_core(p, la, h, head_norm, w_out, *, batch, seq):
    rows, d = h.shape
    n = p.shape[1]
    nh, dv = head_norm.shape
    hk = la.shape[1]
    dk = hk // nh
    C = GLA_CHUNK
    grid, (p_spec, la_spec, h_spec), o_spec = _recurrence_specs(C, (n, hk, d), d, batch=batch, seq=seq)
    return pl.pallas_call(
        functools.partial(_gla_core_kernel, dk=dk, dv=dv),
        grid=grid,
        in_specs=[p_spec, la_spec, h_spec, _const_spec((nh, dv)), _const_spec((nh * dv, d))],
        out_specs=o_spec,
        out_shape=jax.ShapeDtypeStruct((rows, d), F32),
        input_output_aliases={2: 0},
        scratch_shapes=[
            pltpu.VMEM((nh, dv, dk), F32),
            pltpu.VMEM((nh, dv, dk), F32),
            pltpu.VMEM((C, hk), F32),
            pltpu.VMEM((C, nh * dv), BF16),
        ],
        compiler_params=pltpu.CompilerParams(
            dimension_semantics=("arbitrary", "arbitrary"), vmem_limit_bytes=VMEM_LIMIT),
        name="gla_core",
    )(p, la, h, head_norm, w_out)


def kernel(x, meta_tokens, norm_ffn1, ffn1_w_in, ffn1_w_out, norm_mix, norm_ffn2, ffn2_w_in, ffn2_w_out, ret_w_in, ret_head_norm, ret_w_out, gla_w_in, gla_w_gate, gla_b_gate, gla_head_norm, gla_w_out, final_norm):
    batch, seq, d = x.shape
    depth = norm_ffn1.shape[0]
    tm = ROW_TILE
    assert meta_tokens.shape == (N_META, d) and seq % tm == 0 and tm >= LEAD

    tail = jnp.zeros((tm, d), x.dtype).at[LEAD_ZERO:LEAD].set(meta_tokens.astype(x.dtype))
    h = x.reshape(batch * seq, d)

    half = ret_w_in.shape[2] // 6 // RET_HEADS // 2
    assert half == LANES
    inv = 1.0 / (ROPE_BASE ** jnp.linspace(0.0, 1.0, half, dtype=F32))
    bf = lambda w: w.astype(BF16)

    for i in range(depth):
        j = i // 2
        h = _ffn(h, norm_ffn1[i], bf(ffn1_w_in[i]), bf(ffn1_w_out[i]), tm=tm,
                 tail=tail if i == 0 else None)
        if i % 2 == 0:
            p = _ret_proj(h, norm_mix[i], bf(ret_w_in[j]), inv, tm=tm, seq=seq)
            h = _ret_core(p, h, ret_head_norm[j], bf(ret_w_out[j]), batch=batch, seq=seq)
        else:
            p, la = _gla_proj(h, norm_mix[i], bf(gla_w_in[j]), bf(gla_w_gate[j]), gla_b_gate[j],
                              tm=tm, seq=seq)
            h = _gla_core(p, la, h, gla_head_norm[j], bf(gla_w_out[j]), batch=batch, seq=seq)
        h = _ffn(h, norm_ffn2[i], bf(ffn2_w_in[i]), bf(ffn2_w_out[i]), tm=tm,
                 final_w=final_norm if i == depth - 1 else None)
    return h.reshape(batch, seq, d)
```

```python
import functools
import math

import jax
import jax.numpy as jnp
from jax import lax
from jax.experimental import pallas as pl
from jax.experimental.pallas import tpu as pltpu

F32 = jnp.float32
BF16 = jnp.bfloat16

EPS = 1e-6
N_META = 16
LEAD = 256
LEAD_ZERO = LEAD - N_META
ROW_TILE = 512
ROPE_BASE = 10000.0
RET_HEADS = 4
GLA_HEADS = 4
GLA_RANK = 16
GLA_TAU = 16.0
RET_CHUNK = 256
GLA_CHUNK = 128
LANES = 128
BF16_SUBLANES = 16
VMEM_LIMIT = 56 * 1024 * 1024

RET_LOG_GAMMA = tuple(math.log1p(-2.0 ** (-5.0 - h)) for h in range(RET_HEADS))


def _const_spec(shape):
    nd = len(shape)
    return pl.BlockSpec(shape, lambda *_: (0,) * nd, pipeline_mode=pl.Buffered(1))


def _rms(x, w):
    ms = jnp.mean(x * x, axis=-1, keepdims=True)
    return x * lax.rsqrt(ms + EPS) * w


def _silu(x):
    return x * jax.nn.sigmoid(x)


def _dot(a, b):
    return jnp.dot(a, b, preferred_element_type=F32)


def _dot_nt(a, b):
    return lax.dot_general(a, b, (((1,), (1,)), ((), ())), preferred_element_type=F32)


def _dot_tn(a, b):
    return lax.dot_general(a, b, (((0,), (0,)), ((), ())), preferred_element_type=F32)


def _cast_blocks(rows, n_steps):
    units = rows // BF16_SUBLANES
    assert rows % BF16_SUBLANES == 0
    return max(n for n in range(1, min(units, n_steps) + 1) if units % n == 0)


def _call(body, *, name, grid, in_specs, args, out_specs, out_shapes, scratch_shapes=(), aliases=None,
          casts=()):
    n_in, n_out, n_cast = len(args), len(out_shapes), len(casts)
    n_steps = math.prod(grid)
    linear = (lambda i: i) if len(grid) == 1 else (lambda b, c: b * grid[1] + c)
    in_specs, out_specs, out_shapes, args = list(in_specs), list(out_specs), list(out_shapes), list(args)
    for stack, layer in casts:
        _, rows, cols = stack.shape
        nb = _cast_blocks(rows, n_steps)
        rb = rows // nb
        blk = lambda *g, nb=nb: jnp.minimum(linear(*g), nb - 1)
        in_specs.append(pl.BlockSpec((None, rb, cols), lambda *g, blk=blk, layer=layer: (layer, blk(*g), 0)))
        out_specs.append(pl.BlockSpec((rb, cols), lambda *g, blk=blk: (blk(*g), 0)))
        out_shapes.append(jax.ShapeDtypeStruct((rows, cols), BF16))
        args.append(stack)

    def kern(*refs):
        ins, rest = refs[:n_in], refs[n_in:]
        cast_in, rest = rest[:n_cast], rest[n_cast:]
        outs, rest = rest[:n_out], rest[n_out:]
        cast_out, scratch = rest[:n_cast], rest[n_cast:]
        for src, dst in zip(cast_in, cast_out):
            dst[...] = src[...].astype(BF16)
        body(*ins, *outs, *scratch)

    res = pl.pallas_call(
        kern, grid=grid, in_specs=in_specs, out_specs=out_specs, out_shape=out_shapes,
        scratch_shapes=list(scratch_shapes), input_output_aliases=aliases or {},
        compiler_params=pltpu.CompilerParams(
            dimension_semantics=("arbitrary",) * len(grid), vmem_limit_bytes=VMEM_LIMIT),
        name=name,
    )(*args)
    return res[:n_out], res[n_out:]


def _cast_weights(casts, *, n_steps=8):
    _, out = _call(lambda: None, name="cast_weights", grid=(n_steps,), in_specs=[], args=[],
                   out_specs=[], out_shapes=[], casts=casts)
    return out


def _ffn_kernel(x_ref, *rest, n_x_tiles, first, final):
    rest = list(rest)
    tail_ref = rest.pop(0) if first else None
    nw_ref, wg_ref, wu_ref, wo_ref = rest[:4]
    fw_ref = rest[4] if final else None
    o_ref = rest[-1]
    x = x_ref[...]
    if first:
        x = jnp.where(pl.program_id(0) >= n_x_tiles, tail_ref[...], x)
    xn = _rms(x, nw_ref[...]).astype(BF16)
    g = _dot(xn, wg_ref[...])
    u = _dot(xn, wu_ref[...])
    hid = (_silu(g) * u).astype(BF16)
    y = x + 0.5 * _dot(hid, wo_ref[...])
    if final:
        y = _rms(y, fw_ref[...])
    o_ref[...] = y


def _ffn(h, norm_w, w_in, w_out, *, tm, tail=None, final_w=None, casts=()):
    d = h.shape[1]
    dff = w_out.shape[0]
    first, final = tail is not None, final_w is not None
    n_x_tiles = (h.shape[0] if first else h.shape[0] - tm) // tm
    n_tiles = n_x_tiles if final else n_x_tiles + 1
    in_specs = [pl.BlockSpec((tm, d), lambda i: (jnp.minimum(i, n_x_tiles - 1), 0) if first else (i, 0))]
    args = [h]
    if first:
        in_specs.append(_const_spec((tm, d)))
        args.append(tail)
    in_specs += [
        _const_spec((1, d)),
        pl.BlockSpec((d, dff), lambda i: (0, 0), pipeline_mode=pl.Buffered(1)),
        pl.BlockSpec((d, dff), lambda i: (0, 1), pipeline_mode=pl.Buffered(1)),
        _const_spec((dff, d)),
    ]
    args += [norm_w.reshape(1, d), w_in, w_in, w_out]
    if final:
        in_specs.append(_const_spec((1, d)))
        args.append(final_w.reshape(1, d))
    (out,), cast = _call(
        functools.partial(_ffn_kernel, n_x_tiles=n_x_tiles, first=first, final=final),
        name="ffn_first" if first else "ffn_final" if final else "ffn",
        grid=(n_tiles,), in_specs=in_specs, args=args,
        out_specs=[pl.BlockSpec((tm, d), lambda i: (i, 0))],
        out_shapes=[jax.ShapeDtypeStruct((n_tiles * tm, d), F32)], casts=casts)
    return out, cast


def _tile_first_pos(i, tm, seq, n_x_tiles):
    return jnp.where(i >= n_x_tiles, -LEAD_ZERO, (i * tm) % seq + N_META)


def _ret_proj_kernel(h_ref, nw_ref, w_ref, inv_ref, o_ref, cosr_ref, sinr_ref, *, tm, seq, n_x_tiles):
    i = pl.program_id(0)
    d = h_ref.shape[1]
    half = LANES

    @pl.when(i == 0)
    def _():
        r = lax.broadcasted_iota(jnp.int32, (tm, half), 0).astype(F32)
        ang = r * inv_ref[...]
        cosr_ref[...] = jnp.cos(ang)
        sinr_ref[...] = jnp.sin(ang)

    base = _tile_first_pos(i, tm, seq, n_x_tiles).astype(F32)
    ang0 = base * inv_ref[...]
    cb, sb = jnp.cos(ang0), jnp.sin(ang0)
    cos = cb * cosr_ref[...] - sb * sinr_ref[...]
    sin = sb * cosr_ref[...] + cb * sinr_ref[...]

    hn = _rms(h_ref[...], nw_ref[...]).astype(BF16)
    n_chunks = w_ref.shape[1] // d
    for j in range(n_chunks):
        y = _dot(hn, w_ref[:, j * d:(j + 1) * d])
        if j < 2:
            scale = 1.0 if j == 0 else (2 * half) ** -0.5
            for hd in range(d // (2 * half)):
                lo = hd * 2 * half
                t1 = y[:, lo:lo + half]
                t2 = y[:, lo + half:lo + 2 * half]
                o_ref[:, j * d + lo:j * d + lo + half] = ((t1 * cos - t2 * sin) * scale).astype(BF16)
                o_ref[:, j * d + lo + half:j * d + lo + 2 * half] = ((t1 * sin + t2 * cos) * scale).astype(BF16)
        else:
            o_ref[:, j * d:(j + 1) * d] = y.astype(BF16)


def _ret_proj(h, norm_w, w_in, inv, *, tm, seq, casts=()):
    rows, d = h.shape
    n = w_in.shape[1]
    (p,), cast = _call(
        functools.partial(_ret_proj_kernel, tm=tm, seq=seq, n_x_tiles=rows // tm - 1),
        name="ret_proj", grid=(rows // tm,),
        in_specs=[pl.BlockSpec((tm, d), lambda i: (i, 0)), _const_spec((1, d)), _const_spec((d, n)),
                  _const_spec((1, LANES))],
        args=[h, norm_w.reshape(1, d), w_in, inv.reshape(1, LANES)],
        out_specs=[pl.BlockSpec((tm, n), lambda i: (i, 0))],
        out_shapes=[jax.ShapeDtypeStruct((rows, n), BF16)],
        scratch_shapes=[pltpu.VMEM((tm, LANES), F32), pltpu.VMEM((tm, LANES), F32)], casts=casts)
    return p, cast


def _chunk_block(b, c, *, chunks_per_batch, lead_block):
    first = jnp.where(b == 0, lead_block, b * chunks_per_batch)
    return jnp.where(c == 0, first, b * chunks_per_batch + c - 1)


def _recurrence_specs(C, widths, d, *, batch, seq):
    cpb = seq // C
    assert seq % C == 0 and LEAD_ZERO // C == (LEAD - 1) // C
    lead_block = batch * cpb + LEAD_ZERO // C
    idx = lambda b, c: (_chunk_block(b, c, chunks_per_batch=cpb, lead_block=lead_block), 0)
    return (batch, cpb + 1), [pl.BlockSpec((C, w), idx) for w in widths], pl.BlockSpec((C, d), idx)


def _ret_core_kernel(p_ref, h_ref, hn_ref, wo_ref, o_ref,
                     s_ref, slead_ref, dec_ref, dq_ref, dk_ref, obuf_ref, *, dk, dv):
    b, c = pl.program_id(0), pl.program_id(1)
    C = p_ref.shape[0]
    nh = RET_HEADS
    k0, v0, g0 = nh * dk, 2 * nh * dk, 2 * nh * dk + nh * dv

    @pl.when((b == 0) & (c == 0))
    def _():
        s_ref[...] = jnp.zeros_like(s_ref)
        row = lax.broadcasted_iota(jnp.int32, (C, C), 0)
        col = lax.broadcasted_iota(jnp.int32, (C, C), 1)
        rel = (row - col).astype(F32)
        rowl = lax.broadcasted_iota(jnp.int32, (C, LANES), 0).astype(F32)
        for hd in range(nh):
            lg = RET_LOG_GAMMA[hd]
            dec_ref[hd] = jnp.where(rel >= 0, jnp.exp(lg * jnp.maximum(rel, 0.0)), 0.0)
            dq_ref[hd] = jnp.exp(lg * (rowl + 1.0))
            dk_ref[hd] = jnp.exp(lg * (C - 1.0 - rowl))

    @pl.when((b > 0) & (c == 0))
    def _():
        s_ref[...] = slead_ref[...]

    @pl.when((b == 0) | (c > 0))
    def _():
        for hd in range(nh):
            q = p_ref[:, hd * dk:(hd + 1) * dk]
            k = p_ref[:, k0 + hd * dk:k0 + (hd + 1) * dk]
            v = p_ref[:, v0 + hd * dv:v0 + (hd + 1) * dv]
            g = p_ref[:, g0 + hd * dv:g0 + (hd + 1) * dv]
            s = (_dot_nt(q, k) * dec_ref[hd]).astype(BF16)
            st = s_ref[hd]
            dq = jnp.concatenate([dq_ref[hd]] * (dv // LANES), axis=1)
            o = _dot(s, v) + dq * _dot(q, st.astype(BF16))
            dkc = jnp.concatenate([dk_ref[hd]] * (dk // LANES), axis=1)
            kd = (k.astype(F32) * dkc).astype(BF16)
            s_ref[hd] = st * math.exp(RET_LOG_GAMMA[hd] * C) + _dot_tn(kd, v)
            on = _rms(o, hn_ref[hd:hd + 1, :]) * _silu(g.astype(F32))
            obuf_ref[:, hd * dv:(hd + 1) * dv] = on.astype(BF16)
        o_ref[...] = h_ref[...] + _dot(obuf_ref[...], wo_ref[...])

    @pl.when((b == 0) & (c == 0))
    def _():
        slead_ref[...] = s_ref[...]


def _ret_core(p, h, head_norm, w_out, *, batch, seq, casts=()):
    rows, d = h.shape
    n = p.shape[1]
    nh, dv = head_norm.shape
    dk = (n - 2 * nh * dv) // (2 * nh)
    C = RET_CHUNK
    grid, (p_spec, h_spec), o_spec = _recurrence_specs(C, (n, d), d, batch=batch, seq=seq)
    (out,), cast = _call(
        functools.partial(_ret_core_kernel, dk=dk, dv=dv), name="ret_core", grid=grid,
        in_specs=[p_spec, h_spec, _const_spec((nh, dv)), _const_spec((nh * dv, d))],
        args=[p, h, head_norm, w_out],
        out_specs=[o_spec], out_shapes=[jax.ShapeDtypeStruct((rows, d), F32)],
        aliases={1: 0},
        scratch_shapes=[
            pltpu.VMEM((nh, dk, dv), F32),
            pltpu.VMEM((nh, dk, dv), F32),
            pltpu.VMEM((nh, C, C), F32),
            pltpu.VMEM((nh, C, LANES), F32),
            pltpu.VMEM((nh, C, LANES), F32),
            pltpu.VMEM((C, nh * dv), BF16),
        ], casts=casts)
    return out, cast


def _gla_proj_kernel(h_ref, nw_ref, w_ref, wg_ref, bg_ref, o_ref, la_ref, *, tm, seq, n_x_tiles, n_main):
    i = pl.program_id(0)
    hk = wg_ref.shape[1]
    dk = hk // GLA_HEADS
    hn = _rms(h_ref[...], nw_ref[...]).astype(BF16)
    step = 512
    for lo in range(0, n_main, step):
        y = _dot(hn, w_ref[:, lo:lo + step])
        if lo < hk:
            y = y * dk ** -0.5
        o_ref[:, lo:lo + step] = y.astype(BF16)
    z = _dot(hn, w_ref[:, n_main:n_main + GLA_RANK])
    xg = _dot(z.astype(BF16), wg_ref[...]) + bg_ref[...]
    ls = jnp.minimum(xg, 0.0) - jnp.log1p(jnp.exp(-jnp.abs(xg)))
    pos = _tile_first_pos(i, tm, seq, n_x_tiles) + lax.broadcasted_iota(jnp.int32, (tm, hk), 0)
    la_ref[...] = jnp.where(pos >= 0, ls * (1.0 / GLA_TAU), 0.0)


def _gla_proj(h, norm_w, w_in, w_gate, b_gate, *, tm, seq, casts=()):
    rows, d = h.shape
    n = w_in.shape[1]
    n_main = n - GLA_RANK
    hk = w_gate.shape[1]
    (p, la), cast = _call(
        functools.partial(_gla_proj_kernel, tm=tm, seq=seq, n_x_tiles=rows // tm - 1, n_main=n_main),
        name="gla_proj", grid=(rows // tm,),
        in_specs=[pl.BlockSpec((tm, d), lambda i: (i, 0)), _const_spec((1, d)), _const_spec((d, n)),
                  _const_spec((GLA_RANK, hk)), _const_spec((1, hk))],
        args=[h, norm_w.reshape(1, d), w_in, w_gate, b_gate.reshape(1, hk)],
        out_specs=[pl.BlockSpec((tm, n_main), lambda i: (i, 0)), pl.BlockSpec((tm, hk), lambda i: (i, 0))],
        out_shapes=[jax.ShapeDtypeStruct((rows, n_main), BF16), jax.ShapeDtypeStruct((rows, hk), F32)],
        casts=casts)
    return p, la, cast


def _gla_core_kernel(p_ref, la_ref, h_ref, hn_ref, wo_ref, o_ref,
                     st_ref, stlead_ref, ball_ref, obuf_ref, *, dk, dv):
    bi, c = pl.program_id(0), pl.program_id(1)

    @pl.when((bi == 0) & (c == 0))
    def _():
        st_ref[...] = jnp.zeros_like(st_ref)

    @pl.when((bi > 0) & (c == 0))
    def _():
        st_ref[...] = stlead_ref[...]

    @pl.when((bi == 0) | (c > 0))
    def _():
        _gla_chunk(p_ref, la_ref, h_ref, hn_ref, wo_ref, o_ref, st_ref, ball_ref, obuf_ref, dk=dk, dv=dv)

    @pl.when((bi == 0) & (c == 0))
    def _():
        stlead_ref[...] = st_ref[...]


def _left_block_end_rows(b_ref, s):
    C, dk = b_ref.shape
    sub = 8
    bcast = lambda e, n: jnp.broadcast_to(b_ref[e:e + 1, :], (n, dk))
    if 2 * s >= sub:
        n = max(2 * s, sub)
        return jnp.concatenate([bcast(e, n) for e in range(s - 1, C, n)], axis=0)
    r = lax.broadcasted_iota(jnp.int32, (sub, dk), 0)
    tiles = []
    for t0 in range(0, C, sub):
        tile = bcast(t0 + s - 1, sub)
        for blk in range(2 * s, sub, 2 * s):
            tile = jnp.where(r >= blk, bcast(t0 + blk + s - 1, sub), tile)
        tiles.append(tile)
    return jnp.concatenate(tiles, axis=0)


def _gla_chunk(p_ref, la_ref, h_ref, hn_ref, wo_ref, o_ref, st_ref, ball_ref, obuf_ref, *, dk, dv):
    C = p_ref.shape[0]
    nh = GLA_HEADS
    k0, v0, g0 = nh * dk, 2 * nh * dk, 2 * nh * dk + nh * dv

    rowl = lax.broadcasted_iota(jnp.int32, (C, dk), 0)
    row = lax.broadcasted_iota(jnp.int32, (C, C), 0)
    col = lax.broadcasted_iota(jnp.int32, (C, C), 1)

    differ = jnp.bitwise_xor(row, col)
    level = jnp.full((C, C), -1, jnp.int32)
    for lv in range(C.bit_length() - 1):
        level = jnp.where((row > col) & (differ >= (1 << lv)), lv, level)

    a_all = la_ref[...]
    a1 = a_all.astype(BF16)
    r1 = a_all - a1.astype(F32)
    a2 = r1.astype(BF16)
    a3 = (r1 - a2.astype(F32)).astype(BF16)
    tri = (row >= col).astype(BF16)
    ball_ref[...] = _dot(tri, a1) + _dot(tri, a2) + _dot(tri, a3)

    for hd in range(nh):
        qb = p_ref[:, hd * dk:(hd + 1) * dk]
        kb = p_ref[:, k0 + hd * dk:k0 + (hd + 1) * dk]
        q = qb.astype(F32)
        k = kb.astype(F32)
        v = p_ref[:, v0 + hd * dv:v0 + (hd + 1) * dv]
        g = p_ref[:, g0 + hd * dv:g0 + (hd + 1) * dv]
        b_ref = ball_ref.at[:, hd * dk:(hd + 1) * dk]
        b = b_ref[...]

        scores = jnp.where(col == row, _dot_nt(qb, kb), 0.0)
        for lv in range(C.bit_length() - 1):
            s = 1 << lv
            right = (rowl & s) != 0
            if s == 1:
                w = jnp.where(right, jnp.exp(la_ref[:, hd * dk:(hd + 1) * dk]), 1.0)
            else:
                w = jnp.exp(-jnp.abs(b - _left_block_end_rows(b_ref, s)))
            z = (jnp.where(right, q, k) * w).astype(BF16)
            scores = jnp.where(level == lv, _dot_nt(z, z), scores)

        btot = b_ref[C - 1:C, :]
        st = st_ref[hd]
        o = _dot(scores.astype(BF16), v) + _dot_nt((q * jnp.exp(b)).astype(BF16), st.astype(BF16))
        kt = (k * jnp.exp(btot - b)).astype(BF16)
        st_ref[hd] = st * jnp.exp(btot) + _dot_tn(v, kt)
        on = _rms(o, hn_ref[hd:hd + 1, :]) * _silu(g.astype(F32))
        obuf_ref[:, hd * dv:(hd + 1) * dv] = on.astype(BF16)

    o_ref[...] = h_ref[...] + _dot(obuf_ref[...], wo_ref[...])


def _gla_core(p, la, h, head_norm, w_out, *, batch, seq, casts=()):
    rows, d = h.shape
    n = p.shape[1]
    nh, dv = head_norm.shape
    hk = la.shape[1]
    dk = hk // nh
    C = GLA_CHUNK
    grid, (p_spec, la_spec, h_spec), o_spec = _recurrence_specs(C, (n, hk, d), d, batch=batch, seq=seq)
    (out,), cast = _call(
        functools.partial(_gla_core_kernel, dk=dk, dv=dv), name="gla_core", grid=grid,
        in_specs=[p_spec, la_spec, h_spec, _const_spec((nh, dv)), _const_spec((nh * dv, d))],
        args=[p, la, h, head_norm, w_out],
        out_specs=[o_spec], out_shapes=[jax.ShapeDtypeStruct((rows, d), F32)],
        aliases={2: 0},
        scratch_shapes=[
            pltpu.VMEM((nh, dv, dk), F32),
            pltpu.VMEM((nh, dv, dk), F32),
            pltpu.VMEM((C, hk), F32),
            pltpu.VMEM((C, nh * dv), BF16),
        ], casts=casts)
    return out, cast


def kernel(x, meta_tokens, norm_ffn1, ffn1_w_in, ffn1_w_out, norm_mix, norm_ffn2, ffn2_w_in, ffn2_w_out, ret_w_in, ret_head_norm, ret_w_out, gla_w_in, gla_w_gate, gla_b_gate, gla_head_norm, gla_w_out, final_norm):
    batch, seq, d = x.shape
    depth = norm_ffn1.shape[0]
    tm = ROW_TILE
    assert meta_tokens.shape == (N_META, d) and seq % tm == 0 and tm >= LEAD

    tail = jnp.zeros((tm, d), x.dtype).at[LEAD_ZERO:LEAD].set(meta_tokens.astype(x.dtype))
    h = x.reshape(batch * seq, d)

    half = ret_w_in.shape[2] // 6 // RET_HEADS // 2
    assert half == LANES
    inv = 1.0 / (ROPE_BASE ** jnp.linspace(0.0, 1.0, half, dtype=F32))

    stages = []
    for i in range(depth):
        j = i // 2
        stages.append([(ffn1_w_in, i), (ffn1_w_out, i)])
        if i % 2 == 0:
            stages += [[(ret_w_in, j)], [(ret_w_out, j)]]
        else:
            stages += [[(gla_w_in, j), (gla_w_gate, j)], [(gla_w_out, j)]]
        stages.append([(ffn2_w_in, i), (ffn2_w_out, i)])
    stages.append([])
    nxt = iter(stages[1:])

    w = _cast_weights(stages[0])
    for i in range(depth):
        j = i // 2
        h, w = _ffn(h, norm_ffn1[i], *w, tm=tm, tail=tail if i == 0 else None, casts=next(nxt))
        if i % 2 == 0:
            p, w = _ret_proj(h, norm_mix[i], *w, inv, tm=tm, seq=seq, casts=next(nxt))
            h, w = _ret_core(p, h, ret_head_norm[j], *w, batch=batch, seq=seq, casts=next(nxt))
        else:
            p, la, w = _gla_proj(h, norm_mix[i], *w, gla_b_gate[j], tm=tm, seq=seq, casts=next(nxt))
            h, w = _gla_core(p, la, h, gla_head_norm[j], *w, batch=batch, seq=seq, casts=next(nxt))
        h, w = _ffn(h, norm_ffn2[i], *w, tm=tm, final_w=final_norm if i == depth - 1 else None,
                    casts=next(nxt))
    return h.reshape(batch, seq, d)
```

```python
import functools
import math

import jax
import jax.numpy as jnp
from jax import lax
from jax.experimental import pallas as pl
from jax.experimental.pallas import tpu as pltpu

F32 = jnp.float32
BF16 = jnp.bfloat16

EPS = 1e-6
N_META = 16
LEAD = 256
LEAD_ZERO = LEAD - N_META
ROW_TILE = 512
ROPE_BASE = 10000.0
RET_HEADS = 4
GLA_HEADS = 4
GLA_RANK = 16
GLA_TAU = 16.0
RET_CHUNK = 256
GLA_CHUNK = 128
LANES = 128
BF16_SUBLANES = 16
VMEM_LIMIT = 56 * 1024 * 1024

RET_LOG_GAMMA = tuple(math.log1p(-2.0 ** (-5.0 - h)) for h in range(RET_HEADS))


def _const_spec(shape):
    nd = len(shape)
    return pl.BlockSpec(shape, lambda *_: (0,) * nd, pipeline_mode=pl.Buffered(1))


def _rms(x, w):
    ms = jnp.mean(x * x, axis=-1, keepdims=True)
    return x * lax.rsqrt(ms + EPS) * w


def _silu(x):
    return x * jax.nn.sigmoid(x)


def _dot(a, b):
    return jnp.dot(a, b, preferred_element_type=F32)


def _dot_nt(a, b):
    return lax.dot_general(a, b, (((1,), (1,)), ((), ())), preferred_element_type=F32)


def _dot_tn(a, b):
    return lax.dot_general(a, b, (((0,), (0,)), ((), ())), preferred_element_type=F32)


def _cast_blocks(rows, n_steps):
    units = rows // BF16_SUBLANES
    assert rows % BF16_SUBLANES == 0
    return max(n for n in range(1, min(units, n_steps) + 1) if units % n == 0)


def _call(body, *, name, grid, in_specs, args, out_specs, out_shapes, scratch_shapes=(), aliases=None,
          casts=()):
    n_in, n_out, n_cast = len(args), len(out_shapes), len(casts)
    n_steps = math.prod(grid)
    linear = (lambda i: i) if len(grid) == 1 else (lambda b, c: b * grid[1] + c)
    in_specs, out_specs, out_shapes, args = list(in_specs), list(out_specs), list(out_shapes), list(args)
    for stack, layer in casts:
        _, rows, cols = stack.shape
        nb = _cast_blocks(rows, n_steps)
        rb = rows // nb
        blk = lambda *g, nb=nb: jnp.minimum(linear(*g), nb - 1)
        in_specs.append(pl.BlockSpec((None, rb, cols), lambda *g, blk=blk, layer=layer: (layer, blk(*g), 0)))
        out_specs.append(pl.BlockSpec((rb, cols), lambda *g, blk=blk: (blk(*g), 0)))
        out_shapes.append(jax.ShapeDtypeStruct((rows, cols), BF16))
        args.append(stack)

    def kern(*refs):
        ins, rest = refs[:n_in], refs[n_in:]
        cast_in, rest = rest[:n_cast], rest[n_cast:]
        outs, rest = rest[:n_out], rest[n_out:]
        cast_out, scratch = rest[:n_cast], rest[n_cast:]
        for src, dst in zip(cast_in, cast_out):
            dst[...] = src[...].astype(BF16)
        body(*ins, *outs, *scratch)

    res = pl.pallas_call(
        kern, grid=grid, in_specs=in_specs, out_specs=out_specs, out_shape=out_shapes,
        scratch_shapes=list(scratch_shapes), input_output_aliases=aliases or {},
        compiler_params=pltpu.CompilerParams(
            dimension_semantics=("arbitrary",) * len(grid), vmem_limit_bytes=VMEM_LIMIT),
        name=name,
    )(*args)
    return res[:n_out], res[n_out:]


def _cast_weights(casts, *, n_steps=8):
    _, out = _call(lambda: None, name="cast_weights", grid=(n_steps,), in_specs=[], args=[],
                   out_specs=[], out_shapes=[], casts=casts)
    return out


def _ffn_kernel(x_ref, *rest, n_x_tiles, first, final):
    rest = list(rest)
    tail_ref = rest.pop(0) if first else None
    nw_ref, wg_ref, wu_ref, wo_ref = rest[:4]
    fw_ref = rest[4] if final else None
    o_ref = rest[-1]
    x = x_ref[...]
    if first:
        x = jnp.where(pl.program_id(0) >= n_x_tiles, tail_ref[...], x)
    xn = _rms(x, nw_ref[...]).astype(BF16)
    g = _dot(xn, wg_ref[...])
    u = _dot(xn, wu_ref[...])
    hid = (_silu(g) * u).astype(BF16)
    y = x + 0.5 * _dot(hid, wo_ref[...])
    if final:
        y = _rms(y, fw_ref[...])
    o_ref[...] = y


def _ffn(h, norm_w, w_in, w_out, *, tm, tail=None, final_w=None, casts=()):
    d = h.shape[1]
    dff = w_out.shape[0]
    first, final = tail is not None, final_w is not None
    n_x_tiles = (h.shape[0] if first else h.shape[0] - tm) // tm
    n_tiles = n_x_tiles if final else n_x_tiles + 1
    in_specs = [pl.BlockSpec((tm, d), lambda i: (jnp.minimum(i, n_x_tiles - 1), 0) if first else (i, 0))]
    args = [h]
    if first:
        in_specs.append(_const_spec((tm, d)))
        args.append(tail)
    in_specs += [
        _const_spec((1, d)),
        pl.BlockSpec((d, dff), lambda i: (0, 0), pipeline_mode=pl.Buffered(1)),
        pl.BlockSpec((d, dff), lambda i: (0, 1), pipeline_mode=pl.Buffered(1)),
        _const_spec((dff, d)),
    ]
    args += [norm_w.reshape(1, d), w_in, w_in, w_out]
    if final:
        in_specs.append(_const_spec((1, d)))
        args.append(final_w.reshape(1, d))
    (out,), cast = _call(
        functools.partial(_ffn_kernel, n_x_tiles=n_x_tiles, first=first, final=final),
        name="ffn_first" if first else "ffn_final" if final else "ffn",
        grid=(n_tiles,), in_specs=in_specs, args=args,
        out_specs=[pl.BlockSpec((tm, d), lambda i: (i, 0))],
        out_shapes=[jax.ShapeDtypeStruct((n_tiles * tm, d), F32)], casts=casts)
    return out, cast


def _tile_first_pos(i, tm, seq, n_x_tiles):
    return jnp.where(i >= n_x_tiles, -LEAD_ZERO, (i * tm) % seq + N_META)


def _ret_proj_kernel(h_ref, nw_ref, w_ref, inv_ref, o_ref, cosr_ref, sinr_ref, dq_ref, dk_ref,
                     *, tm, seq, n_x_tiles):
    i = pl.program_id(0)
    d = h_ref.shape[1]
    half = LANES
    C = RET_CHUNK

    @pl.when(i == 0)
    def _():
        row = lax.broadcasted_iota(jnp.int32, (tm, half), 0)
        ang = row.astype(F32) * inv_ref[...]
        cosr_ref[...] = jnp.cos(ang)
        sinr_ref[...] = jnp.sin(ang)
        ic = (row % C).astype(F32)
        for hd in range(RET_HEADS):
            dq_ref[hd] = jnp.exp(RET_LOG_GAMMA[hd] * (ic + 1.0))
            dk_ref[hd] = jnp.exp(RET_LOG_GAMMA[hd] * (C - 1.0 - ic))

    base = _tile_first_pos(i, tm, seq, n_x_tiles).astype(F32)
    ang0 = base * inv_ref[...]
    cb, sb = jnp.cos(ang0), jnp.sin(ang0)
    cos = cb * cosr_ref[...] - sb * sinr_ref[...]
    sin = sb * cosr_ref[...] + cb * sinr_ref[...]

    hn = _rms(h_ref[...], nw_ref[...]).astype(BF16)
    n_chunks = w_ref.shape[1] // d
    for j in range(n_chunks):
        y = _dot(hn, w_ref[:, j * d:(j + 1) * d])
        if j < 2:
            scale = 1.0 if j == 0 else (2 * half) ** -0.5
            dec_ref = dq_ref if j == 0 else dk_ref
            for hd in range(d // (2 * half)):
                lo = hd * 2 * half
                t1 = y[:, lo:lo + half]
                t2 = y[:, lo + half:lo + 2 * half]
                r1 = (t1 * cos - t2 * sin) * scale
                r2 = (t1 * sin + t2 * cos) * scale
                o_ref[:, j * d + lo:j * d + lo + half] = r1.astype(BF16)
                o_ref[:, j * d + lo + half:j * d + lo + 2 * half] = r2.astype(BF16)
                dlo = n_chunks * d + j * d + lo
                o_ref[:, dlo:dlo + half] = (r1 * dec_ref[hd]).astype(BF16)
                o_ref[:, dlo + half:dlo + 2 * half] = (r2 * dec_ref[hd]).astype(BF16)
        elif j < 4:
            o_ref[:, j * d:(j + 1) * d] = y.astype(BF16)
        else:
            o_ref[:, j * d:(j + 1) * d] = _silu(y).astype(BF16)


def _ret_proj(h, norm_w, w_in, inv, *, tm, seq, casts=()):
    rows, d = h.shape
    n = w_in.shape[1]
    assert n == 6 * d and tm % RET_CHUNK == 0
    tab = pltpu.VMEM((tm, LANES), F32)
    dec = pltpu.VMEM((RET_HEADS, tm, LANES), F32)
    (p,), cast = _call(
        functools.partial(_ret_proj_kernel, tm=tm, seq=seq, n_x_tiles=rows // tm - 1),
        name="ret_proj", grid=(rows // tm,),
        in_specs=[pl.BlockSpec((tm, d), lambda i: (i, 0)), _const_spec((1, d)), _const_spec((d, n)),
                  _const_spec((1, LANES))],
        args=[h, norm_w.reshape(1, d), w_in, inv.reshape(1, LANES)],
        out_specs=[pl.BlockSpec((tm, n + 2 * d), lambda i: (i, 0))],
        out_shapes=[jax.ShapeDtypeStruct((rows, n + 2 * d), BF16)],
        scratch_shapes=[tab, tab, dec, dec], casts=casts)
    return p, cast


def _chunk_block(b, c, *, chunks_per_batch, lead_block):
    first = jnp.where(b == 0, lead_block, b * chunks_per_batch)
    return jnp.where(c == 0, first, b * chunks_per_batch + c - 1)


def _recurrence_specs(C, widths, d, *, batch, seq):
    cpb = seq // C
    assert seq % C == 0 and LEAD_ZERO // C == (LEAD - 1) // C
    lead_block = batch * cpb + LEAD_ZERO // C
    idx = lambda b, c: (_chunk_block(b, c, chunks_per_batch=cpb, lead_block=lead_block), 0)
    return (batch, cpb + 1), [pl.BlockSpec((C, w), idx) for w in widths], pl.BlockSpec((C, d), idx)


def _ret_core_kernel(p_ref, h_ref, hn_ref, wo_ref, o_ref,
                     s_ref, slead_ref, dec_ref, obuf_ref, *, dk, dv):
    b, c = pl.program_id(0), pl.program_id(1)
    C = p_ref.shape[0]
    nh = RET_HEADS
    k0, v0, g0 = nh * dk, 2 * nh * dk, 2 * nh * dk + nh * dv
    qd0, kd0 = g0 + nh * dv, g0 + nh * dv + nh * dk

    @pl.when((b == 0) & (c == 0))
    def _():
        s_ref[...] = jnp.zeros_like(s_ref)
        row = lax.broadcasted_iota(jnp.int32, (C, C), 0)
        col = lax.broadcasted_iota(jnp.int32, (C, C), 1)
        rel = (row - col).astype(F32)
        for hd in range(nh):
            dec_ref[hd] = jnp.where(rel >= 0, jnp.exp(RET_LOG_GAMMA[hd] * jnp.maximum(rel, 0.0)), 0.0)

    @pl.when((b > 0) & (c == 0))
    def _():
        s_ref[...] = slead_ref[...]

    @pl.when((b == 0) | (c > 0))
    def _():
        for hd in range(nh):
            q = p_ref[:, hd * dk:(hd + 1) * dk]
            k = p_ref[:, k0 + hd * dk:k0 + (hd + 1) * dk]
            v = p_ref[:, v0 + hd * dv:v0 + (hd + 1) * dv]
            sg = p_ref[:, g0 + hd * dv:g0 + (hd + 1) * dv]
            qd = p_ref[:, qd0 + hd * dk:qd0 + (hd + 1) * dk]
            kd = p_ref[:, kd0 + hd * dk:kd0 + (hd + 1) * dk]
            s = (_dot_nt(q, k) * dec_ref[hd]).astype(BF16)
            st = s_ref[hd]
            o = _dot(s, v) + _dot(qd, st.astype(BF16))
            s_ref[hd] = st * math.exp(RET_LOG_GAMMA[hd] * C) + _dot_tn(kd, v)
            on = _rms(o, hn_ref[hd:hd + 1, :]) * sg.astype(F32)
            obuf_ref[:, hd * dv:(hd + 1) * dv] = on.astype(BF16)
        o_ref[...] = h_ref[...] + _dot(obuf_ref[...], wo_ref[...])

    @pl.when((b == 0) & (c == 0))
    def _():
        slead_ref[...] = s_ref[...]


def _ret_core(p, h, head_norm, w_out, *, batch, seq, casts=()):
    rows, d = h.shape
    n = p.shape[1]
    nh, dv = head_norm.shape
    dk = (n - 2 * nh * dv) // (4 * nh)
    C = RET_CHUNK
    grid, (p_spec, h_spec), o_spec = _recurrence_specs(C, (n, d), d, batch=batch, seq=seq)
    (out,), cast = _call(
        functools.partial(_ret_core_kernel, dk=dk, dv=dv), name="ret_core", grid=grid,
        in_specs=[p_spec, h_spec, _const_spec((nh, dv)), _const_spec((nh * dv, d))],
        args=[p, h, head_norm, w_out],
        out_specs=[o_spec], out_shapes=[jax.ShapeDtypeStruct((rows, d), F32)],
        aliases={1: 0},
        scratch_shapes=[
            pltpu.VMEM((nh, dk, dv), F32),
            pltpu.VMEM((nh, dk, dv), F32),
            pltpu.VMEM((nh, C, C), F32),
            pltpu.VMEM((C, nh * dv), BF16),
        ], casts=casts)
    return out, cast


def _gla_proj_kernel(h_ref, nw_ref, w_ref, wg_ref, bg_ref, o_ref, la_ref, *, tm, seq, n_x_tiles, n_main):
    i = pl.program_id(0)
    hk = wg_ref.shape[1]
    dk = hk // GLA_HEADS
    hn = _rms(h_ref[...], nw_ref[...]).astype(BF16)
    step = 512
    for lo in range(0, n_main, step):
        y = _dot(hn, w_ref[:, lo:lo + step])
        if lo < hk:
            y = y * dk ** -0.5
        elif lo >= n_main - (n_main - 2 * hk) // 2:
            y = _silu(y)
        o_ref[:, lo:lo + step] = y.astype(BF16)
    z = _dot(hn, w_ref[:, n_main:n_main + GLA_RANK])
    xg = _dot(z.astype(BF16), wg_ref[...]) + bg_ref[...]
    ls = jnp.minimum(xg, 0.0) - jnp.log1p(jnp.exp(-jnp.abs(xg)))
    pos = _tile_first_pos(i, tm, seq, n_x_tiles) + lax.broadcasted_iota(jnp.int32, (tm, hk), 0)
    la_ref[...] = jnp.where(pos >= 0, ls * (1.0 / GLA_TAU), 0.0)


def _gla_proj(h, norm_w, w_in, w_gate, b_gate, *, tm, seq, casts=()):
    rows, d = h.shape
    n = w_in.shape[1]
    n_main = n - GLA_RANK
    hk = w_gate.shape[1]
    (p, la), cast = _call(
        functools.partial(_gla_proj_kernel, tm=tm, seq=seq, n_x_tiles=rows // tm - 1, n_main=n_main),
        name="gla_proj", grid=(rows // tm,),
        in_specs=[pl.BlockSpec((tm, d), lambda i: (i, 0)), _const_spec((1, d)), _const_spec((d, n)),
                  _const_spec((GLA_RANK, hk)), _const_spec((1, hk))],
        args=[h, norm_w.reshape(1, d), w_in, w_gate, b_gate.reshape(1, hk)],
        out_specs=[pl.BlockSpec((tm, n_main), lambda i: (i, 0)), pl.BlockSpec((tm, hk), lambda i: (i, 0))],
        out_shapes=[jax.ShapeDtypeStruct((rows, n_main), BF16), jax.ShapeDtypeStruct((rows, hk), F32)],
        casts=casts)
    return p, la, cast


def _gla_core_kernel(p_ref, la_ref, h_ref, hn_ref, wo_ref, o_ref,
                     st_ref, stlead_ref, ball_ref, obuf_ref, *, dk, dv):
    bi, c = pl.program_id(0), pl.program_id(1)

    @pl.when((bi == 0) & (c == 0))
    def _():
        st_ref[...] = jnp.zeros_like(st_ref)

    @pl.when((bi > 0) & (c == 0))
    def _():
        st_ref[...] = stlead_ref[...]

    @pl.when((bi == 0) | (c > 0))
    def _():
        _gla_chunk(p_ref, la_ref, h_ref, hn_ref, wo_ref, o_ref, st_ref, ball_ref, obuf_ref, dk=dk, dv=dv)

    @pl.when((bi == 0) & (c == 0))
    def _():
        stlead_ref[...] = st_ref[...]


def _left_block_end_rows(b_ref, s):
    C, dk = b_ref.shape
    sub = 8
    bcast = lambda e, n: jnp.broadcast_to(b_ref[e:e + 1, :], (n, dk))
    if 2 * s >= sub:
        n = max(2 * s, sub)
        return jnp.concatenate([bcast(e, n) for e in range(s - 1, C, n)], axis=0)
    r = lax.broadcasted_iota(jnp.int32, (sub, dk), 0)
    tiles = []
    for t0 in range(0, C, sub):
        tile = bcast(t0 + s - 1, sub)
        for blk in range(2 * s, sub, 2 * s):
            tile = jnp.where(r >= blk, bcast(t0 + blk + s - 1, sub), tile)
        tiles.append(tile)
    return jnp.concatenate(tiles, axis=0)


def _gla_chunk(p_ref, la_ref, h_ref, hn_ref, wo_ref, o_ref, st_ref, ball_ref, obuf_ref, *, dk, dv):
    C = p_ref.shape[0]
    nh = GLA_HEADS
    k0, v0, g0 = nh * dk, 2 * nh * dk, 2 * nh * dk + nh * dv

    rowl = lax.broadcasted_iota(jnp.int32, (C, dk), 0)
    row = lax.broadcasted_iota(jnp.int32, (C, C), 0)
    col = lax.broadcasted_iota(jnp.int32, (C, C), 1)

    differ = jnp.bitwise_xor(row, col)
    level = jnp.full((C, C), -1, jnp.int32)
    for lv in range(C.bit_length() - 1):
        level = jnp.where((row > col) & (differ >= (1 << lv)), lv, level)

    a_all = la_ref[...]
    a1 = a_all.astype(BF16)
    r1 = a_all - a1.astype(F32)
    a2 = r1.astype(BF16)
    a3 = (r1 - a2.astype(F32)).astype(BF16)
    tri = (row >= col).astype(BF16)
    ball_ref[...] = _dot(tri, a1) + _dot(tri, a2) + _dot(tri, a3)

    for hd in range(nh):
        qb = p_ref[:, hd * dk:(hd + 1) * dk]
        kb = p_ref[:, k0 + hd * dk:k0 + (hd + 1) * dk]
        q = qb.astype(F32)
        k = kb.astype(F32)
        v = p_ref[:, v0 + hd * dv:v0 + (hd + 1) * dv]
        sg = p_ref[:, g0 + hd * dv:g0 + (hd + 1) * dv]
        b_ref = ball_ref.at[:, hd * dk:(hd + 1) * dk]
        b = b_ref[...]

        scores = jnp.where(col == row, _dot_nt(qb, kb), 0.0)
        for lv in range(C.bit_length() - 1):
            s = 1 << lv
            right = (rowl & s) != 0
            if s == 1:
                w = jnp.where(right, jnp.exp(la_ref[:, hd * dk:(hd + 1) * dk]), 1.0)
            else:
                w = jnp.exp(-jnp.abs(b - _left_block_end_rows(b_ref, s)))
            z = (jnp.where(right, q, k) * w).astype(BF16)
            scores = jnp.where(level == lv, _dot_nt(z, z), scores)

        btot = b_ref[C - 1:C, :]
        st = st_ref[hd]
        o = _dot(scores.astype(BF16), v) + _dot_nt((q * jnp.exp(b)).astype(BF16), st.astype(BF16))
        kt = (k * jnp.exp(btot - b)).astype(BF16)
        st_ref[hd] = st * jnp.exp(btot) + _dot_tn(v, kt)
        on = _rms(o, hn_ref[hd:hd + 1, :]) * sg.astype(F32)
        obuf_ref[:, hd * dv:(hd + 1) * dv] = on.astype(BF16)

    o_ref[...] = h_ref[...] + _dot(obuf_ref[...], wo_ref[...])


def _gla_core(p, la, h, head_norm, w_out, *, batch, seq, casts=()):
    rows, d = h.shape
    n = p.shape[1]
    nh, dv = head_norm.shape
    hk = la.shape[1]
    dk = hk // nh
    C = GLA_CHUNK
    grid, (p_spec, la_spec, h_spec), o_spec = _recurrence_specs(C, (n, hk, d), d, batch=batch, seq=seq)
    (out,), cast = _call(
        functools.partial(_gla_core_kernel, dk=dk, dv=dv), name="gla_core", grid=grid,
        in_specs=[p_spec, la_spec, h_spec, _const_spec((nh, dv)), _const_spec((nh * dv, d))],
        args=[p, la, h, head_norm, w_out],
        out_specs=[o_spec], out_shapes=[jax.ShapeDtypeStruct((rows, d), F32)],
        aliases={2: 0},
        scratch_shapes=[
            pltpu.VMEM((nh, dv, dk), F32),
            pltpu.VMEM((nh, dv, dk), F32),
            pltpu.VMEM((C, hk), F32),
            pltpu.VMEM((C, nh * dv), BF16),
        ], casts=casts)
    return out, cast


def kernel(x, meta_tokens, norm_ffn1, ffn1_w_in, ffn1_w_out, norm_mix, norm_ffn2, ffn2_w_in, ffn2_w_out, ret_w_in, ret_head_norm, ret_w_out, gla_w_in, gla_w_gate, gla_b_gate, gla_head_norm, gla_w_out, final_norm):
    batch, seq, d = x.shape
    depth = norm_ffn1.shape[0]
    tm = ROW_TILE
    assert meta_tokens.shape == (N_META, d) and seq % tm == 0 and tm >= LEAD

    tail = jnp.zeros((tm, d), x.dtype).at[LEAD_ZERO:LEAD].set(meta_tokens.astype(x.dtype))
    h = x.reshape(batch * seq, d)

    half = ret_w_in.shape[2] // 6 // RET_HEADS // 2
    assert half == LANES
    inv = 1.0 / (ROPE_BASE ** jnp.linspace(0.0, 1.0, half, dtype=F32))

    stages = []
    for i in range(depth):
        j = i // 2
        stages.append([(ffn1_w_in, i), (ffn1_w_out, i)])
        if i % 2 == 0:
            stages += [[(ret_w_in, j)], [(ret_w_out, j)]]
        else:
            stages += [[(gla_w_in, j), (gla_w_gate, j)], [(gla_w_out, j)]]
        stages.append([(ffn2_w_in, i), (ffn2_w_out, i)])
    stages.append([])
    nxt = iter(stages[1:])

    w = _cast_weights(stages[0])
    for i in range(depth):
        j = i // 2
        h, w = _ffn(h, norm_ffn1[i], *w, tm=tm, tail=tail if i == 0 else None, casts=next(nxt))
        if i % 2 == 0:
            p, w = _ret_proj(h, norm_mix[i], *w, inv, tm=tm, seq=seq, casts=next(nxt))
            h, w = _ret_core(p, h, ret_head_norm[j], *w, batch=batch, seq=seq, casts=next(nxt))
        else:
            p, la, w = _gla_proj(h, norm_mix[i], *w, gla_b_gate[j], tm=tm, seq=seq, casts=next(nxt))
            h, w = _gla_core(p, la, h, gla_head_norm[j], *w, batch=batch, seq=seq, casts=next(nxt))
        h, w = _ffn(h, norm_ffn2[i], *w, tm=tm, final_w=final_norm if i == depth - 1 else None,
                    casts=next(nxt))
    return h.reshape(batch, seq, d)
```

```python
import functools
import math

import jax
import jax.numpy as jnp
from jax import lax
from jax.experimental import pallas as pl
from jax.experimental.pallas import tpu as pltpu

F32 = jnp.float32
BF16 = jnp.bfloat16

EPS = 1e-6
N_META = 16
LEAD = 256
LEAD_ZERO = LEAD - N_META
ROW_TILE = 512
ROPE_BASE = 10000.0
RET_HEADS = 4
GLA_HEADS = 4
GLA_RANK = 16
GLA_TAU = 16.0
RET_CHUNK = 256
GLA_CHUNK = 128
LANES = 128
BF16_SUBLANES = 16
VMEM_LIMIT = 56 * 1024 * 1024

RET_LOG_GAMMA = tuple(math.log1p(-2.0 ** (-5.0 - h)) for h in range(RET_HEADS))


def _const_spec(shape):
    nd = len(shape)
    return pl.BlockSpec(shape, lambda *_: (0,) * nd, pipeline_mode=pl.Buffered(1))


def _rms(x, w):
    ms = jnp.mean(x * x, axis=-1, keepdims=True)
    return x * lax.rsqrt(ms + EPS) * w


def _silu(x):
    return x * jax.nn.sigmoid(x)


def _dot(a, b):
    return jnp.dot(a, b, preferred_element_type=F32)


def _dot_nt(a, b):
    return lax.dot_general(a, b, (((1,), (1,)), ((), ())), preferred_element_type=F32)


def _dot_tn(a, b):
    return lax.dot_general(a, b, (((0,), (0,)), ((), ())), preferred_element_type=F32)


def _cast_blocks(rows, n_steps):
    units = rows // BF16_SUBLANES
    assert rows % BF16_SUBLANES == 0
    return max(n for n in range(1, min(units, n_steps) + 1) if units % n == 0)


def _call(body, *, name, grid, in_specs, args, out_specs, out_shapes, scratch_shapes=(), aliases=None,
          casts=()):
    n_in, n_out, n_cast = len(args), len(out_shapes), len(casts)
    n_steps = math.prod(grid)
    linear = (lambda i: i) if len(grid) == 1 else (lambda b, c: b * grid[1] + c)
    in_specs, out_specs, out_shapes, args = list(in_specs), list(out_specs), list(out_shapes), list(args)
    for stack, layer in casts:
        _, rows, cols = stack.shape
        nb = _cast_blocks(rows, n_steps)
        rb = rows // nb
        blk = lambda *g, nb=nb: jnp.minimum(linear(*g), nb - 1)
        in_specs.append(pl.BlockSpec((None, rb, cols), lambda *g, blk=blk, layer=layer: (layer, blk(*g), 0)))
        out_specs.append(pl.BlockSpec((rb, cols), lambda *g, blk=blk: (blk(*g), 0)))
        out_shapes.append(jax.ShapeDtypeStruct((rows, cols), BF16))
        args.append(stack)

    def kern(*refs):
        ins, rest = refs[:n_in], refs[n_in:]
        cast_in, rest = rest[:n_cast], rest[n_cast:]
        outs, rest = rest[:n_out], rest[n_out:]
        cast_out, scratch = rest[:n_cast], rest[n_cast:]
        for src, dst in zip(cast_in, cast_out):
            dst[...] = src[...].astype(BF16)
        body(*ins, *outs, *scratch)

    res = pl.pallas_call(
        kern, grid=grid, in_specs=in_specs, out_specs=out_specs, out_shape=out_shapes,
        scratch_shapes=list(scratch_shapes), input_output_aliases=aliases or {},
        compiler_params=pltpu.CompilerParams(
            dimension_semantics=("arbitrary",) * len(grid), vmem_limit_bytes=VMEM_LIMIT),
        name=name,
    )(*args)
    return res[:n_out], res[n_out:]


def _cast_weights(casts, *, n_steps=8):
    _, out = _call(lambda: None, name="cast_weights", grid=(n_steps,), in_specs=[], args=[],
                   out_specs=[], out_shapes=[], casts=casts)
    return out


def _ffn_kernel(x_ref, *rest, n_x_tiles, first, final):
    rest = list(rest)
    tail_ref = rest.pop(0) if first else None
    nw_ref, wg_ref, wu_ref, wo_ref = rest[:4]
    fw_ref = rest[4] if final else None
    o_ref = rest[-1]
    x = x_ref[...]
    if first:
        x = jnp.where(pl.program_id(0) >= n_x_tiles, tail_ref[...], x)
    xn = _rms(x, nw_ref[...]).astype(BF16)
    g = _dot(xn, wg_ref[...])
    u = _dot(xn, wu_ref[...])
    hid = (_silu(g) * u).astype(BF16)
    y = x + 0.5 * _dot(hid, wo_ref[...])
    if final:
        y = _rms(y, fw_ref[...])
    o_ref[...] = y


def _ffn(h, norm_w, w_in, w_out, *, tm, tail=None, final_w=None, casts=()):
    d = h.shape[1]
    dff = w_out.shape[0]
    first, final = tail is not None, final_w is not None
    n_x_tiles = (h.shape[0] if first else h.shape[0] - tm) // tm
    n_tiles = n_x_tiles if final else n_x_tiles + 1
    in_specs = [pl.BlockSpec((tm, d), lambda i: (jnp.minimum(i, n_x_tiles - 1), 0) if first else (i, 0))]
    args = [h]
    if first:
        in_specs.append(_const_spec((tm, d)))
        args.append(tail)
    in_specs += [
        _const_spec((1, d)),
        pl.BlockSpec((d, dff), lambda i: (0, 0), pipeline_mode=pl.Buffered(1)),
        pl.BlockSpec((d, dff), lambda i: (0, 1), pipeline_mode=pl.Buffered(1)),
        _const_spec((dff, d)),
    ]
    args += [norm_w.reshape(1, d), w_in, w_in, w_out]
    if final:
        in_specs.append(_const_spec((1, d)))
        args.append(final_w.reshape(1, d))
    (out,), cast = _call(
        functools.partial(_ffn_kernel, n_x_tiles=n_x_tiles, first=first, final=final),
        name="ffn_first" if first else "ffn_final" if final else "ffn",
        grid=(n_tiles,), in_specs=in_specs, args=args,
        out_specs=[pl.BlockSpec((tm, d), lambda i: (i, 0))],
        out_shapes=[jax.ShapeDtypeStruct((n_tiles * tm, d), F32)], casts=casts)
    return out, cast


def _tile_first_pos(i, tm, seq, n_x_tiles):
    return jnp.where(i >= n_x_tiles, -LEAD_ZERO, (i * tm) % seq + N_META)


def _chunk_block(b, c, *, chunks_per_batch, lead_block):
    first = jnp.where(b == 0, lead_block, b * chunks_per_batch)
    return jnp.where(c == 0, first, b * chunks_per_batch + c - 1)


def _recurrence_specs(C, widths, d, *, batch, seq):
    cpb = seq // C
    assert seq % C == 0 and LEAD_ZERO // C == (LEAD - 1) // C
    lead_block = batch * cpb + LEAD_ZERO // C
    idx = lambda b, c: (_chunk_block(b, c, chunks_per_batch=cpb, lead_block=lead_block), 0)
    return (batch, cpb + 1), [pl.BlockSpec((C, w), idx) for w in widths], pl.BlockSpec((C, d), idx)


def _chunk_first_pos(c, C):
    return jnp.where(c == 0, LEAD_ZERO // C * C - LEAD_ZERO, (c - 1) * C + N_META)


def _ret_mixer_kernel(h_ref, nw_ref, w_ref, inv_ref, hn_ref, wo_ref, o_ref,
                      s_ref, slead_ref, dec_ref, dq_ref, dk_ref, cosr_ref, sinr_ref, obuf_ref, *, dk, dv):
    b, c = pl.program_id(0), pl.program_id(1)
    C = h_ref.shape[0]
    nh = RET_HEADS
    half = dk // 2
    k0, v0, g0 = nh * dk, 2 * nh * dk, 2 * nh * dk + nh * dv

    @pl.when((b == 0) & (c == 0))
    def _():
        s_ref[...] = jnp.zeros_like(s_ref)
        row = lax.broadcasted_iota(jnp.int32, (C, C), 0)
        col = lax.broadcasted_iota(jnp.int32, (C, C), 1)
        rel = (row - col).astype(F32)
        rowl = lax.broadcasted_iota(jnp.int32, (C, half), 0).astype(F32)
        ang = rowl * inv_ref[...]
        cosr_ref[...] = jnp.cos(ang)
        sinr_ref[...] = jnp.sin(ang)
        for hd in range(nh):
            lg = RET_LOG_GAMMA[hd]
            dec_ref[hd] = jnp.where(rel >= 0, jnp.exp(lg * jnp.maximum(rel, 0.0)), 0.0)
            dq_ref[hd] = jnp.exp(lg * (rowl + 1.0))
            dk_ref[hd] = jnp.exp(lg * (C - 1.0 - rowl))

    @pl.when((b > 0) & (c == 0))
    def _():
        s_ref[...] = slead_ref[...]

    @pl.when((b == 0) | (c > 0))
    def _():
        ang0 = _chunk_first_pos(c, C).astype(F32) * inv_ref[...]
        cb, sb = jnp.cos(ang0), jnp.sin(ang0)
        cos = cb * cosr_ref[...] - sb * sinr_ref[...]
        sin = sb * cosr_ref[...] + cb * sinr_ref[...]

        def rotary(y):
            t1, t2 = y[:, :half], y[:, half:]
            return jnp.concatenate([t1 * cos - t2 * sin, t1 * sin + t2 * cos], axis=1)

        x = h_ref[...]
        xn = _rms(x, nw_ref[...]).astype(BF16)
        for hd in range(nh):
            q = rotary(_dot(xn, w_ref[:, hd * dk:(hd + 1) * dk]))
            k = rotary(_dot(xn, w_ref[:, k0 + hd * dk:k0 + (hd + 1) * dk])) * dk ** -0.5
            v = _dot(xn, w_ref[:, v0 + hd * dv:v0 + (hd + 1) * dv]).astype(BF16)
            g = _dot(xn, w_ref[:, g0 + hd * dv:g0 + (hd + 1) * dv])
            qd = (q * jnp.concatenate([dq_ref[hd]] * 2, axis=1)).astype(BF16)
            kd = (k * jnp.concatenate([dk_ref[hd]] * 2, axis=1)).astype(BF16)
            s = (_dot_nt(q.astype(BF16), k.astype(BF16)) * dec_ref[hd]).astype(BF16)
            st = s_ref[hd]
            o = _dot(s, v) + _dot(qd, st.astype(BF16))
            s_ref[hd] = st * math.exp(RET_LOG_GAMMA[hd] * C) + _dot_tn(kd, v)
            on = _rms(o, hn_ref[hd:hd + 1, :]) * _silu(g)
            obuf_ref[:, hd * dv:(hd + 1) * dv] = on.astype(BF16)
        o_ref[...] = x + _dot(obuf_ref[...], wo_ref[...])

    @pl.when((b == 0) & (c == 0))
    def _():
        slead_ref[...] = s_ref[...]


def _ret_mixer(h, norm_w, w_in, w_out, head_norm, inv, *, batch, seq, casts=()):
    rows, d = h.shape
    n = w_in.shape[1]
    nh, dv = head_norm.shape
    dk = (n - 2 * nh * dv) // (2 * nh)
    assert dk == 2 * LANES
    C = RET_CHUNK
    grid, (h_spec,), o_spec = _recurrence_specs(C, (d,), d, batch=batch, seq=seq)
    (out,), cast = _call(
        functools.partial(_ret_mixer_kernel, dk=dk, dv=dv), name="ret_mixer", grid=grid,
        in_specs=[h_spec, _const_spec((1, d)), _const_spec((d, n)), _const_spec((1, dk // 2)),
                  _const_spec((nh, dv)), _const_spec((nh * dv, d))],
        args=[h, norm_w.reshape(1, d), w_in, inv.reshape(1, dk // 2), head_norm, w_out],
        out_specs=[o_spec], out_shapes=[jax.ShapeDtypeStruct((rows, d), F32)],
        aliases={0: 0},
        scratch_shapes=[
            pltpu.VMEM((nh, dk, dv), F32),
            pltpu.VMEM((nh, dk, dv), F32),
            pltpu.VMEM((nh, C, C), F32),
            pltpu.VMEM((nh, C, dk // 2), F32),
            pltpu.VMEM((nh, C, dk // 2), F32),
            pltpu.VMEM((C, dk // 2), F32),
            pltpu.VMEM((C, dk // 2), F32),
            pltpu.VMEM((C, nh * dv), BF16),
        ], casts=casts)
    return out, cast


def _gla_proj_kernel(h_ref, nw_ref, w_ref, wg_ref, bg_ref, o_ref, la_ref, *, tm, seq, n_x_tiles, n_main):
    i = pl.program_id(0)
    hk = wg_ref.shape[1]
    dk = hk // GLA_HEADS
    hn = _rms(h_ref[...], nw_ref[...]).astype(BF16)
    step = 512
    for lo in range(0, n_main, step):
        y = _dot(hn, w_ref[:, lo:lo + step])
        if lo < hk:
            y = y * dk ** -0.5
        elif lo >= n_main - (n_main - 2 * hk) // 2:
            y = _silu(y)
        o_ref[:, lo:lo + step] = y.astype(BF16)
    z = _dot(hn, w_ref[:, n_main:n_main + GLA_RANK])
    xg = _dot(z.astype(BF16), wg_ref[...]) + bg_ref[...]
    ls = jnp.minimum(xg, 0.0) - jnp.log1p(jnp.exp(-jnp.abs(xg)))
    pos = _tile_first_pos(i, tm, seq, n_x_tiles) + lax.broadcasted_iota(jnp.int32, (tm, hk), 0)
    la_ref[...] = jnp.where(pos >= 0, ls * (1.0 / GLA_TAU), 0.0)


def _gla_proj(h, norm_w, w_in, w_gate, b_gate, *, tm, seq, casts=()):
    rows, d = h.shape
    n = w_in.shape[1]
    n_main = n - GLA_RANK
    hk = w_gate.shape[1]
    (p, la), cast = _call(
        functools.partial(_gla_proj_kernel, tm=tm, seq=seq, n_x_tiles=rows // tm - 1, n_main=n_main),
        name="gla_proj", grid=(rows // tm,),
        in_specs=[pl.BlockSpec((tm, d), lambda i: (i, 0)), _const_spec((1, d)), _const_spec((d, n)),
                  _const_spec((GLA_RANK, hk)), _const_spec((1, hk))],
        args=[h, norm_w.reshape(1, d), w_in, w_gate, b_gate.reshape(1, hk)],
        out_specs=[pl.BlockSpec((tm, n_main), lambda i: (i, 0)), pl.BlockSpec((tm, hk), lambda i: (i, 0))],
        out_shapes=[jax.ShapeDtypeStruct((rows, n_main), BF16), jax.ShapeDtypeStruct((rows, hk), F32)],
        casts=casts)
    return p, la, cast


def _gla_core_kernel(p_ref, la_ref, h_ref, hn_ref, wo_ref, o_ref,
                     st_ref, stlead_ref, ball_ref, obuf_ref, *, dk, dv):
    bi, c = pl.program_id(0), pl.program_id(1)

    @pl.when((bi == 0) & (c == 0))
    def _():
        st_ref[...] = jnp.zeros_like(st_ref)

    @pl.when((bi > 0) & (c == 0))
    def _():
        st_ref[...] = stlead_ref[...]

    @pl.when((bi == 0) | (c > 0))
    def _():
        _gla_chunk(p_ref, la_ref, h_ref, hn_ref, wo_ref, o_ref, st_ref, ball_ref, obuf_ref, dk=dk, dv=dv)

    @pl.when((bi == 0) & (c == 0))
    def _():
        stlead_ref[...] = st_ref[...]


def _left_block_end_rows(b_ref, s):
    C, dk = b_ref.shape
    sub = 8
    bcast = lambda e, n: jnp.broadcast_to(b_ref[e:e + 1, :], (n, dk))
    if 2 * s >= sub:
        n = max(2 * s, sub)
        return jnp.concatenate([bcast(e, n) for e in range(s - 1, C, n)], axis=0)
    r = lax.broadcasted_iota(jnp.int32, (sub, dk), 0)
    tiles = []
    for t0 in range(0, C, sub):
        tile = bcast(t0 + s - 1, sub)
        for blk in range(2 * s, sub, 2 * s):
            tile = jnp.where(r >= blk, bcast(t0 + blk + s - 1, sub), tile)
        tiles.append(tile)
    return jnp.concatenate(tiles, axis=0)


def _gla_chunk(p_ref, la_ref, h_ref, hn_ref, wo_ref, o_ref, st_ref, ball_ref, obuf_ref, *, dk, dv):
    C = p_ref.shape[0]
    nh = GLA_HEADS
    k0, v0, g0 = nh * dk, 2 * nh * dk, 2 * nh * dk + nh * dv

    rowl = lax.broadcasted_iota(jnp.int32, (C, dk), 0)
    row = lax.broadcasted_iota(jnp.int32, (C, C), 0)
    col = lax.broadcasted_iota(jnp.int32, (C, C), 1)

    differ = jnp.bitwise_xor(row, col)
    level = jnp.full((C, C), -1, jnp.int32)
    for lv in range(C.bit_length() - 1):
        level = jnp.where((row > col) & (differ >= (1 << lv)), lv, level)

    a_all = la_ref[...]
    a1 = a_all.astype(BF16)
    r1 = a_all - a1.astype(F32)
    a2 = r1.astype(BF16)
    a3 = (r1 - a2.astype(F32)).astype(BF16)
    tri = (row >= col).astype(BF16)
    ball_ref[...] = _dot(tri, a1) + _dot(tri, a2) + _dot(tri, a3)

    for hd in range(nh):
        qb = p_ref[:, hd * dk:(hd + 1) * dk]
        kb = p_ref[:, k0 + hd * dk:k0 + (hd + 1) * dk]
        q = qb.astype(F32)
        k = kb.astype(F32)
        v = p_ref[:, v0 + hd * dv:v0 + (hd + 1) * dv]
        sg = p_ref[:, g0 + hd * dv:g0 + (hd + 1) * dv]
        b_ref = ball_ref.at[:, hd * dk:(hd + 1) * dk]
        b = b_ref[...]

        scores = jnp.where(col == row, _dot_nt(qb, kb), 0.0)
        for lv in range(C.bit_length() - 1):
            s = 1 << lv
            right = (rowl & s) != 0
            if s == 1:
                w = jnp.where(right, jnp.exp(la_ref[:, hd * dk:(hd + 1) * dk]), 1.0)
            else:
                w = jnp.exp(-jnp.abs(b - _left_block_end_rows(b_ref, s)))
            z = (jnp.where(right, q, k) * w).astype(BF16)
            scores = jnp.where(level == lv, _dot_nt(z, z), scores)

        btot = b_ref[C - 1:C, :]
        st = st_ref[hd]
        o = _dot(scores.astype(BF16), v) + _dot_nt((q * jnp.exp(b)).astype(BF16), st.astype(BF16))
        kt = (k * jnp.exp(btot - b)).astype(BF16)
        st_ref[hd] = st * jnp.exp(btot) + _dot_tn(v, kt)
        on = _rms(o, hn_ref[hd:hd + 1, :]) * sg.astype(F32)
        obuf_ref[:, hd * dv:(hd + 1) * dv] = on.astype(BF16)

    o_ref[...] = h_ref[...] + _dot(obuf_ref[...], wo_ref[...])


def _gla_core(p, la, h, head_norm, w_out, *, batch, seq, casts=()):
    rows, d = h.shape
    n = p.shape[1]
    nh, dv = head_norm.shape
    hk = la.shape[1]
    dk = hk // nh
    C = GLA_CHUNK
    grid, (p_spec, la_spec, h_spec), o_spec = _recurrence_specs(C, (n, hk, d), d, batch=batch, seq=seq)
    (out,), cast = _call(
        functools.partial(_gla_core_kernel, dk=dk, dv=dv), name="gla_core", grid=grid,
        in_specs=[p_spec, la_spec, h_spec, _const_spec((nh, dv)), _const_spec((nh * dv, d))],
        args=[p, la, h, head_norm, w_out],
        out_specs=[o_spec], out_shapes=[jax.ShapeDtypeStruct((rows, d), F32)],
        aliases={2: 0},
        scratch_shapes=[
            pltpu.VMEM((nh, dv, dk), F32),
            pltpu.VMEM((nh, dv, dk), F32),
            pltpu.VMEM((C, hk), F32),
            pltpu.VMEM((C, nh * dv), BF16),
        ], casts=casts)
    return out, cast


def kernel(x, meta_tokens, norm_ffn1, ffn1_w_in, ffn1_w_out, norm_mix, norm_ffn2, ffn2_w_in, ffn2_w_out, ret_w_in, ret_head_norm, ret_w_out, gla_w_in, gla_w_gate, gla_b_gate, gla_head_norm, gla_w_out, final_norm):
    batch, seq, d = x.shape
    depth = norm_ffn1.shape[0]
    tm = ROW_TILE
    assert meta_tokens.shape == (N_META, d) and seq % tm == 0 and tm >= LEAD

    tail = jnp.zeros((tm, d), x.dtype).at[LEAD_ZERO:LEAD].set(meta_tokens.astype(x.dtype))
    h = x.reshape(batch * seq, d)

    half = ret_w_in.shape[2] // 6 // RET_HEADS // 2
    assert half == LANES
    inv = 1.0 / (ROPE_BASE ** jnp.linspace(0.0, 1.0, half, dtype=F32))

    stages = []
    for i in range(depth):
        j = i // 2
        stages.append([(ffn1_w_in, i), (ffn1_w_out, i)])
        if i % 2 == 0:
            stages.append([(ret_w_in, j), (ret_w_out, j)])
        else:
            stages += [[(gla_w_in, j), (gla_w_gate, j)], [(gla_w_out, j)]]
        stages.append([(ffn2_w_in, i), (ffn2_w_out, i)])
    stages.append([])
    nxt = iter(stages[1:])

    w = _cast_weights(stages[0])
    for i in range(depth):
        j = i // 2
        h, w = _ffn(h, norm_ffn1[i], *w, tm=tm, tail=tail if i == 0 else None, casts=next(nxt))
        if i % 2 == 0:
            h, w = _ret_mixer(h, norm_mix[i], *w, ret_head_norm[j], inv, batch=batch, seq=seq,
                              casts=next(nxt))
        else:
            p, la, w = _gla_proj(h, norm_mix[i], *w, gla_b_gate[j], tm=tm, seq=seq, casts=next(nxt))
            h, w = _gla_core(p, la, h, gla_head_norm[j], *w, batch=batch, seq=seq, casts=next(nxt))
        h, w = _ffn(h, norm_ffn2[i], *w, tm=tm, final_w=final_norm if i == depth - 1 else None,
                    casts=next(nxt))
    return h.reshape(batch, seq, d)
```

```python
import functools
import math

import jax
import jax.numpy as jnp
from jax import lax
from jax.experimental import pallas as pl
from jax.experimental.pallas import tpu as pltpu

F32 = jnp.float32
BF16 = jnp.bfloat16

EPS = 1e-6
N_META = 16
LEAD = 256
LEAD_ZERO = LEAD - N_META
ROW_TILE = 512
ROPE_BASE = 10000.0
RET_HEADS = 4
GLA_HEADS = 4
GLA_RANK = 16
GLA_TAU = 16.0
RET_CHUNK = 256
GLA_CHUNK = 128
LANES = 128
BF16_SUBLANES = 16
VMEM_LIMIT = 56 * 1024 * 1024

RET_LOG_GAMMA = tuple(math.log1p(-2.0 ** (-5.0 - h)) for h in range(RET_HEADS))


def _const_spec(shape):
    nd = len(shape)
    return pl.BlockSpec(shape, lambda *_: (0,) * nd, pipeline_mode=pl.Buffered(1))


def _rms(x, w):
    ms = jnp.mean(x * x, axis=-1, keepdims=True)
    return x * lax.rsqrt(ms + EPS) * w


def _silu(x):
    return x * jax.nn.sigmoid(x)


def _dot(a, b):
    return jnp.dot(a, b, preferred_element_type=F32)


def _dot_nt(a, b):
    return lax.dot_general(a, b, (((1,), (1,)), ((), ())), preferred_element_type=F32)


def _dot_tn(a, b):
    return lax.dot_general(a, b, (((0,), (0,)), ((), ())), preferred_element_type=F32)


def _cast_blocks(rows, n_steps):
    units = rows // BF16_SUBLANES
    assert rows % BF16_SUBLANES == 0
    return max(n for n in range(1, min(units, n_steps) + 1) if units % n == 0)


def _call(body, *, name, grid, in_specs, args, out_specs, out_shapes, scratch_shapes=(), aliases=None,
          casts=()):
    n_in, n_out, n_cast = len(args), len(out_shapes), len(casts)
    n_steps = math.prod(grid)
    linear = (lambda i: i) if len(grid) == 1 else (lambda b, c: b * grid[1] + c)
    in_specs, out_specs, out_shapes, args = list(in_specs), list(out_specs), list(out_shapes), list(args)
    for stack, layer in casts:
        _, rows, cols = stack.shape
        nb = _cast_blocks(rows, n_steps)
        rb = rows // nb
        blk = lambda *g, nb=nb: jnp.minimum(linear(*g), nb - 1)
        in_specs.append(pl.BlockSpec((None, rb, cols), lambda *g, blk=blk, layer=layer: (layer, blk(*g), 0)))
        out_specs.append(pl.BlockSpec((rb, cols), lambda *g, blk=blk: (blk(*g), 0)))
        out_shapes.append(jax.ShapeDtypeStruct((rows, cols), BF16))
        args.append(stack)

    def kern(*refs):
        ins, rest = refs[:n_in], refs[n_in:]
        cast_in, rest = rest[:n_cast], rest[n_cast:]
        outs, rest = rest[:n_out], rest[n_out:]
        cast_out, scratch = rest[:n_cast], rest[n_cast:]
        for src, dst in zip(cast_in, cast_out):
            dst[...] = src[...].astype(BF16)
        body(*ins, *outs, *scratch)

    res = pl.pallas_call(
        kern, grid=grid, in_specs=in_specs, out_specs=out_specs, out_shape=out_shapes,
        scratch_shapes=list(scratch_shapes), input_output_aliases=aliases or {},
        compiler_params=pltpu.CompilerParams(
            dimension_semantics=("arbitrary",) * len(grid), vmem_limit_bytes=VMEM_LIMIT),
        name=name,
    )(*args)
    return res[:n_out], res[n_out:]


def _cast_weights(casts, *, n_steps=8):
    _, out = _call(lambda: None, name="cast_weights", grid=(n_steps,), in_specs=[], args=[],
                   out_specs=[], out_shapes=[], casts=casts)
    return out


def _ffn_kernel(x_ref, *rest, n_x_tiles, first, final):
    rest = list(rest)
    tail_ref = rest.pop(0) if first else None
    nw_ref, wg_ref, wu_ref, wo_ref = rest[:4]
    fw_ref = rest[4] if final else None
    o_ref = rest[-1]
    x = x_ref[...]
    if first:
        x = jnp.where(pl.program_id(0) >= n_x_tiles, tail_ref[...], x)
    xn = _rms(x, nw_ref[...]).astype(BF16)
    g = _dot(xn, wg_ref[...])
    u = _dot(xn, wu_ref[...])
    hid = (_silu(g) * u).astype(BF16)
    y = x + 0.5 * _dot(hid, wo_ref[...])
    if final:
        y = _rms(y, fw_ref[...])
    o_ref[...] = y


def _ffn(h, norm_w, w_in, w_out, *, tm, tail=None, final_w=None, casts=()):
    d = h.shape[1]
    dff = w_out.shape[0]
    first, final = tail is not None, final_w is not None
    n_x_tiles = (h.shape[0] if first else h.shape[0] - tm) // tm
    n_tiles = n_x_tiles if final else n_x_tiles + 1
    in_specs = [pl.BlockSpec((tm, d), lambda i: (jnp.minimum(i, n_x_tiles - 1), 0) if first else (i, 0))]
    args = [h]
    if first:
        in_specs.append(_const_spec((tm, d)))
        args.append(tail)
    in_specs += [
        _const_spec((1, d)),
        pl.BlockSpec((d, dff), lambda i: (0, 0), pipeline_mode=pl.Buffered(1)),
        pl.BlockSpec((d, dff), lambda i: (0, 1), pipeline_mode=pl.Buffered(1)),
        _const_spec((dff, d)),
    ]
    args += [norm_w.reshape(1, d), w_in, w_in, w_out]
    if final:
        in_specs.append(_const_spec((1, d)))
        args.append(final_w.reshape(1, d))
    (out,), cast = _call(
        functools.partial(_ffn_kernel, n_x_tiles=n_x_tiles, first=first, final=final),
        name="ffn_first" if first else "ffn_final" if final else "ffn",
        grid=(n_tiles,), in_specs=in_specs, args=args,
        out_specs=[pl.BlockSpec((tm, d), lambda i: (i, 0))],
        out_shapes=[jax.ShapeDtypeStruct((n_tiles * tm, d), F32)], casts=casts)
    return out, cast


def _chunk_block(b, c, *, chunks_per_batch, lead_block):
    first = jnp.where(b == 0, lead_block, b * chunks_per_batch)
    return jnp.where(c == 0, first, b * chunks_per_batch + c - 1)


def _recurrence_specs(C, widths, d, *, batch, seq):
    cpb = seq // C
    assert seq % C == 0 and LEAD_ZERO // C == (LEAD - 1) // C
    lead_block = batch * cpb + LEAD_ZERO // C
    idx = lambda b, c: (_chunk_block(b, c, chunks_per_batch=cpb, lead_block=lead_block), 0)
    return (batch, cpb + 1), [pl.BlockSpec((C, w), idx) for w in widths], pl.BlockSpec((C, d), idx)


def _chunk_first_pos(c, C):
    return jnp.where(c == 0, LEAD_ZERO // C * C - LEAD_ZERO, (c - 1) * C + N_META)


def _ret_mixer_kernel(h_ref, nw_ref, w_ref, inv_ref, hn_ref, wo_ref, o_ref,
                      s_ref, slead_ref, dec_ref, dq_ref, dk_ref, cosr_ref, sinr_ref, obuf_ref, *, dk, dv):
    b, c = pl.program_id(0), pl.program_id(1)
    C = h_ref.shape[0]
    nh = RET_HEADS
    half = dk // 2
    k0, v0, g0 = nh * dk, 2 * nh * dk, 2 * nh * dk + nh * dv

    @pl.when((b == 0) & (c == 0))
    def _():
        s_ref[...] = jnp.zeros_like(s_ref)
        row = lax.broadcasted_iota(jnp.int32, (C, C), 0)
        col = lax.broadcasted_iota(jnp.int32, (C, C), 1)
        rel = (row - col).astype(F32)
        rowl = lax.broadcasted_iota(jnp.int32, (C, half), 0).astype(F32)
        ang = rowl * inv_ref[...]
        cosr_ref[...] = jnp.cos(ang)
        sinr_ref[...] = jnp.sin(ang)
        for hd in range(nh):
            lg = RET_LOG_GAMMA[hd]
            dec_ref[hd] = jnp.where(rel >= 0, jnp.exp(lg * jnp.maximum(rel, 0.0)), 0.0)
            dq_ref[hd] = jnp.exp(lg * (rowl + 1.0))
            dk_ref[hd] = jnp.exp(lg * (C - 1.0 - rowl))

    @pl.when((b > 0) & (c == 0))
    def _():
        s_ref[...] = slead_ref[...]

    @pl.when((b == 0) | (c > 0))
    def _():
        ang0 = _chunk_first_pos(c, C).astype(F32) * inv_ref[...]
        cb, sb = jnp.cos(ang0), jnp.sin(ang0)
        cos = cb * cosr_ref[...] - sb * sinr_ref[...]
        sin = sb * cosr_ref[...] + cb * sinr_ref[...]

        def rotary(y):
            t1, t2 = y[:, :half], y[:, half:]
            return jnp.concatenate([t1 * cos - t2 * sin, t1 * sin + t2 * cos], axis=1)

        x = h_ref[...]
        xn = _rms(x, nw_ref[...]).astype(BF16)
        for hd in range(nh):
            q = rotary(_dot(xn, w_ref[:, hd * dk:(hd + 1) * dk]))
            k = rotary(_dot(xn, w_ref[:, k0 + hd * dk:k0 + (hd + 1) * dk])) * dk ** -0.5
            v = _dot(xn, w_ref[:, v0 + hd * dv:v0 + (hd + 1) * dv]).astype(BF16)
            g = _dot(xn, w_ref[:, g0 + hd * dv:g0 + (hd + 1) * dv])
            qd = (q * jnp.concatenate([dq_ref[hd]] * 2, axis=1)).astype(BF16)
            kd = (k * jnp.concatenate([dk_ref[hd]] * 2, axis=1)).astype(BF16)
            s = (_dot_nt(q.astype(BF16), k.astype(BF16)) * dec_ref[hd]).astype(BF16)
            st = s_ref[hd]
            o = _dot(s, v) + _dot(qd, st.astype(BF16))
            s_ref[hd] = st * math.exp(RET_LOG_GAMMA[hd] * C) + _dot_tn(kd, v)
            on = _rms(o, hn_ref[hd:hd + 1, :]) * _silu(g)
            obuf_ref[:, hd * dv:(hd + 1) * dv] = on.astype(BF16)
        o_ref[...] = x + _dot(obuf_ref[...], wo_ref[...])

    @pl.when((b == 0) & (c == 0))
    def _():
        slead_ref[...] = s_ref[...]


def _ret_mixer(h, norm_w, w_in, w_out, head_norm, inv, *, batch, seq, casts=()):
    rows, d = h.shape
    n = w_in.shape[1]
    nh, dv = head_norm.shape
    dk = (n - 2 * nh * dv) // (2 * nh)
    assert dk == 2 * LANES
    C = RET_CHUNK
    grid, (h_spec,), o_spec = _recurrence_specs(C, (d,), d, batch=batch, seq=seq)
    (out,), cast = _call(
        functools.partial(_ret_mixer_kernel, dk=dk, dv=dv), name="ret_mixer", grid=grid,
        in_specs=[h_spec, _const_spec((1, d)), _const_spec((d, n)), _const_spec((1, dk // 2)),
                  _const_spec((nh, dv)), _const_spec((nh * dv, d))],
        args=[h, norm_w.reshape(1, d), w_in, inv.reshape(1, dk // 2), head_norm, w_out],
        out_specs=[o_spec], out_shapes=[jax.ShapeDtypeStruct((rows, d), F32)],
        aliases={0: 0},
        scratch_shapes=[
            pltpu.VMEM((nh, dk, dv), F32),
            pltpu.VMEM((nh, dk, dv), F32),
            pltpu.VMEM((nh, C, C), F32),
            pltpu.VMEM((nh, C, dk // 2), F32),
            pltpu.VMEM((nh, C, dk // 2), F32),
            pltpu.VMEM((C, dk // 2), F32),
            pltpu.VMEM((C, dk // 2), F32),
            pltpu.VMEM((C, nh * dv), BF16),
        ], casts=casts)
    return out, cast


def _left_block_end_rows(b_ref, s):
    C, dk = b_ref.shape
    sub = 8
    bcast = lambda e, n: jnp.broadcast_to(b_ref[e:e + 1, :], (n, dk))
    if 2 * s >= sub:
        n = max(2 * s, sub)
        return jnp.concatenate([bcast(e, n) for e in range(s - 1, C, n)], axis=0)
    r = lax.broadcasted_iota(jnp.int32, (sub, dk), 0)
    tiles = []
    for t0 in range(0, C, sub):
        tile = bcast(t0 + s - 1, sub)
        for blk in range(2 * s, sub, 2 * s):
            tile = jnp.where(r >= blk, bcast(t0 + blk + s - 1, sub), tile)
        tiles.append(tile)
    return jnp.concatenate(tiles, axis=0)


def _gla_project_pieces(first_pos, h_ref, nw_ref, w_ref, wg_ref, bg_ref, buf, xn_ref, *, dk, dv):
    x_buf, q_buf, k_buf, v_buf, g_buf, la_buf, b_buf = buf
    C = h_ref.shape[0]
    nh = GLA_HEADS
    hk, hv = nh * dk, nh * dv
    k0, v0, g0, z0 = hk, 2 * hk, 2 * hk + hv, 2 * hk + 2 * hv
    step = 2 * LANES

    def start():
        x = h_ref[...]
        x_buf[...] = x
        xn_ref[...] = _rms(x, nw_ref[...]).astype(BF16)

    def block(dst, col0, lo, scale, dtype):
        def run():
            y = _dot(xn_ref[...], w_ref[:, col0 + lo:col0 + lo + step])
            dst[:, lo:lo + step] = (y if scale is None else y * scale).astype(dtype)
        return run

    def gates():
        z = _dot(xn_ref[...], w_ref[:, z0:z0 + GLA_RANK])
        xg = _dot(z.astype(BF16), wg_ref[...]) + bg_ref[...]
        ls = jnp.minimum(xg, 0.0) - jnp.log1p(jnp.exp(-jnp.abs(xg)))
        pos = first_pos + lax.broadcasted_iota(jnp.int32, (C, hk), 0)
        a_all = jnp.where(pos >= 0, ls * (1.0 / GLA_TAU), 0.0)
        la_buf[...] = a_all
        a1 = a_all.astype(BF16)
        r1 = a_all - a1.astype(F32)
        a2 = r1.astype(BF16)
        a3 = (r1 - a2.astype(F32)).astype(BF16)
        row = lax.broadcasted_iota(jnp.int32, (C, C), 0)
        col = lax.broadcasted_iota(jnp.int32, (C, C), 1)
        tri = (row >= col).astype(BF16)
        b_buf[...] = _dot(tri, a1) + _dot(tri, a2) + _dot(tri, a3)

    pieces = [start, gates]
    for dst, col0, width, scale, dtype in ((q_buf, 0, hk, dk ** -0.5, F32), (k_buf, k0, hk, None, F32),
                                           (v_buf, v0, hv, None, BF16), (g_buf, g0, hv, None, F32)):
        pieces += [block(dst, col0, lo, scale, dtype) for lo in range(0, width, step)]
    return pieces


def _gla_recur(buf, hn_ref, wo_ref, o_ref, st_ref, obuf_ref, side, *, dk, dv):
    x_buf, q_buf, k_buf, v_buf, g_buf, la_buf, b_buf = buf
    side = list(side)
    n_slots = GLA_HEADS * ((x_buf.shape[0].bit_length() - 1) // 2 + 1)
    per_slot = -(-len(side) // n_slots)

    def run_side():
        for piece in side[:per_slot]:
            piece()
        del side[:per_slot]

    C = x_buf.shape[0]
    nh = GLA_HEADS

    rowl = lax.broadcasted_iota(jnp.int32, (C, dk), 0)
    row = lax.broadcasted_iota(jnp.int32, (C, C), 0)
    col = lax.broadcasted_iota(jnp.int32, (C, C), 1)

    differ = jnp.bitwise_xor(row, col)
    level = jnp.full((C, C), -1, jnp.int32)
    for lv in range(C.bit_length() - 1):
        level = jnp.where((row > col) & (differ >= (1 << lv)), lv, level)

    for hd in range(nh):
        q = q_buf[:, hd * dk:(hd + 1) * dk]
        k = k_buf[:, hd * dk:(hd + 1) * dk]
        v = v_buf[:, hd * dv:(hd + 1) * dv]
        b_ref = b_buf.at[:, hd * dk:(hd + 1) * dk]
        b = b_ref[...]

        scores = jnp.where(col == row, _dot_nt(q.astype(BF16), k.astype(BF16)), 0.0)
        for lv in range(C.bit_length() - 1):
            s = 1 << lv
            right = (rowl & s) != 0
            if s == 1:
                w = jnp.where(right, jnp.exp(la_buf[:, hd * dk:(hd + 1) * dk]), 1.0)
            else:
                w = jnp.exp(-jnp.abs(b - _left_block_end_rows(b_ref, s)))
            zz = (jnp.where(right, q, k) * w).astype(BF16)
            scores = jnp.where(level == lv, _dot_nt(zz, zz), scores)
            if lv % 2 == 1:
                run_side()

        btot = b_ref[C - 1:C, :]
        st = st_ref[hd]
        o = _dot(scores.astype(BF16), v) + _dot_nt((q * jnp.exp(b)).astype(BF16), st.astype(BF16))
        kt = (k * jnp.exp(btot - b)).astype(BF16)
        st_ref[hd] = st * jnp.exp(btot) + _dot_tn(v, kt)
        on = _rms(o, hn_ref[hd:hd + 1, :]) * _silu(g_buf[:, hd * dv:(hd + 1) * dv])
        obuf_ref[:, hd * dv:(hd + 1) * dv] = on.astype(BF16)
        run_side()

    assert not side
    o_ref[...] = x_buf[...] + _dot(obuf_ref[...], wo_ref[...])


def _gla_mixer_kernel(h_ref, nw_ref, w_ref, wg_ref, bg_ref, hn_ref, wo_ref, o_ref,
                      st_ref, stlead_ref, obuf_ref, xn_ref, *bufs, dk, dv, cpb):
    t = pl.program_id(0)
    C = h_ref.shape[0]
    half = len(bufs) // 2
    buf_a, buf_b = bufs[:half], bufs[half:]
    g = t - 1

    @pl.when(t == 0)
    def _():
        st_ref[...] = jnp.zeros_like(st_ref)
        for ref in buf_b:
            ref[...] = jnp.zeros_like(ref)

    @pl.when((g > 1) & ((g - 1) % cpb == 0))
    def _():
        st_ref[...] = stlead_ref[...]

    first_pos = jnp.where(t == 0, LEAD_ZERO // C * C - LEAD_ZERO, ((t - 1) % cpb) * C + N_META)
    for parity, cur, prev in ((0, buf_a, buf_b), (1, buf_b, buf_a)):
        @pl.when(t % 2 == parity)
        def _():
            side = _gla_project_pieces(first_pos, h_ref, nw_ref, w_ref, wg_ref, bg_ref, cur, xn_ref,
                                       dk=dk, dv=dv)
            _gla_recur(prev, hn_ref, wo_ref, o_ref, st_ref, obuf_ref, side, dk=dk, dv=dv)

    @pl.when(t == 1)
    def _():
        stlead_ref[...] = st_ref[...]


def _gla_mixer(h, norm_w, w_in, w_gate, w_out, b_gate, head_norm, *, batch, seq, casts=()):
    rows, d = h.shape
    n = w_in.shape[1]
    nh, dv = head_norm.shape
    hk = w_gate.shape[1]
    dk = hk // nh
    assert n == 2 * hk + 2 * nh * dv + GLA_RANK
    C = GLA_CHUNK
    cpb = seq // C
    assert seq % C == 0 and LEAD_ZERO // C == (LEAD - 1) // C
    n_chunks = 1 + batch * cpb
    lead_block = batch * cpb + LEAD_ZERO // C
    block = lambda g: jnp.where(g == 0, lead_block, g - 1)
    buf = [pltpu.VMEM((C, d), F32), pltpu.VMEM((C, hk), F32), pltpu.VMEM((C, hk), F32),
           pltpu.VMEM((C, nh * dv), BF16), pltpu.VMEM((C, nh * dv), F32), pltpu.VMEM((C, hk), F32),
           pltpu.VMEM((C, hk), F32)]
    (out,), cast = _call(
        functools.partial(_gla_mixer_kernel, dk=dk, dv=dv, cpb=cpb), name="gla_mixer",
        grid=(n_chunks + 1,),
        in_specs=[pl.BlockSpec((C, d), lambda t: (block(jnp.minimum(t, n_chunks - 1)), 0)),
                  _const_spec((1, d)), _const_spec((d, n)), _const_spec((GLA_RANK, hk)),
                  _const_spec((1, hk)), _const_spec((nh, dv)), _const_spec((nh * dv, d))],
        args=[h, norm_w.reshape(1, d), w_in, w_gate, b_gate.reshape(1, hk), head_norm, w_out],
        out_specs=[pl.BlockSpec((C, d), lambda t: (block(jnp.maximum(t - 1, 0)), 0))],
        out_shapes=[jax.ShapeDtypeStruct((rows, d), F32)],
        aliases={0: 0},
        scratch_shapes=[pltpu.VMEM((nh, dv, dk), F32), pltpu.VMEM((nh, dv, dk), F32),
                        pltpu.VMEM((C, nh * dv), BF16), pltpu.VMEM((C, d), BF16)] + buf + buf,
        casts=casts)
    return out, cast


def kernel(x, meta_tokens, norm_ffn1, ffn1_w_in, ffn1_w_out, norm_mix, norm_ffn2, ffn2_w_in, ffn2_w_out, ret_w_in, ret_head_norm, ret_w_out, gla_w_in, gla_w_gate, gla_b_gate, gla_head_norm, gla_w_out, final_norm):
    batch, seq, d = x.shape
    depth = norm_ffn1.shape[0]
    tm = ROW_TILE
    assert meta_tokens.shape == (N_META, d) and seq % tm == 0 and tm >= LEAD

    tail = jnp.zeros((tm, d), x.dtype).at[LEAD_ZERO:LEAD].set(meta_tokens.astype(x.dtype))
    h = x.reshape(batch * seq, d)

    half = ret_w_in.shape[2] // 6 // RET_HEADS // 2
    assert half == LANES
    inv = 1.0 / (ROPE_BASE ** jnp.linspace(0.0, 1.0, half, dtype=F32))

    stages = []
    for i in range(depth):
        j = i // 2
        stages.append([(ffn1_w_in, i), (ffn1_w_out, i)])
        if i % 2 == 0:
            stages.append([(ret_w_in, j), (ret_w_out, j)])
        else:
            stages.append([(gla_w_in, j), (gla_w_gate, j), (gla_w_out, j)])
        stages.append([(ffn2_w_in, i), (ffn2_w_out, i)])
    stages.append([])
    nxt = iter(stages[1:])

    w = _cast_weights(stages[0])
    for i in range(depth):
        j = i // 2
        h, w = _ffn(h, norm_ffn1[i], *w, tm=tm, tail=tail if i == 0 else None, casts=next(nxt))
        if i % 2 == 0:
            h, w = _ret_mixer(h, norm_mix[i], *w, ret_head_norm[j], inv, batch=batch, seq=seq,
                              casts=next(nxt))
        else:
            h, w = _gla_mixer(h, norm_mix[i], *w, gla_b_gate[j], gla_head_norm[j], batch=batch, seq=seq,
                              casts=next(nxt))
        h, w = _ffn(h, norm_ffn2[i], *w, tm=tm, final_w=final_norm if i == depth - 1 else None,
                    casts=next(nxt))
    return h.reshape(batch, seq, d)
```

```python
import functools
import math

import jax
import jax.numpy as jnp
from jax import lax
from jax.experimental import pallas as pl
from jax.experimental.pallas import tpu as pltpu

F32 = jnp.float32
BF16 = jnp.bfloat16

EPS = 1e-6
N_META = 16
LEAD = 256
LEAD_ZERO = LEAD - N_META
ROW_TILE = 512
ROPE_BASE = 10000.0
RET_HEADS = 4
GLA_HEADS = 4
GLA_RANK = 16
GLA_TAU = 16.0
RET_CHUNK = 256
GLA_CHUNK = 128
GLA_CHUNKS_PER_STEP = 2
LANES = 128
BF16_SUBLANES = 16
VMEM_LIMIT = 56 * 1024 * 1024

RET_LOG_GAMMA = tuple(math.log1p(-2.0 ** (-5.0 - h)) for h in range(RET_HEADS))


def _const_spec(shape):
    nd = len(shape)
    return pl.BlockSpec(shape, lambda *_: (0,) * nd, pipeline_mode=pl.Buffered(1))


def _rms(x, w):
    ms = jnp.mean(x * x, axis=-1, keepdims=True)
    return x * lax.rsqrt(ms + EPS) * w


def _silu(x):
    return x * jax.nn.sigmoid(x)


def _dot(a, b):
    return jnp.dot(a, b, preferred_element_type=F32)


def _dot_nt(a, b):
    return lax.dot_general(a, b, (((1,), (1,)), ((), ())), preferred_element_type=F32)


def _dot_tn(a, b):
    return lax.dot_general(a, b, (((0,), (0,)), ((), ())), preferred_element_type=F32)


def _cast_blocks(rows, n_steps):
    units = rows // BF16_SUBLANES
    assert rows % BF16_SUBLANES == 0
    return max(n for n in range(1, min(units, n_steps) + 1) if units % n == 0)


def _call(body, *, name, grid, in_specs, args, out_specs, out_shapes, scratch_shapes=(), aliases=None,
          casts=()):
    n_in, n_out, n_cast = len(args), len(out_shapes), len(casts)
    n_steps = math.prod(grid)
    linear = (lambda i: i) if len(grid) == 1 else (lambda b, c: b * grid[1] + c)
    in_specs, out_specs, out_shapes, args = list(in_specs), list(out_specs), list(out_shapes), list(args)
    for stack, layer in casts:
        _, rows, cols = stack.shape
        nb = _cast_blocks(rows, n_steps)
        rb = rows // nb
        blk = lambda *g, nb=nb: jnp.minimum(linear(*g), nb - 1)
        in_specs.append(pl.BlockSpec((None, rb, cols), lambda *g, blk=blk, layer=layer: (layer, blk(*g), 0)))
        out_specs.append(pl.BlockSpec((rb, cols), lambda *g, blk=blk: (blk(*g), 0)))
        out_shapes.append(jax.ShapeDtypeStruct((rows, cols), BF16))
        args.append(stack)

    def kern(*refs):
        ins, rest = refs[:n_in], refs[n_in:]
        cast_in, rest = rest[:n_cast], rest[n_cast:]
        outs, rest = rest[:n_out], rest[n_out:]
        cast_out, scratch = rest[:n_cast], rest[n_cast:]
        for src, dst in zip(cast_in, cast_out):
            dst[...] = src[...].astype(BF16)
        body(*ins, *outs, *scratch)

    res = pl.pallas_call(
        kern, grid=grid, in_specs=in_specs, out_specs=out_specs, out_shape=out_shapes,
        scratch_shapes=list(scratch_shapes), input_output_aliases=aliases or {},
        compiler_params=pltpu.CompilerParams(
            dimension_semantics=("arbitrary",) * len(grid), vmem_limit_bytes=VMEM_LIMIT),
        name=name,
    )(*args)
    return res[:n_out], res[n_out:]


def _cast_weights(casts, *, n_steps=8):
    _, out = _call(lambda: None, name="cast_weights", grid=(n_steps,), in_specs=[], args=[],
                   out_specs=[], out_shapes=[], casts=casts)
    return out


def _ffn_kernel(x_ref, *rest, n_x_tiles, first, final):
    rest = list(rest)
    tail_ref = rest.pop(0) if first else None
    nw_ref, wg_ref, wu_ref, wo_ref = rest[:4]
    fw_ref = rest[4] if final else None
    o_ref = rest[-1]
    x = x_ref[...]
    if first:
        x = jnp.where(pl.program_id(0) >= n_x_tiles, tail_ref[...], x)
    xn = _rms(x, nw_ref[...]).astype(BF16)
    g = _dot(xn, wg_ref[...])
    u = _dot(xn, wu_ref[...])
    hid = (_silu(g) * u).astype(BF16)
    y = x + 0.5 * _dot(hid, wo_ref[...])
    if final:
        y = _rms(y, fw_ref[...])
    o_ref[...] = y


def _ffn(h, norm_w, w_in, w_out, *, tm, tail=None, final_w=None, casts=()):
    d = h.shape[1]
    dff = w_out.shape[0]
    first, final = tail is not None, final_w is not None
    n_x_tiles = (h.shape[0] if first else h.shape[0] - tm) // tm
    n_tiles = n_x_tiles if final else n_x_tiles + 1
    in_specs = [pl.BlockSpec((tm, d), lambda i: (jnp.minimum(i, n_x_tiles - 1), 0) if first else (i, 0))]
    args = [h]
    if first:
        in_specs.append(_const_spec((tm, d)))
        args.append(tail)
    in_specs += [
        _const_spec((1, d)),
        pl.BlockSpec((d, dff), lambda i: (0, 0), pipeline_mode=pl.Buffered(1)),
        pl.BlockSpec((d, dff), lambda i: (0, 1), pipeline_mode=pl.Buffered(1)),
        _const_spec((dff, d)),
    ]
    args += [norm_w.reshape(1, d), w_in, w_in, w_out]
    if final:
        in_specs.append(_const_spec((1, d)))
        args.append(final_w.reshape(1, d))
    (out,), cast = _call(
        functools.partial(_ffn_kernel, n_x_tiles=n_x_tiles, first=first, final=final),
        name="ffn_first" if first else "ffn_final" if final else "ffn",
        grid=(n_tiles,), in_specs=in_specs, args=args,
        out_specs=[pl.BlockSpec((tm, d), lambda i: (i, 0))],
        out_shapes=[jax.ShapeDtypeStruct((n_tiles * tm, d), F32)], casts=casts)
    return out, cast


def _tile_first_pos(i, tm, seq, n_x_tiles):
    return jnp.where(i >= n_x_tiles, -LEAD_ZERO, (i * tm) % seq + N_META)


def _chunk_block(b, c, *, chunks_per_batch, lead_block):
    first = jnp.where(b == 0, lead_block, b * chunks_per_batch)
    return jnp.where(c == 0, first, b * chunks_per_batch + c - 1)


def _recurrence_specs(C, widths, d, *, batch, seq):
    cpb = seq // C
    assert seq % C == 0 and LEAD_ZERO // C == (LEAD - 1) // C
    lead_block = batch * cpb + LEAD_ZERO // C
    idx = lambda b, c: (_chunk_block(b, c, chunks_per_batch=cpb, lead_block=lead_block), 0)
    return (batch, cpb + 1), [pl.BlockSpec((C, w), idx) for w in widths], pl.BlockSpec((C, d), idx)


def _chunk_first_pos(c, C):
    return jnp.where(c == 0, LEAD_ZERO // C * C - LEAD_ZERO, (c - 1) * C + N_META)


def _ret_mixer_kernel(h_ref, nw_ref, w_ref, inv_ref, hn_ref, wo_ref, o_ref,
                      s_ref, slead_ref, dec_ref, dq_ref, dk_ref, cosr_ref, sinr_ref, obuf_ref, *, dk, dv):
    b, c = pl.program_id(0), pl.program_id(1)
    C = h_ref.shape[0]
    nh = RET_HEADS
    half = dk // 2
    k0, v0, g0 = nh * dk, 2 * nh * dk, 2 * nh * dk + nh * dv

    @pl.when((b == 0) & (c == 0))
    def _():
        s_ref[...] = jnp.zeros_like(s_ref)
        row = lax.broadcasted_iota(jnp.int32, (C, C), 0)
        col = lax.broadcasted_iota(jnp.int32, (C, C), 1)
        rel = (row - col).astype(F32)
        rowl = lax.broadcasted_iota(jnp.int32, (C, half), 0).astype(F32)
        ang = rowl * inv_ref[...]
        cosr_ref[...] = jnp.cos(ang)
        sinr_ref[...] = jnp.sin(ang)
        for hd in range(nh):
            lg = RET_LOG_GAMMA[hd]
            dec_ref[hd] = jnp.where(rel >= 0, jnp.exp(lg * jnp.maximum(rel, 0.0)), 0.0)
            dq_ref[hd] = jnp.exp(lg * (rowl + 1.0))
            dk_ref[hd] = jnp.exp(lg * (C - 1.0 - rowl))

    @pl.when((b > 0) & (c == 0))
    def _():
        s_ref[...] = slead_ref[...]

    @pl.when((b == 0) | (c > 0))
    def _():
        ang0 = _chunk_first_pos(c, C).astype(F32) * inv_ref[...]
        cb, sb = jnp.cos(ang0), jnp.sin(ang0)
        cos = cb * cosr_ref[...] - sb * sinr_ref[...]
        sin = sb * cosr_ref[...] + cb * sinr_ref[...]

        def rotary(y):
            t1, t2 = y[:, :half], y[:, half:]
            return jnp.concatenate([t1 * cos - t2 * sin, t1 * sin + t2 * cos], axis=1)

        x = h_ref[...]
        xn = _rms(x, nw_ref[...]).astype(BF16)
        for hd in range(nh):
            q = rotary(_dot(xn, w_ref[:, hd * dk:(hd + 1) * dk]))
            k = rotary(_dot(xn, w_ref[:, k0 + hd * dk:k0 + (hd + 1) * dk])) * dk ** -0.5
            v = _dot(xn, w_ref[:, v0 + hd * dv:v0 + (hd + 1) * dv]).astype(BF16)
            g = _dot(xn, w_ref[:, g0 + hd * dv:g0 + (hd + 1) * dv])
            qd = (q * jnp.concatenate([dq_ref[hd]] * 2, axis=1)).astype(BF16)
            kd = (k * jnp.concatenate([dk_ref[hd]] * 2, axis=1)).astype(BF16)
            s = (_dot_nt(q.astype(BF16), k.astype(BF16)) * dec_ref[hd]).astype(BF16)
            st = s_ref[hd]
            o = _dot(s, v) + _dot(qd, st.astype(BF16))
            s_ref[hd] = st * math.exp(RET_LOG_GAMMA[hd] * C) + _dot_tn(kd, v)
            on = _rms(o, hn_ref[hd:hd + 1, :]) * _silu(g)
            obuf_ref[:, hd * dv:(hd + 1) * dv] = on.astype(BF16)
        o_ref[...] = x + _dot(obuf_ref[...], wo_ref[...])

    @pl.when((b == 0) & (c == 0))
    def _():
        slead_ref[...] = s_ref[...]


def _ret_mixer(h, norm_w, w_in, w_out, head_norm, inv, *, batch, seq, casts=()):
    rows, d = h.shape
    n = w_in.shape[1]
    nh, dv = head_norm.shape
    dk = (n - 2 * nh * dv) // (2 * nh)
    assert dk == 2 * LANES
    C = RET_CHUNK
    grid, (h_spec,), o_spec = _recurrence_specs(C, (d,), d, batch=batch, seq=seq)
    (out,), cast = _call(
        functools.partial(_ret_mixer_kernel, dk=dk, dv=dv), name="ret_mixer", grid=grid,
        in_specs=[h_spec, _const_spec((1, d)), _const_spec((d, n)), _const_spec((1, dk // 2)),
                  _const_spec((nh, dv)), _const_spec((nh * dv, d))],
        args=[h, norm_w.reshape(1, d), w_in, inv.reshape(1, dk // 2), head_norm, w_out],
        out_specs=[o_spec], out_shapes=[jax.ShapeDtypeStruct((rows, d), F32)],
        aliases={0: 0},
        scratch_shapes=[
            pltpu.VMEM((nh, dk, dv), F32),
            pltpu.VMEM((nh, dk, dv), F32),
            pltpu.VMEM((nh, C, C), F32),
            pltpu.VMEM((nh, C, dk // 2), F32),
            pltpu.VMEM((nh, C, dk // 2), F32),
            pltpu.VMEM((C, dk // 2), F32),
            pltpu.VMEM((C, dk // 2), F32),
            pltpu.VMEM((C, nh * dv), BF16),
        ], casts=casts)
    return out, cast


def _gla_proj_kernel(h_ref, nw_ref, w_ref, wg_ref, bg_ref, o_ref, la_ref, *, tm, seq, n_x_tiles, n_main):
    i = pl.program_id(0)
    hk = wg_ref.shape[1]
    dk = hk // GLA_HEADS
    hn = _rms(h_ref[...], nw_ref[...]).astype(BF16)
    step = 512
    for lo in range(0, n_main, step):
        y = _dot(hn, w_ref[:, lo:lo + step])
        if lo < hk:
            y = y * dk ** -0.5
        elif lo >= n_main - (n_main - 2 * hk) // 2:
            y = _silu(y)
        o_ref[:, lo:lo + step] = y.astype(BF16)
    z = _dot(hn, w_ref[:, n_main:n_main + GLA_RANK])
    xg = _dot(z.astype(BF16), wg_ref[...]) + bg_ref[...]
    ls = jnp.minimum(xg, 0.0) - jnp.log1p(jnp.exp(-jnp.abs(xg)))
    pos = _tile_first_pos(i, tm, seq, n_x_tiles) + lax.broadcasted_iota(jnp.int32, (tm, hk), 0)
    la_ref[...] = jnp.where(pos >= 0, ls * (1.0 / GLA_TAU), 0.0)


def _gla_proj(h, norm_w, w_in, w_gate, b_gate, *, tm, seq, casts=()):
    rows, d = h.shape
    n = w_in.shape[1]
    n_main = n - GLA_RANK
    hk = w_gate.shape[1]
    (p, la), cast = _call(
        functools.partial(_gla_proj_kernel, tm=tm, seq=seq, n_x_tiles=rows // tm - 1, n_main=n_main),
        name="gla_proj", grid=(rows // tm,),
        in_specs=[pl.BlockSpec((tm, d), lambda i: (i, 0)), _const_spec((1, d)), _const_spec((d, n)),
                  _const_spec((GLA_RANK, hk)), _const_spec((1, hk))],
        args=[h, norm_w.reshape(1, d), w_in, w_gate, b_gate.reshape(1, hk)],
        out_specs=[pl.BlockSpec((tm, n_main), lambda i: (i, 0)), pl.BlockSpec((tm, hk), lambda i: (i, 0))],
        out_shapes=[jax.ShapeDtypeStruct((rows, n_main), BF16), jax.ShapeDtypeStruct((rows, hk), F32)],
        casts=casts)
    return p, la, cast


def _gla_core_kernel(p_ref, la_ref, h_ref, hn_ref, wo_ref, o_ref,
                     st_ref, stlead_ref, ball_ref, obuf_ref, *, dk, dv):
    bi, c = pl.program_id(0), pl.program_id(1)

    @pl.when((bi == 0) & (c == 0))
    def _():
        st_ref[...] = jnp.zeros_like(st_ref)

    @pl.when((bi > 0) & (c == 0))
    def _():
        st_ref[...] = stlead_ref[...]

    @pl.when((bi == 0) | (c > 0))
    def _():
        C = GLA_CHUNK
        for n in range(p_ref.shape[0] // C):
            rows = pl.ds(n * C, C)
            _gla_chunk(p_ref.at[rows, :], la_ref.at[rows, :], hn_ref, st_ref, ball_ref.at[n],
                       obuf_ref.at[rows, :], dk=dk, dv=dv)
        o_ref[...] = h_ref[...] + _dot(obuf_ref[...], wo_ref[...])

    @pl.when((bi == 0) & (c == 0))
    def _():
        stlead_ref[...] = st_ref[...]


def _left_block_end_rows(b_ref, s):
    C, dk = b_ref.shape
    sub = 8
    bcast = lambda e, n: jnp.broadcast_to(b_ref[e:e + 1, :], (n, dk))
    if 2 * s >= sub:
        n = max(2 * s, sub)
        return jnp.concatenate([bcast(e, n) for e in range(s - 1, C, n)], axis=0)
    r = lax.broadcasted_iota(jnp.int32, (sub, dk), 0)
    tiles = []
    for t0 in range(0, C, sub):
        tile = bcast(t0 + s - 1, sub)
        for blk in range(2 * s, sub, 2 * s):
            tile = jnp.where(r >= blk, bcast(t0 + blk + s - 1, sub), tile)
        tiles.append(tile)
    return jnp.concatenate(tiles, axis=0)


def _gla_chunk(p_ref, la_ref, hn_ref, st_ref, ball_ref, obuf_ref, *, dk, dv):
    C = p_ref.shape[0]
    nh = GLA_HEADS
    k0, v0, g0 = nh * dk, 2 * nh * dk, 2 * nh * dk + nh * dv

    rowl = lax.broadcasted_iota(jnp.int32, (C, dk), 0)
    row = lax.broadcasted_iota(jnp.int32, (C, C), 0)
    col = lax.broadcasted_iota(jnp.int32, (C, C), 1)

    differ = jnp.bitwise_xor(row, col)
    level = jnp.full((C, C), -1, jnp.int32)
    for lv in range(C.bit_length() - 1):
        level = jnp.where((row > col) & (differ >= (1 << lv)), lv, level)

    a_all = la_ref[...]
    a1 = a_all.astype(BF16)
    r1 = a_all - a1.astype(F32)
    a2 = r1.astype(BF16)
    a3 = (r1 - a2.astype(F32)).astype(BF16)
    tri = (row >= col).astype(BF16)
    ball_ref[...] = _dot(tri, a1) + _dot(tri, a2) + _dot(tri, a3)

    for hd in range(nh):
        qb = p_ref[:, hd * dk:(hd + 1) * dk]
        kb = p_ref[:, k0 + hd * dk:k0 + (hd + 1) * dk]
        q = qb.astype(F32)
        k = kb.astype(F32)
        v = p_ref[:, v0 + hd * dv:v0 + (hd + 1) * dv]
        sg = p_ref[:, g0 + hd * dv:g0 + (hd + 1) * dv]
        b_ref = ball_ref.at[:, hd * dk:(hd + 1) * dk]
        b = b_ref[...]

        scores = jnp.where(col == row, _dot_nt(qb, kb), 0.0)
        for lv in range(C.bit_length() - 1):
            s = 1 << lv
            right = (rowl & s) != 0
            if s == 1:
                w = jnp.where(right, jnp.exp(la_ref[:, hd * dk:(hd + 1) * dk]), 1.0)
            else:
                w = jnp.exp(-jnp.abs(b - _left_block_end_rows(b_ref, s)))
            z = (jnp.where(right, q, k) * w).astype(BF16)
            scores = jnp.where(level == lv, _dot_nt(z, z), scores)

        btot = b_ref[C - 1:C, :]
        st = st_ref[hd]
        o = _dot(scores.astype(BF16), v) + _dot_nt((q * jnp.exp(b)).astype(BF16), st.astype(BF16))
        kt = (k * jnp.exp(btot - b)).astype(BF16)
        st_ref[hd] = st * jnp.exp(btot) + _dot_tn(v, kt)
        on = _rms(o, hn_ref[hd:hd + 1, :]) * sg.astype(F32)
        obuf_ref[:, hd * dv:(hd + 1) * dv] = on.astype(BF16)


def _gla_core(p, la, h, head_norm, w_out, *, batch, seq, casts=()):
    rows, d = h.shape
    n = p.shape[1]
    nh, dv = head_norm.shape
    hk = la.shape[1]
    dk = hk // nh
    per_step = GLA_CHUNKS_PER_STEP
    C = per_step * GLA_CHUNK
    grid, (p_spec, la_spec, h_spec), o_spec = _recurrence_specs(C, (n, hk, d), d, batch=batch, seq=seq)
    (out,), cast = _call(
        functools.partial(_gla_core_kernel, dk=dk, dv=dv), name="gla_core", grid=grid,
        in_specs=[p_spec, la_spec, h_spec, _const_spec((nh, dv)), _const_spec((nh * dv, d))],
        args=[p, la, h, head_norm, w_out],
        out_specs=[o_spec], out_shapes=[jax.ShapeDtypeStruct((rows, d), F32)],
        aliases={2: 0},
        scratch_shapes=[
            pltpu.VMEM((nh, dv, dk), F32),
            pltpu.VMEM((nh, dv, dk), F32),
            pltpu.VMEM((per_step, GLA_CHUNK, hk), F32),
            pltpu.VMEM((C, nh * dv), BF16),
        ], casts=casts)
    return out, cast


def kernel(x, meta_tokens, norm_ffn1, ffn1_w_in, ffn1_w_out, norm_mix, norm_ffn2, ffn2_w_in, ffn2_w_out, ret_w_in, ret_head_norm, ret_w_out, gla_w_in, gla_w_gate, gla_b_gate, gla_head_norm, gla_w_out, final_norm):
    batch, seq, d = x.shape
    depth = norm_ffn1.shape[0]
    tm = ROW_TILE
    assert meta_tokens.shape == (N_META, d) and seq % tm == 0 and tm >= LEAD

    tail = jnp.zeros((tm, d), x.dtype).at[LEAD_ZERO:LEAD].set(meta_tokens.astype(x.dtype))
    h = x.reshape(batch * seq, d)

    half = ret_w_in.shape[2] // 6 // RET_HEADS // 2
    assert half == LANES
    inv = 1.0 / (ROPE_BASE ** jnp.linspace(0.0, 1.0, half, dtype=F32))

    stages = []
    for i in range(depth):
        j = i // 2
        stages.append([(ffn1_w_in, i), (ffn1_w_out, i)])
        if i % 2 == 0:
            stages.append([(ret_w_in, j), (ret_w_out, j)])
        else:
            stages += [[(gla_w_in, j), (gla_w_gate, j)], [(gla_w_out, j)]]
        stages.append([(ffn2_w_in, i), (ffn2_w_out, i)])
    stages.append([])
    nxt = iter(stages[1:])

    w = _cast_weights(stages[0])
    for i in range(depth):
        j = i // 2
        h, w = _ffn(h, norm_ffn1[i], *w, tm=tm, tail=tail if i == 0 else None, casts=next(nxt))
        if i % 2 == 0:
            h, w = _ret_mixer(h, norm_mix[i], *w, ret_head_norm[j], inv, batch=batch, seq=seq,
                              casts=next(nxt))
        else:
            p, la, w = _gla_proj(h, norm_mix[i], *w, gla_b_gate[j], tm=tm, seq=seq, casts=next(nxt))
            h, w = _gla_core(p, la, h, gla_head_norm[j], *w, batch=batch, seq=seq, casts=next(nxt))
        h, w = _ffn(h, norm_ffn2[i], *w, tm=tm, final_w=final_norm if i == depth - 1 else None,
                    casts=next(nxt))
    return h.reshape(batch, seq, d)
```

```python
import functools
import math

import jax
import jax.numpy as jnp
from jax import lax
from jax.experimental import pallas as pl
from jax.experimental.pallas import tpu as pltpu

F32 = jnp.float32
BF16 = jnp.bfloat16

EPS = 1e-6
N_META = 16
LEAD = 256
LEAD_ZERO = LEAD - N_META
ROW_TILE = 512
ROPE_BASE = 10000.0
RET_HEADS = 4
GLA_HEADS = 4
GLA_RANK = 16
GLA_TAU = 16.0
RET_CHUNK = 256
GLA_CHUNK = 128
GLA_CHUNKS_PER_STEP = 4
LANES = 128
BF16_SUBLANES = 16
VMEM_LIMIT = 56 * 1024 * 1024

RET_LOG_GAMMA = tuple(math.log1p(-2.0 ** (-5.0 - h)) for h in range(RET_HEADS))


def _const_spec(shape):
    nd = len(shape)
    return pl.BlockSpec(shape, lambda *_: (0,) * nd, pipeline_mode=pl.Buffered(1))


def _rms(x, w):
    ms = jnp.mean(x * x, axis=-1, keepdims=True)
    return x * lax.rsqrt(ms + EPS) * w


def _silu(x):
    return x * jax.nn.sigmoid(x)


def _dot(a, b):
    return jnp.dot(a, b, preferred_element_type=F32)


def _dot_nt(a, b):
    return lax.dot_general(a, b, (((1,), (1,)), ((), ())), preferred_element_type=F32)


def _dot_tn(a, b):
    return lax.dot_general(a, b, (((0,), (0,)), ((), ())), preferred_element_type=F32)


def _cast_blocks(rows, n_steps):
    units = rows // BF16_SUBLANES
    assert rows % BF16_SUBLANES == 0
    return max(n for n in range(1, min(units, n_steps) + 1) if units % n == 0)


def _call(body, *, name, grid, in_specs, args, out_specs, out_shapes, scratch_shapes=(), aliases=None,
          casts=()):
    n_in, n_out, n_cast = len(args), len(out_shapes), len(casts)
    n_steps = math.prod(grid)
    linear = (lambda i: i) if len(grid) == 1 else (lambda b, c: b * grid[1] + c)
    in_specs, out_specs, out_shapes, args = list(in_specs), list(out_specs), list(out_shapes), list(args)
    for stack, layer in casts:
        _, rows, cols = stack.shape
        nb = _cast_blocks(rows, n_steps)
        rb = rows // nb
        blk = lambda *g, nb=nb: jnp.minimum(linear(*g), nb - 1)
        in_specs.append(pl.BlockSpec((None, rb, cols), lambda *g, blk=blk, layer=layer: (layer, blk(*g), 0)))
        out_specs.append(pl.BlockSpec((rb, cols), lambda *g, blk=blk: (blk(*g), 0)))
        out_shapes.append(jax.ShapeDtypeStruct((rows, cols), BF16))
        args.append(stack)

    def kern(*refs):
        ins, rest = refs[:n_in], refs[n_in:]
        cast_in, rest = rest[:n_cast], rest[n_cast:]
        outs, rest = rest[:n_out], rest[n_out:]
        cast_out, scratch = rest[:n_cast], rest[n_cast:]
        for src, dst in zip(cast_in, cast_out):
            dst[...] = src[...].astype(BF16)
        body(*ins, *outs, *scratch)

    res = pl.pallas_call(
        kern, grid=grid, in_specs=in_specs, out_specs=out_specs, out_shape=out_shapes,
        scratch_shapes=list(scratch_shapes), input_output_aliases=aliases or {},
        compiler_params=pltpu.CompilerParams(
            dimension_semantics=("arbitrary",) * len(grid), vmem_limit_bytes=VMEM_LIMIT),
        name=name,
    )(*args)
    return res[:n_out], res[n_out:]


def _cast_weights(casts, *, n_steps=8):
    _, out = _call(lambda: None, name="cast_weights", grid=(n_steps,), in_specs=[], args=[],
                   out_specs=[], out_shapes=[], casts=casts)
    return out


def _ffn_kernel(x_ref, *rest, n_x_tiles, first, final):
    rest = list(rest)
    tail_ref = rest.pop(0) if first else None
    nw_ref, wg_ref, wu_ref, wo_ref = rest[:4]
    fw_ref = rest[4] if final else None
    o_ref = rest[-1]
    x = x_ref[...]
    if first:
        x = jnp.where(pl.program_id(0) >= n_x_tiles, tail_ref[...], x)
    xn = _rms(x, nw_ref[...]).astype(BF16)
    g = _dot(xn, wg_ref[...])
    u = _dot(xn, wu_ref[...])
    hid = (_silu(g) * u).astype(BF16)
    y = x + 0.5 * _dot(hid, wo_ref[...])
    if final:
        y = _rms(y, fw_ref[...])
    o_ref[...] = y


def _ffn(h, norm_w, w_in, w_out, *, tm, tail=None, final_w=None, casts=()):
    d = h.shape[1]
    dff = w_out.shape[0]
    first, final = tail is not None, final_w is not None
    n_x_tiles = (h.shape[0] if first else h.shape[0] - tm) // tm
    n_tiles = n_x_tiles if final else n_x_tiles + 1
    in_specs = [pl.BlockSpec((tm, d), lambda i: (jnp.minimum(i, n_x_tiles - 1), 0) if first else (i, 0))]
    args = [h]
    if first:
        in_specs.append(_const_spec((tm, d)))
        args.append(tail)
    in_specs += [
        _const_spec((1, d)),
        pl.BlockSpec((d, dff), lambda i: (0, 0), pipeline_mode=pl.Buffered(1)),
        pl.BlockSpec((d, dff), lambda i: (0, 1), pipeline_mode=pl.Buffered(1)),
        _const_spec((dff, d)),
    ]
    args += [norm_w.reshape(1, d), w_in, w_in, w_out]
    if final:
        in_specs.append(_const_spec((1, d)))
        args.append(final_w.reshape(1, d))
    (out,), cast = _call(
        functools.partial(_ffn_kernel, n_x_tiles=n_x_tiles, first=first, final=final),
        name="ffn_first" if first else "ffn_final" if final else "ffn",
        grid=(n_tiles,), in_specs=in_specs, args=args,
        out_specs=[pl.BlockSpec((tm, d), lambda i: (i, 0))],
        out_shapes=[jax.ShapeDtypeStruct((n_tiles * tm, d), F32)], casts=casts)
    return out, cast


def _tile_first_pos(i, tm, seq, n_x_tiles):
    return jnp.where(i >= n_x_tiles, -LEAD_ZERO, (i * tm) % seq + N_META)


def _chunk_block(b, c, *, chunks_per_batch, lead_block):
    first = jnp.where(b == 0, lead_block, b * chunks_per_batch)
    return jnp.where(c == 0, first, b * chunks_per_batch + c - 1)


def _recurrence_specs(C, widths, d, *, batch, seq):
    cpb = seq // C
    assert seq % C == 0 and LEAD_ZERO // C == (LEAD - 1) // C
    lead_block = batch * cpb + LEAD_ZERO // C
    idx = lambda b, c: (_chunk_block(b, c, chunks_per_batch=cpb, lead_block=lead_block), 0)
    return (batch, cpb + 1), [pl.BlockSpec((C, w), idx) for w in widths], pl.BlockSpec((C, d), idx)


def _chunk_first_pos(c, C):
    return jnp.where(c == 0, LEAD_ZERO // C * C - LEAD_ZERO, (c - 1) * C + N_META)


def _ret_mixer_kernel(h_ref, nw_ref, w_ref, inv_ref, hn_ref, wo_ref, o_ref,
                      s_ref, slead_ref, dec_ref, dq_ref, dk_ref, cosr_ref, sinr_ref, obuf_ref, *, dk, dv):
    b, c = pl.program_id(0), pl.program_id(1)
    C = h_ref.shape[0]
    nh = RET_HEADS
    half = dk // 2
    k0, v0, g0 = nh * dk, 2 * nh * dk, 2 * nh * dk + nh * dv

    @pl.when((b == 0) & (c == 0))
    def _():
        s_ref[...] = jnp.zeros_like(s_ref)
        row = lax.broadcasted_iota(jnp.int32, (C, C), 0)
        col = lax.broadcasted_iota(jnp.int32, (C, C), 1)
        rel = (row - col).astype(F32)
        rowl = lax.broadcasted_iota(jnp.int32, (C, half), 0).astype(F32)
        ang = rowl * inv_ref[...]
        cosr_ref[...] = jnp.cos(ang)
        sinr_ref[...] = jnp.sin(ang)
        for hd in range(nh):
            lg = RET_LOG_GAMMA[hd]
            dec_ref[hd] = jnp.where(rel >= 0, jnp.exp(lg * jnp.maximum(rel, 0.0)), 0.0)
            dq_ref[hd] = jnp.exp(lg * (rowl + 1.0))
            dk_ref[hd] = jnp.exp(lg * (C - 1.0 - rowl))

    @pl.when((b > 0) & (c == 0))
    def _():
        s_ref[...] = slead_ref[...]

    @pl.when((b == 0) | (c > 0))
    def _():
        ang0 = _chunk_first_pos(c, C).astype(F32) * inv_ref[...]
        cb, sb = jnp.cos(ang0), jnp.sin(ang0)
        cos = cb * cosr_ref[...] - sb * sinr_ref[...]
        sin = sb * cosr_ref[...] + cb * sinr_ref[...]

        def rotary(y):
            t1, t2 = y[:, :half], y[:, half:]
            return jnp.concatenate([t1 * cos - t2 * sin, t1 * sin + t2 * cos], axis=1)

        x = h_ref[...]
        xn = _rms(x, nw_ref[...]).astype(BF16)
        for hd in range(nh):
            q = rotary(_dot(xn, w_ref[:, hd * dk:(hd + 1) * dk]))
            k = rotary(_dot(xn, w_ref[:, k0 + hd * dk:k0 + (hd + 1) * dk])) * dk ** -0.5
            v = _dot(xn, w_ref[:, v0 + hd * dv:v0 + (hd + 1) * dv]).astype(BF16)
            g = _dot(xn, w_ref[:, g0 + hd * dv:g0 + (hd + 1) * dv])
            qd = (q * jnp.concatenate([dq_ref[hd]] * 2, axis=1)).astype(BF16)
            kd = (k * jnp.concatenate([dk_ref[hd]] * 2, axis=1)).astype(BF16)
            s = (_dot_nt(q.astype(BF16), k.astype(BF16)) * dec_ref[hd]).astype(BF16)
            st = s_ref[hd]
            o = _dot(s, v) + _dot(qd, st.astype(BF16))
            s_ref[hd] = st * math.exp(RET_LOG_GAMMA[hd] * C) + _dot_tn(kd, v)
            on = _rms(o, hn_ref[hd:hd + 1, :]) * _silu(g)
            obuf_ref[:, hd * dv:(hd + 1) * dv] = on.astype(BF16)
        o_ref[...] = x + _dot(obuf_ref[...], wo_ref[...])

    @pl.when((b == 0) & (c == 0))
    def _():
        slead_ref[...] = s_ref[...]


def _ret_mixer(h, norm_w, w_in, w_out, head_norm, inv, *, batch, seq, casts=()):
    rows, d = h.shape
    n = w_in.shape[1]
    nh, dv = head_norm.shape
    dk = (n - 2 * nh * dv) // (2 * nh)
    assert dk == 2 * LANES
    C = RET_CHUNK
    grid, (h_spec,), o_spec = _recurrence_specs(C, (d,), d, batch=batch, seq=seq)
    (out,), cast = _call(
        functools.partial(_ret_mixer_kernel, dk=dk, dv=dv), name="ret_mixer", grid=grid,
        in_specs=[h_spec, _const_spec((1, d)), _const_spec((d, n)), _const_spec((1, dk // 2)),
                  _const_spec((nh, dv)), _const_spec((nh * dv, d))],
        args=[h, norm_w.reshape(1, d), w_in, inv.reshape(1, dk // 2), head_norm, w_out],
        out_specs=[o_spec], out_shapes=[jax.ShapeDtypeStruct((rows, d), F32)],
        aliases={0: 0},
        scratch_shapes=[
            pltpu.VMEM((nh, dk, dv), F32),
            pltpu.VMEM((nh, dk, dv), F32),
            pltpu.VMEM((nh, C, C), F32),
            pltpu.VMEM((nh, C, dk // 2), F32),
            pltpu.VMEM((nh, C, dk // 2), F32),
            pltpu.VMEM((C, dk // 2), F32),
            pltpu.VMEM((C, dk // 2), F32),
            pltpu.VMEM((C, nh * dv), BF16),
        ], casts=casts)
    return out, cast


def _gla_proj_kernel(h_ref, nw_ref, w_ref, wg_ref, bg_ref, o_ref, la_ref, *, tm, seq, n_x_tiles, n_main):
    i = pl.program_id(0)
    hk = wg_ref.shape[1]
    dk = hk // GLA_HEADS
    hn = _rms(h_ref[...], nw_ref[...]).astype(BF16)
    step = 512
    for lo in range(0, n_main, step):
        y = _dot(hn, w_ref[:, lo:lo + step])
        if lo < hk:
            y = y * dk ** -0.5
        elif lo >= n_main - (n_main - 2 * hk) // 2:
            y = _silu(y)
        o_ref[:, lo:lo + step] = y.astype(BF16)
    z = _dot(hn, w_ref[:, n_main:n_main + GLA_RANK])
    xg = _dot(z.astype(BF16), wg_ref[...]) + bg_ref[...]
    ls = jnp.minimum(xg, 0.0) - jnp.log1p(jnp.exp(-jnp.abs(xg)))
    pos = _tile_first_pos(i, tm, seq, n_x_tiles) + lax.broadcasted_iota(jnp.int32, (tm, hk), 0)
    real = (pos >= 0) & ((i < n_x_tiles) | (pos < N_META))
    la_ref[...] = jnp.where(real, ls * (1.0 / GLA_TAU), 0.0)


def _gla_proj(h, norm_w, w_in, w_gate, b_gate, *, tm, seq, casts=()):
    rows, d = h.shape
    n = w_in.shape[1]
    n_main = n - GLA_RANK
    hk = w_gate.shape[1]
    (p, la), cast = _call(
        functools.partial(_gla_proj_kernel, tm=tm, seq=seq, n_x_tiles=rows // tm - 1, n_main=n_main),
        name="gla_proj", grid=(rows // tm,),
        in_specs=[pl.BlockSpec((tm, d), lambda i: (i, 0)), _const_spec((1, d)), _const_spec((d, n)),
                  _const_spec((GLA_RANK, hk)), _const_spec((1, hk))],
        args=[h, norm_w.reshape(1, d), w_in, w_gate, b_gate.reshape(1, hk)],
        out_specs=[pl.BlockSpec((tm, n_main), lambda i: (i, 0)), pl.BlockSpec((tm, hk), lambda i: (i, 0))],
        out_shapes=[jax.ShapeDtypeStruct((rows, n_main), BF16), jax.ShapeDtypeStruct((rows, hk), F32)],
        casts=casts)
    return p, la, cast


def _gla_core_kernel(p_ref, la_ref, h_ref, hn_ref, wo_ref, o_ref,
                     st_ref, stlead_ref, ball_ref, obuf_ref, *, dk, dv):
    bi, c = pl.program_id(0), pl.program_id(1)

    @pl.when((bi == 0) & (c == 0))
    def _():
        st_ref[...] = jnp.zeros_like(st_ref)

    @pl.when((bi > 0) & (c == 0))
    def _():
        st_ref[...] = stlead_ref[...]

    @pl.when((bi == 0) | (c > 0))
    def _():
        C = GLA_CHUNK
        for n in range(p_ref.shape[0] // C):
            rows = pl.ds(n * C, C)
            _gla_chunk(p_ref.at[rows, :], la_ref.at[rows, :], hn_ref, st_ref, ball_ref.at[n],
                       obuf_ref.at[rows, :], dk=dk, dv=dv)
        o_ref[...] = h_ref[...] + _dot(obuf_ref[...], wo_ref[...])

    @pl.when((bi == 0) & (c == 0))
    def _():
        stlead_ref[...] = st_ref[...]


def _left_block_end_rows(b_ref, s):
    C, dk = b_ref.shape
    sub = 8
    bcast = lambda e, n: jnp.broadcast_to(b_ref[e:e + 1, :], (n, dk))
    if 2 * s >= sub:
        n = max(2 * s, sub)
        return jnp.concatenate([bcast(e, n) for e in range(s - 1, C, n)], axis=0)
    r = lax.broadcasted_iota(jnp.int32, (sub, dk), 0)
    tiles = []
    for t0 in range(0, C, sub):
        tile = bcast(t0 + s - 1, sub)
        for blk in range(2 * s, sub, 2 * s):
            tile = jnp.where(r >= blk, bcast(t0 + blk + s - 1, sub), tile)
        tiles.append(tile)
    return jnp.concatenate(tiles, axis=0)


def _gla_chunk(p_ref, la_ref, hn_ref, st_ref, ball_ref, obuf_ref, *, dk, dv):
    C = p_ref.shape[0]
    nh = GLA_HEADS
    k0, v0, g0 = nh * dk, 2 * nh * dk, 2 * nh * dk + nh * dv

    rowl = lax.broadcasted_iota(jnp.int32, (C, dk), 0)
    row = lax.broadcasted_iota(jnp.int32, (C, C), 0)
    col = lax.broadcasted_iota(jnp.int32, (C, C), 1)

    differ = jnp.bitwise_xor(row, col)
    level = jnp.full((C, C), -1, jnp.int32)
    for lv in range(C.bit_length() - 1):
        level = jnp.where((row > col) & (differ >= (1 << lv)), lv, level)

    a_all = la_ref[...]
    a1 = a_all.astype(BF16)
    r1 = a_all - a1.astype(F32)
    a2 = r1.astype(BF16)
    a3 = (r1 - a2.astype(F32)).astype(BF16)
    tri = (row >= col).astype(BF16)
    ball_ref[...] = _dot(tri, a1) + _dot(tri, a2) + _dot(tri, a3)

    for hd in range(nh):
        qb = p_ref[:, hd * dk:(hd + 1) * dk]
        kb = p_ref[:, k0 + hd * dk:k0 + (hd + 1) * dk]
        q = qb.astype(F32)
        k = kb.astype(F32)
        v = p_ref[:, v0 + hd * dv:v0 + (hd + 1) * dv]
        sg = p_ref[:, g0 + hd * dv:g0 + (hd + 1) * dv]
        b_ref = ball_ref.at[:, hd * dk:(hd + 1) * dk]
        b = b_ref[...]

        scores = jnp.where(col == row, _dot_nt(qb, kb), 0.0)
        for lv in range(C.bit_length() - 1):
            s = 1 << lv
            right = (rowl & s) != 0
            if s == 1:
                w = jnp.where(right, jnp.exp(la_ref[:, hd * dk:(hd + 1) * dk]), 1.0)
            else:
                w = jnp.exp(-jnp.abs(b - _left_block_end_rows(b_ref, s)))
            z = (jnp.where(right, q, k) * w).astype(BF16)
            scores = jnp.where(level == lv, _dot_nt(z, z), scores)

        btot = b_ref[C - 1:C, :]
        st = st_ref[hd]
        o = _dot(scores.astype(BF16), v) + _dot_nt((q * jnp.exp(b)).astype(BF16), st.astype(BF16))
        kt = (k * jnp.exp(btot - b)).astype(BF16)
        st_ref[hd] = st * jnp.exp(btot) + _dot_tn(v, kt)
        on = _rms(o, hn_ref[hd:hd + 1, :]) * sg.astype(F32)
        obuf_ref[:, hd * dv:(hd + 1) * dv] = on.astype(BF16)


def _gla_core(p, la, h, head_norm, w_out, *, batch, seq, casts=()):
    rows, d = h.shape
    n = p.shape[1]
    nh, dv = head_norm.shape
    hk = la.shape[1]
    dk = hk // nh
    per_step = GLA_CHUNKS_PER_STEP
    C = per_step * GLA_CHUNK
    grid, (p_spec, la_spec, h_spec), o_spec = _recurrence_specs(C, (n, hk, d), d, batch=batch, seq=seq)
    (out,), cast = _call(
        functools.partial(_gla_core_kernel, dk=dk, dv=dv), name="gla_core", grid=grid,
        in_specs=[p_spec, la_spec, h_spec, _const_spec((nh, dv)), _const_spec((nh * dv, d))],
        args=[p, la, h, head_norm, w_out],
        out_specs=[o_spec], out_shapes=[jax.ShapeDtypeStruct((rows, d), F32)],
        aliases={2: 0},
        scratch_shapes=[
            pltpu.VMEM((nh, dv, dk), F32),
            pltpu.VMEM((nh, dv, dk), F32),
            pltpu.VMEM((per_step, GLA_CHUNK, hk), F32),
            pltpu.VMEM((C, nh * dv), BF16),
        ], casts=casts)
    return out, cast


def kernel(x, meta_tokens, norm_ffn1, ffn1_w_in, ffn1_w_out, norm_mix, norm_ffn2, ffn2_w_in, ffn2_w_out, ret_w_in, ret_head_norm, ret_w_out, gla_w_in, gla_w_gate, gla_b_gate, gla_head_norm, gla_w_out, final_norm):
    batch, seq, d = x.shape
    depth = norm_ffn1.shape[0]
    tm = ROW_TILE
    assert meta_tokens.shape == (N_META, d) and seq % tm == 0 and tm >= LEAD

    tail = jnp.zeros((tm, d), x.dtype).at[LEAD_ZERO:LEAD].set(meta_tokens.astype(x.dtype))
    h = x.reshape(batch * seq, d)

    half = ret_w_in.shape[2] // 6 // RET_HEADS // 2
    assert half == LANES
    inv = 1.0 / (ROPE_BASE ** jnp.linspace(0.0, 1.0, half, dtype=F32))

    stages = []
    for i in range(depth):
        j = i // 2
        stages.append([(ffn1_w_in, i), (ffn1_w_out, i)])
        if i % 2 == 0:
            stages.append([(ret_w_in, j), (ret_w_out, j)])
        else:
            stages += [[(gla_w_in, j), (gla_w_gate, j)], [(gla_w_out, j)]]
        stages.append([(ffn2_w_in, i), (ffn2_w_out, i)])
    stages.append([])
    nxt = iter(stages[1:])

    w = _cast_weights(stages[0])
    for i in range(depth):
        j = i // 2
        h, w = _ffn(h, norm_ffn1[i], *w, tm=tm, tail=tail if i == 0 else None, casts=next(nxt))
        if i % 2 == 0:
            h, w = _ret_mixer(h, norm_mix[i], *w, ret_head_norm[j], inv, batch=batch, seq=seq,
                              casts=next(nxt))
        else:
            p, la, w = _gla_proj(h, norm_mix[i], *w, gla_b_gate[j], tm=tm, seq=seq, casts=next(nxt))
            h, w = _gla_core(p, la, h, gla_head_norm[j], *w, batch=batch, seq=seq, casts=next(nxt))
        h, w = _ffn(h, norm_ffn2[i], *w, tm=tm, final_w=final_norm if i == depth - 1 else None,
                    casts=next(nxt))
    return h.reshape(batch, seq, d)
```

```python
import functools
import math

import jax
import jax.numpy as jnp
from jax import lax
from jax.experimental import pallas as pl
from jax.experimental.pallas import tpu as pltpu

F32 = jnp.float32
BF16 = jnp.bfloat16

EPS = 1e-6
N_META = 16
LEAD = 256
LEAD_ZERO = LEAD - N_META
ROW_TILE = 512
ROPE_BASE = 10000.0
RET_HEADS = 4
GLA_HEADS = 4
GLA_RANK = 16
GLA_TAU = 16.0
RET_CHUNK = 256
GLA_CHUNK = 128
GLA_CHUNKS_PER_STEP = 4
LANES = 128
BF16_SUBLANES = 16
VMEM_LIMIT = 56 * 1024 * 1024

RET_LOG_GAMMA = tuple(math.log1p(-2.0 ** (-5.0 - h)) for h in range(RET_HEADS))


def _const_spec(shape):
    nd = len(shape)
    return pl.BlockSpec(shape, lambda *_: (0,) * nd, pipeline_mode=pl.Buffered(1))


def _rms(x, w):
    ms = jnp.mean(x * x, axis=-1, keepdims=True)
    return x * lax.rsqrt(ms + EPS) * w


def _silu(x):
    return x * jax.nn.sigmoid(x)


def _dot(a, b):
    return jnp.dot(a, b, preferred_element_type=F32)


def _dot_nt(a, b):
    return lax.dot_general(a, b, (((1,), (1,)), ((), ())), preferred_element_type=F32)


def _dot_tn(a, b):
    return lax.dot_general(a, b, (((0,), (0,)), ((), ())), preferred_element_type=F32)


def _cast_blocks(rows, n_steps):
    units = rows // BF16_SUBLANES
    assert rows % BF16_SUBLANES == 0
    return max(n for n in range(1, min(units, n_steps) + 1) if units % n == 0)


def _call(body, *, name, grid, in_specs, args, out_specs, out_shapes, scratch_shapes=(), aliases=None,
          casts=()):
    n_in, n_out, n_cast = len(args), len(out_shapes), len(casts)
    n_steps = math.prod(grid)
    linear = (lambda i: i) if len(grid) == 1 else (lambda b, c: b * grid[1] + c)
    in_specs, out_specs, out_shapes, args = list(in_specs), list(out_specs), list(out_shapes), list(args)
    for stack, layer in casts:
        _, rows, cols = stack.shape
        nb = _cast_blocks(rows, n_steps)
        rb = rows // nb
        blk = lambda *g, nb=nb: jnp.minimum(linear(*g), nb - 1)
        in_specs.append(pl.BlockSpec((None, rb, cols), lambda *g, blk=blk, layer=layer: (layer, blk(*g), 0)))
        out_specs.append(pl.BlockSpec((rb, cols), lambda *g, blk=blk: (blk(*g), 0)))
        out_shapes.append(jax.ShapeDtypeStruct((rows, cols), BF16))
        args.append(stack)

    def kern(*refs):
        ins, rest = refs[:n_in], refs[n_in:]
        cast_in, rest = rest[:n_cast], rest[n_cast:]
        outs, rest = rest[:n_out], rest[n_out:]
        cast_out, scratch = rest[:n_cast], rest[n_cast:]
        for src, dst in zip(cast_in, cast_out):
            dst[...] = src[...].astype(BF16)
        body(*ins, *outs, *scratch)

    res = pl.pallas_call(
        kern, grid=grid, in_specs=in_specs, out_specs=out_specs, out_shape=out_shapes,
        scratch_shapes=list(scratch_shapes), input_output_aliases=aliases or {},
        compiler_params=pltpu.CompilerParams(
            dimension_semantics=("arbitrary",) * len(grid), vmem_limit_bytes=VMEM_LIMIT),
        name=name,
    )(*args)
    return res[:n_out], res[n_out:]


def _cast_weights(casts, *, n_steps=8):
    _, out = _call(lambda: None, name="cast_weights", grid=(n_steps,), in_specs=[], args=[],
                   out_specs=[], out_shapes=[], casts=casts)
    return out


def _ffn_kernel(x_ref, *rest, n_x_tiles, first, final):
    rest = list(rest)
    tail_ref = rest.pop(0) if first else None
    nw_ref, wg_ref, wu_ref, wo_ref = rest[:4]
    fw_ref = rest[4] if final else None
    o_ref = rest[-1]
    x = x_ref[...]
    if first:
        x = jnp.where(pl.program_id(0) >= n_x_tiles, tail_ref[...], x)
    xn = _rms(x, nw_ref[...]).astype(BF16)
    g = _dot(xn, wg_ref[...])
    u = _dot(xn, wu_ref[...])
    hid = (_silu(g) * u).astype(BF16)
    y = x + 0.5 * _dot(hid, wo_ref[...])
    if final:
        y = _rms(y, fw_ref[...])
    o_ref[...] = y


def _ffn(h, norm_w, w_in, w_out, *, tm, tail=None, final_w=None, casts=()):
    d = h.shape[1]
    dff = w_out.shape[0]
    first, final = tail is not None, final_w is not None
    n_x_tiles = (h.shape[0] if first else h.shape[0] - tm) // tm
    n_tiles = n_x_tiles if final else n_x_tiles + 1
    in_specs = [pl.BlockSpec((tm, d), lambda i: (jnp.minimum(i, n_x_tiles - 1), 0) if first else (i, 0))]
    args = [h]
    if first:
        in_specs.append(_const_spec((tm, d)))
        args.append(tail)
    in_specs += [
        _const_spec((1, d)),
        pl.BlockSpec((d, dff), lambda i: (0, 0), pipeline_mode=pl.Buffered(1)),
        pl.BlockSpec((d, dff), lambda i: (0, 1), pipeline_mode=pl.Buffered(1)),
        _const_spec((dff, d)),
    ]
    args += [norm_w.reshape(1, d), w_in, w_in, w_out]
    if final:
        in_specs.append(_const_spec((1, d)))
        args.append(final_w.reshape(1, d))
    (out,), cast = _call(
        functools.partial(_ffn_kernel, n_x_tiles=n_x_tiles, first=first, final=final),
        name="ffn_first" if first else "ffn_final" if final else "ffn",
        grid=(n_tiles,), in_specs=in_specs, args=args,
        out_specs=[pl.BlockSpec((tm, d), lambda i: (i, 0))],
        out_shapes=[jax.ShapeDtypeStruct((n_tiles * tm, d), F32)], casts=casts)
    return out, cast


def _tile_first_pos(i, tm, seq, n_x_tiles):
    return jnp.where(i >= n_x_tiles, -LEAD_ZERO, (i * tm) % seq + N_META)


def _chunk_block(b, c, *, chunks_per_batch, lead_block):
    first = jnp.where(b == 0, lead_block, b * chunks_per_batch)
    return jnp.where(c == 0, first, b * chunks_per_batch + c - 1)


def _recurrence_specs(C, widths, d, *, batch, seq):
    cpb = seq // C
    assert seq % C == 0 and LEAD_ZERO // C == (LEAD - 1) // C
    lead_block = batch * cpb + LEAD_ZERO // C
    idx = lambda b, c: (_chunk_block(b, c, chunks_per_batch=cpb, lead_block=lead_block), 0)
    return (batch, cpb + 1), [pl.BlockSpec((C, w), idx) for w in widths], pl.BlockSpec((C, d), idx)


def _chunk_first_pos(c, C):
    return jnp.where(c == 0, LEAD_ZERO // C * C - LEAD_ZERO, (c - 1) * C + N_META)


def _ret_mixer_kernel(h_ref, nw_ref, w_ref, inv_ref, hn_ref, wo_ref, o_ref,
                      s_ref, slead_ref, dec_ref, dq_ref, dk_ref, cosr_ref, sinr_ref, obuf_ref, *, dk, dv):
    b, c = pl.program_id(0), pl.program_id(1)
    C = h_ref.shape[0]
    nh = RET_HEADS
    half = dk // 2
    k0, v0, g0 = nh * dk, 2 * nh * dk, 2 * nh * dk + nh * dv

    @pl.when((b == 0) & (c == 0))
    def _():
        s_ref[...] = jnp.zeros_like(s_ref)
        row = lax.broadcasted_iota(jnp.int32, (C, C), 0)
        col = lax.broadcasted_iota(jnp.int32, (C, C), 1)
        rel = (row - col).astype(F32)
        rowl = lax.broadcasted_iota(jnp.int32, (C, half), 0).astype(F32)
        ang = rowl * inv_ref[...]
        cosr_ref[...] = jnp.cos(ang)
        sinr_ref[...] = jnp.sin(ang)
        for hd in range(nh):
            lg = RET_LOG_GAMMA[hd]
            dec_ref[hd] = jnp.where(rel >= 0, jnp.exp(lg * jnp.maximum(rel, 0.0)), 0.0)
            dq_ref[hd] = jnp.exp(lg * (rowl + 1.0))
            dk_ref[hd] = jnp.exp(lg * (C - 1.0 - rowl))

    @pl.when((b > 0) & (c == 0))
    def _():
        s_ref[...] = slead_ref[...]

    @pl.when((b == 0) | (c > 0))
    def _():
        ang0 = _chunk_first_pos(c, C).astype(F32) * inv_ref[...]
        cb, sb = jnp.cos(ang0), jnp.sin(ang0)
        cos = cb * cosr_ref[...] - sb * sinr_ref[...]
        sin = sb * cosr_ref[...] + cb * sinr_ref[...]

        def rotary(y):
            t1, t2 = y[:, :half], y[:, half:]
            return jnp.concatenate([t1 * cos - t2 * sin, t1 * sin + t2 * cos], axis=1)

        x = h_ref[...]
        xn = _rms(x, nw_ref[...]).astype(BF16)
        for hd in range(nh):
            q = rotary(_dot(xn, w_ref[:, hd * dk:(hd + 1) * dk]))
            k = rotary(_dot(xn, w_ref[:, k0 + hd * dk:k0 + (hd + 1) * dk])) * dk ** -0.5
            v = _dot(xn, w_ref[:, v0 + hd * dv:v0 + (hd + 1) * dv]).astype(BF16)
            g = _dot(xn, w_ref[:, g0 + hd * dv:g0 + (hd + 1) * dv])
            qd = (q * jnp.concatenate([dq_ref[hd]] * 2, axis=1)).astype(BF16)
            kd = (k * jnp.concatenate([dk_ref[hd]] * 2, axis=1)).astype(BF16)
            s = (_dot_nt(q.astype(BF16), k.astype(BF16)) * dec_ref[hd]).astype(BF16)
            st = s_ref[hd]
            o = _dot(s, v) + _dot(qd, st.astype(BF16))
            s_ref[hd] = st * math.exp(RET_LOG_GAMMA[hd] * C) + _dot_tn(kd, v)
            on = _rms(o, hn_ref[hd:hd + 1, :]) * _silu(g)
            obuf_ref[:, hd * dv:(hd + 1) * dv] = on.astype(BF16)
        o_ref[...] = x + _dot(obuf_ref[...], wo_ref[...])

    @pl.when((b == 0) & (c == 0))
    def _():
        slead_ref[...] = s_ref[...]


def _ret_mixer(h, norm_w, w_in, w_out, head_norm, inv, *, batch, seq, casts=()):
    rows, d = h.shape
    n = w_in.shape[1]
    nh, dv = head_norm.shape
    dk = (n - 2 * nh * dv) // (2 * nh)
    assert dk == 2 * LANES
    C = RET_CHUNK
    grid, (h_spec,), o_spec = _recurrence_specs(C, (d,), d, batch=batch, seq=seq)
    (out,), cast = _call(
        functools.partial(_ret_mixer_kernel, dk=dk, dv=dv), name="ret_mixer", grid=grid,
        in_specs=[h_spec, _const_spec((1, d)), _const_spec((d, n)), _const_spec((1, dk // 2)),
                  _const_spec((nh, dv)), _const_spec((nh * dv, d))],
        args=[h, norm_w.reshape(1, d), w_in, inv.reshape(1, dk // 2), head_norm, w_out],
        out_specs=[o_spec], out_shapes=[jax.ShapeDtypeStruct((rows, d), F32)],
        aliases={0: 0},
        scratch_shapes=[
            pltpu.VMEM((nh, dk, dv), F32),
            pltpu.VMEM((nh, dk, dv), F32),
            pltpu.VMEM((nh, C, C), F32),
            pltpu.VMEM((nh, C, dk // 2), F32),
            pltpu.VMEM((nh, C, dk // 2), F32),
            pltpu.VMEM((C, dk // 2), F32),
            pltpu.VMEM((C, dk // 2), F32),
            pltpu.VMEM((C, nh * dv), BF16),
        ], casts=casts)
    return out, cast


def _gla_proj_kernel(h_ref, nw_ref, w_ref, wg_ref, bg_ref, o_ref, la_ref, *, tm, seq, n_x_tiles, n_main):
    i = pl.program_id(0)
    hk = wg_ref.shape[1]
    dk = hk // GLA_HEADS
    hn = _rms(h_ref[...], nw_ref[...]).astype(BF16)
    z = _dot(hn, w_ref[:, n_main:n_main + GLA_RANK])
    xg = _dot(z.astype(BF16), wg_ref[...]) + bg_ref[...]
    pos = _tile_first_pos(i, tm, seq, n_x_tiles) + lax.broadcasted_iota(jnp.int32, (tm, dk), 0)
    real = (pos >= 0) & ((i < n_x_tiles) | (pos < N_META))

    def log_gates(hd):
        xh = xg[:, hd * dk:(hd + 1) * dk]
        ls = jnp.minimum(xh, 0.0) - jnp.log(1.0 + jnp.exp(-jnp.abs(xh)))
        la_ref[:, hd * dk:(hd + 1) * dk] = jnp.where(real, ls * (1.0 / GLA_TAU), 0.0)

    step = 512
    for n, lo in enumerate(range(0, n_main, step)):
        y = _dot(hn, w_ref[:, lo:lo + step])
        if lo < hk:
            y = y * dk ** -0.5
        elif lo >= n_main - (n_main - 2 * hk) // 2:
            y = _silu(y)
        o_ref[:, lo:lo + step] = y.astype(BF16)
        if n < GLA_HEADS:
            log_gates(n)
    assert n_main // step >= GLA_HEADS


def _gla_proj(h, norm_w, w_in, w_gate, b_gate, *, tm, seq, casts=()):
    rows, d = h.shape
    n = w_in.shape[1]
    n_main = n - GLA_RANK
    hk = w_gate.shape[1]
    (p, la), cast = _call(
        functools.partial(_gla_proj_kernel, tm=tm, seq=seq, n_x_tiles=rows // tm - 1, n_main=n_main),
        name="gla_proj", grid=(rows // tm,),
        in_specs=[pl.BlockSpec((tm, d), lambda i: (i, 0)), _const_spec((1, d)), _const_spec((d, n)),
                  _const_spec((GLA_RANK, hk)), _const_spec((1, hk))],
        args=[h, norm_w.reshape(1, d), w_in, w_gate, b_gate.reshape(1, hk)],
        out_specs=[pl.BlockSpec((tm, n_main), lambda i: (i, 0)), pl.BlockSpec((tm, hk), lambda i: (i, 0))],
        out_shapes=[jax.ShapeDtypeStruct((rows, n_main), BF16), jax.ShapeDtypeStruct((rows, hk), F32)],
        casts=casts)
    return p, la, cast


def _gla_core_kernel(p_ref, la_ref, h_ref, hn_ref, wo_ref, o_ref,
                     st_ref, stlead_ref, ball_ref, obuf_ref, *, dk, dv):
    bi, c = pl.program_id(0), pl.program_id(1)

    @pl.when((bi == 0) & (c == 0))
    def _():
        st_ref[...] = jnp.zeros_like(st_ref)

    @pl.when((bi > 0) & (c == 0))
    def _():
        st_ref[...] = stlead_ref[...]

    @pl.when((bi == 0) | (c > 0))
    def _():
        C = GLA_CHUNK
        for n in range(p_ref.shape[0] // C):
            rows = pl.ds(n * C, C)
            _gla_chunk(p_ref.at[rows, :], la_ref.at[rows, :], hn_ref, st_ref, ball_ref.at[n],
                       obuf_ref.at[rows, :], dk=dk, dv=dv)
        o_ref[...] = h_ref[...] + _dot(obuf_ref[...], wo_ref[...])

    @pl.when((bi == 0) & (c == 0))
    def _():
        stlead_ref[...] = st_ref[...]


def _left_block_end_rows(b_ref, s):
    C, dk = b_ref.shape
    sub = 8
    bcast = lambda e, n: jnp.broadcast_to(b_ref[e:e + 1, :], (n, dk))
    if 2 * s >= sub:
        n = max(2 * s, sub)
        return jnp.concatenate([bcast(e, n) for e in range(s - 1, C, n)], axis=0)
    r = lax.broadcasted_iota(jnp.int32, (sub, dk), 0)
    tiles = []
    for t0 in range(0, C, sub):
        tile = bcast(t0 + s - 1, sub)
        for blk in range(2 * s, sub, 2 * s):
            tile = jnp.where(r >= blk, bcast(t0 + blk + s - 1, sub), tile)
        tiles.append(tile)
    return jnp.concatenate(tiles, axis=0)


def _gla_chunk(p_ref, la_ref, hn_ref, st_ref, ball_ref, obuf_ref, *, dk, dv):
    C = p_ref.shape[0]
    nh = GLA_HEADS
    k0, v0, g0 = nh * dk, 2 * nh * dk, 2 * nh * dk + nh * dv

    rowl = lax.broadcasted_iota(jnp.int32, (C, dk), 0)
    row = lax.broadcasted_iota(jnp.int32, (C, C), 0)
    col = lax.broadcasted_iota(jnp.int32, (C, C), 1)

    differ = jnp.bitwise_xor(row, col)
    level = jnp.full((C, C), -1, jnp.int32)
    for lv in range(C.bit_length() - 1):
        level = jnp.where((row > col) & (differ >= (1 << lv)), lv, level)

    a_all = la_ref[...]
    a1 = a_all.astype(BF16)
    r1 = a_all - a1.astype(F32)
    a2 = r1.astype(BF16)
    a3 = (r1 - a2.astype(F32)).astype(BF16)
    tri = (row >= col).astype(BF16)
    ball_ref[...] = _dot(tri, a1) + _dot(tri, a2) + _dot(tri, a3)

    for hd in range(nh):
        qb = p_ref[:, hd * dk:(hd + 1) * dk]
        kb = p_ref[:, k0 + hd * dk:k0 + (hd + 1) * dk]
        q = qb.astype(F32)
        k = kb.astype(F32)
        v = p_ref[:, v0 + hd * dv:v0 + (hd + 1) * dv]
        sg = p_ref[:, g0 + hd * dv:g0 + (hd + 1) * dv]
        b_ref = ball_ref.at[:, hd * dk:(hd + 1) * dk]
        b = b_ref[...]

        scores = jnp.where(col == row, _dot_nt(qb, kb), 0.0)
        for lv in range(C.bit_length() - 1):
            s = 1 << lv
            right = (rowl & s) != 0
            if s == 1:
                w = jnp.where(right, jnp.exp(la_ref[:, hd * dk:(hd + 1) * dk]), 1.0)
            else:
                w = jnp.exp(-jnp.abs(b - _left_block_end_rows(b_ref, s)))
            z = (jnp.where(right, q, k) * w).astype(BF16)
            scores = jnp.where(level == lv, _dot_nt(z, z), scores)

        btot = b_ref[C - 1:C, :]
        st = st_ref[hd]
        o = _dot(scores.astype(BF16), v) + _dot_nt((q * jnp.exp(b)).astype(BF16), st.astype(BF16))
        kt = (k * jnp.exp(btot - b)).astype(BF16)
        st_ref[hd] = st * jnp.exp(btot) + _dot_tn(v, kt)
        on = _rms(o, hn_ref[hd:hd + 1, :]) * sg.astype(F32)
        obuf_ref[:, hd * dv:(hd + 1) * dv] = on.astype(BF16)


def _gla_core(p, la, h, head_norm, w_out, *, batch, seq, casts=()):
    rows, d = h.shape
    n = p.shape[1]
    nh, dv = head_norm.shape
    hk = la.shape[1]
    dk = hk // nh
    per_step = GLA_CHUNKS_PER_STEP
    C = per_step * GLA_CHUNK
    grid, (p_spec, la_spec, h_spec), o_spec = _recurrence_specs(C, (n, hk, d), d, batch=batch, seq=seq)
    (out,), cast = _call(
        functools.partial(_gla_core_kernel, dk=dk, dv=dv), name="gla_core", grid=grid,
        in_specs=[p_spec, la_spec, h_spec, _const_spec((nh, dv)), _const_spec((nh * dv, d))],
        args=[p, la, h, head_norm, w_out],
        out_specs=[o_spec], out_shapes=[jax.ShapeDtypeStruct((rows, d), F32)],
        aliases={2: 0},
        scratch_shapes=[
            pltpu.VMEM((nh, dv, dk), F32),
            pltpu.VMEM((nh, dv, dk), F32),
            pltpu.VMEM((per_step, GLA_CHUNK, hk), F32),
            pltpu.VMEM((C, nh * dv), BF16),
        ], casts=casts)
    return out, cast


def kernel(x, meta_tokens, norm_ffn1, ffn1_w_in, ffn1_w_out, norm_mix, norm_ffn2, ffn2_w_in, ffn2_w_out, ret_w_in, ret_head_norm, ret_w_out, gla_w_in, gla_w_gate, gla_b_gate, gla_head_norm, gla_w_out, final_norm):
    batch, seq, d = x.shape
    depth = norm_ffn1.shape[0]
    tm = ROW_TILE
    assert meta_tokens.shape == (N_META, d) and seq % tm == 0 and tm >= LEAD

    tail = jnp.zeros((tm, d), x.dtype).at[LEAD_ZERO:LEAD].set(meta_tokens.astype(x.dtype))
    h = x.reshape(batch * seq, d)

    half = ret_w_in.shape[2] // 6 // RET_HEADS // 2
    assert half == LANES
    inv = 1.0 / (ROPE_BASE ** jnp.linspace(0.0, 1.0, half, dtype=F32))

    stages = []
    for i in range(depth):
        j = i // 2
        stages.append([(ffn1_w_in, i), (ffn1_w_out, i)])
        if i % 2 == 0:
            stages.append([(ret_w_in, j), (ret_w_out, j)])
        else:
            stages += [[(gla_w_in, j), (gla_w_gate, j)], [(gla_w_out, j)]]
        stages.append([(ffn2_w_in, i), (ffn2_w_out, i)])
    stages.append([])
    nxt = iter(stages[1:])

    w = _cast_weights(stages[0])
    for i in range(depth):
        j = i // 2
        h, w = _ffn(h, norm_ffn1[i], *w, tm=tm, tail=tail if i == 0 else None, casts=next(nxt))
        if i % 2 == 0:
            h, w = _ret_mixer(h, norm_mix[i], *w, ret_head_norm[j], inv, batch=batch, seq=seq,
                              casts=next(nxt))
        else:
            p, la, w = _gla_proj(h, norm_mix[i], *w, gla_b_gate[j], tm=tm, seq=seq, casts=next(nxt))
            h, w = _gla_core(p, la, h, gla_head_norm[j], *w, batch=batch, seq=seq, casts=next(nxt))
        h, w = _ffn(h, norm_ffn2[i], *w, tm=tm, final_w=final_norm if i == depth - 1 else None,
                    casts=next(nxt))
    return h.reshape(batch, seq, d)
```

```python
import functools
import math

import jax
import jax.numpy as jnp
from jax import lax
from jax.experimental import pallas as pl
from jax.experimental.pallas import tpu as pltpu

F32 = jnp.float32
BF16 = jnp.bfloat16

EPS = 1e-6
N_META = 16
LEAD = 256
LEAD_ZERO = LEAD - N_META
ROW_TILE = 512
ROPE_BASE = 10000.0
RET_HEADS = 4
GLA_HEADS = 4
GLA_RANK = 16
GLA_TAU = 16.0
RET_CHUNK = 256
GLA_CHUNK = 128
GLA_CHUNKS_PER_STEP = 4
LANES = 128
BF16_SUBLANES = 16
VMEM_LIMIT = 56 * 1024 * 1024

RET_LOG_GAMMA = tuple(math.log1p(-2.0 ** (-5.0 - h)) for h in range(RET_HEADS))


def _const_spec(shape):
    nd = len(shape)
    return pl.BlockSpec(shape, lambda *_: (0,) * nd, pipeline_mode=pl.Buffered(1))


def _rms(x, w):
    ms = jnp.mean(x * x, axis=-1, keepdims=True)
    return x * lax.rsqrt(ms + EPS) * w


def _silu(x):
    return x * jax.nn.sigmoid(x)


def _dot(a, b):
    return jnp.dot(a, b, preferred_element_type=F32)


def _dot_nt(a, b):
    return lax.dot_general(a, b, (((1,), (1,)), ((), ())), preferred_element_type=F32)


def _dot_tn(a, b):
    return lax.dot_general(a, b, (((0,), (0,)), ((), ())), preferred_element_type=F32)


def _cast_blocks(rows, n_steps):
    units = rows // BF16_SUBLANES
    assert rows % BF16_SUBLANES == 0
    return max(n for n in range(1, min(units, n_steps) + 1) if units % n == 0)


def _call(body, *, name, grid, in_specs, args, out_specs, out_shapes, scratch_shapes=(), aliases=None,
          casts=()):
    n_in, n_out, n_cast = len(args), len(out_shapes), len(casts)
    n_steps = math.prod(grid)
    linear = (lambda i: i) if len(grid) == 1 else (lambda b, c: b * grid[1] + c)
    in_specs, out_specs, out_shapes, args = list(in_specs), list(out_specs), list(out_shapes), list(args)
    for stack, layer in casts:
        _, rows, cols = stack.shape
        nb = _cast_blocks(rows, n_steps)
        rb = rows // nb
        blk = lambda *g, nb=nb: jnp.minimum(linear(*g), nb - 1)
        in_specs.append(pl.BlockSpec((None, rb, cols), lambda *g, blk=blk, layer=layer: (layer, blk(*g), 0)))
        out_specs.append(pl.BlockSpec((rb, cols), lambda *g, blk=blk: (blk(*g), 0)))
        out_shapes.append(jax.ShapeDtypeStruct((rows, cols), BF16))
        args.append(stack)

    def kern(*refs):
        ins, rest = refs[:n_in], refs[n_in:]
        cast_in, rest = rest[:n_cast], rest[n_cast:]
        outs, rest = rest[:n_out], rest[n_out:]
        cast_out, scratch = rest[:n_cast], rest[n_cast:]
        for src, dst in zip(cast_in, cast_out):
            dst[...] = src[...].astype(BF16)
        body(*ins, *outs, *scratch)

    res = pl.pallas_call(
        kern, grid=grid, in_specs=in_specs, out_specs=out_specs, out_shape=out_shapes,
        scratch_shapes=list(scratch_shapes), input_output_aliases=aliases or {},
        compiler_params=pltpu.CompilerParams(
            dimension_semantics=("arbitrary",) * len(grid), vmem_limit_bytes=VMEM_LIMIT),
        name=name,
    )(*args)
    return res[:n_out], res[n_out:]


def _cast_weights(casts, *, n_steps=8):
    _, out = _call(lambda: None, name="cast_weights", grid=(n_steps,), in_specs=[], args=[],
                   out_specs=[], out_shapes=[], casts=casts)
    return out


def _ffn_kernel(x_ref, *rest, n_x_tiles, first, final):
    rest = list(rest)
    tail_ref = rest.pop(0) if first else None
    nw_ref, wg_ref, wu_ref, wo_ref = rest[:4]
    fw_ref = rest[4] if final else None
    o_ref = rest[-1]
    i = pl.program_id(0)

    def ffn(x):
        xn = _rms(x, nw_ref[...]).astype(BF16)
        g = _dot(xn, wg_ref[...])
        u = _dot(xn, wu_ref[...])
        hid = (_silu(g) * u).astype(BF16)
        y = x + 0.5 * _dot(hid, wo_ref[...])
        return _rms(y, fw_ref[...]) if final else y

    @pl.when(i < n_x_tiles)
    def _():
        o_ref[...] = ffn(x_ref[...])

    if not final:
        @pl.when(i == n_x_tiles)
        def _():
            src = tail_ref if first else x_ref
            o_ref[...] = jnp.zeros_like(o_ref)
            o_ref[LEAD_ZERO:LEAD, :] = ffn(src[LEAD_ZERO:LEAD, :])


def _ffn(h, norm_w, w_in, w_out, *, tm, tail=None, final_w=None, casts=()):
    d = h.shape[1]
    dff = w_out.shape[0]
    first, final = tail is not None, final_w is not None
    n_x_tiles = (h.shape[0] if first else h.shape[0] - tm) // tm
    n_tiles = n_x_tiles if final else n_x_tiles + 1
    in_specs = [pl.BlockSpec((tm, d), lambda i: (jnp.minimum(i, n_x_tiles - 1), 0) if first else (i, 0))]
    args = [h]
    if first:
        in_specs.append(_const_spec((tm, d)))
        args.append(tail)
    in_specs += [
        _const_spec((1, d)),
        pl.BlockSpec((d, dff), lambda i: (0, 0), pipeline_mode=pl.Buffered(1)),
        pl.BlockSpec((d, dff), lambda i: (0, 1), pipeline_mode=pl.Buffered(1)),
        _const_spec((dff, d)),
    ]
    args += [norm_w.reshape(1, d), w_in, w_in, w_out]
    if final:
        in_specs.append(_const_spec((1, d)))
        args.append(final_w.reshape(1, d))
    (out,), cast = _call(
        functools.partial(_ffn_kernel, n_x_tiles=n_x_tiles, first=first, final=final),
        name="ffn_first" if first else "ffn_final" if final else "ffn",
        grid=(n_tiles,), in_specs=in_specs, args=args,
        out_specs=[pl.BlockSpec((tm, d), lambda i: (i, 0))],
        out_shapes=[jax.ShapeDtypeStruct((n_tiles * tm, d), F32)], casts=casts)
    return out, cast


def _chunk_block(b, c, *, chunks_per_batch, lead_block):
    first = jnp.where(b == 0, lead_block, b * chunks_per_batch)
    return jnp.where(c == 0, first, b * chunks_per_batch + c - 1)


def _recurrence_specs(C, widths, d, *, batch, seq):
    cpb = seq // C
    assert seq % C == 0 and LEAD_ZERO // C == (LEAD - 1) // C
    lead_block = batch * cpb + LEAD_ZERO // C
    idx = lambda b, c: (_chunk_block(b, c, chunks_per_batch=cpb, lead_block=lead_block), 0)
    return (batch, cpb + 1), [pl.BlockSpec((C, w), idx) for w in widths], pl.BlockSpec((C, d), idx)


def _ret_mixer_kernel(h_ref, nw_ref, w_ref, inv_ref, hn_ref, wo_ref, o_ref,
                      s_ref, slead_ref, dec_ref, dq_ref, dk_ref, cosr_ref, sinr_ref, obuf_ref, *, dk, dv):
    b, c = pl.program_id(0), pl.program_id(1)
    C = h_ref.shape[0]
    nh = RET_HEADS
    half = dk // 2
    k0, v0, g0 = nh * dk, 2 * nh * dk, 2 * nh * dk + nh * dv

    @pl.when((b == 0) & (c == 0))
    def _():
        s_ref[...] = jnp.zeros_like(s_ref)
        row = lax.broadcasted_iota(jnp.int32, (C, C), 0)
        col = lax.broadcasted_iota(jnp.int32, (C, C), 1)
        rel = (row - col).astype(F32)
        rowl = lax.broadcasted_iota(jnp.int32, (C, half), 0).astype(F32)
        ang = rowl * inv_ref[...]
        cosr_ref[...] = jnp.cos(ang)
        sinr_ref[...] = jnp.sin(ang)
        for hd in range(nh):
            lg = RET_LOG_GAMMA[hd]
            dec_ref[hd] = jnp.where(rel >= 0, jnp.exp(lg * jnp.maximum(rel, 0.0)), 0.0)
            dq_ref[hd] = jnp.exp(lg * (rowl + 1.0))
            dk_ref[hd] = jnp.exp(lg * (C - 1.0 - rowl))

    def chunk(rows, n, first_pos):
        ang0 = jnp.asarray(first_pos, F32) * inv_ref[...]
        cb, sb = jnp.cos(ang0), jnp.sin(ang0)
        cos = cb * cosr_ref[:n] - sb * sinr_ref[:n]
        sin = sb * cosr_ref[:n] + cb * sinr_ref[:n]

        def rotary(y):
            t1, t2 = y[:, :half], y[:, half:]
            return jnp.concatenate([t1 * cos - t2 * sin, t1 * sin + t2 * cos], axis=1)

        x = h_ref[rows, :]
        xn = _rms(x, nw_ref[...]).astype(BF16)
        for hd in range(nh):
            q = rotary(_dot(xn, w_ref[:, hd * dk:(hd + 1) * dk]))
            k = rotary(_dot(xn, w_ref[:, k0 + hd * dk:k0 + (hd + 1) * dk])) * dk ** -0.5
            v = _dot(xn, w_ref[:, v0 + hd * dv:v0 + (hd + 1) * dv]).astype(BF16)
            g = _dot(xn, w_ref[:, g0 + hd * dv:g0 + (hd + 1) * dv])
            qd = (q * jnp.concatenate([dq_ref[hd, :n]] * 2, axis=1)).astype(BF16)
            kd = (k * jnp.concatenate([dk_ref[hd, C - n:]] * 2, axis=1)).astype(BF16)
            s = (_dot_nt(q.astype(BF16), k.astype(BF16)) * dec_ref[hd, :n, :n]).astype(BF16)
            st = s_ref[hd]
            o = _dot(s, v) + _dot(qd, st.astype(BF16))
            s_ref[hd] = st * math.exp(RET_LOG_GAMMA[hd] * n) + _dot_tn(kd, v)
            on = _rms(o, hn_ref[hd:hd + 1, :]) * _silu(g)
            obuf_ref[:n, hd * dv:(hd + 1) * dv] = on.astype(BF16)
        o_ref[rows, :] = x + _dot(obuf_ref[:n], wo_ref[...])

    @pl.when((b == 0) & (c == 0))
    def _():
        o_ref[...] = jnp.zeros_like(o_ref)
        lead = slice(LEAD_ZERO % C, LEAD_ZERO % C + N_META)
        chunk(lead, N_META, 0)
        slead_ref[...] = s_ref[...]

    @pl.when((b > 0) & (c == 0))
    def _():
        s_ref[...] = slead_ref[...]

    @pl.when(c > 0)
    def _():
        chunk(slice(None), C, (c - 1) * C + N_META)


def _ret_mixer(h, norm_w, w_in, w_out, head_norm, inv, *, batch, seq, casts=()):
    rows, d = h.shape
    n = w_in.shape[1]
    nh, dv = head_norm.shape
    dk = (n - 2 * nh * dv) // (2 * nh)
    assert dk == 2 * LANES
    C = RET_CHUNK
    grid, (h_spec,), o_spec = _recurrence_specs(C, (d,), d, batch=batch, seq=seq)
    (out,), cast = _call(
        functools.partial(_ret_mixer_kernel, dk=dk, dv=dv), name="ret_mixer", grid=grid,
        in_specs=[h_spec, _const_spec((1, d)), _const_spec((d, n)), _const_spec((1, dk // 2)),
                  _const_spec((nh, dv)), _const_spec((nh * dv, d))],
        args=[h, norm_w.reshape(1, d), w_in, inv.reshape(1, dk // 2), head_norm, w_out],
        out_specs=[o_spec], out_shapes=[jax.ShapeDtypeStruct((rows, d), F32)],
        aliases={0: 0},
        scratch_shapes=[
            pltpu.VMEM((nh, dk, dv), F32),
            pltpu.VMEM((nh, dk, dv), F32),
            pltpu.VMEM((nh, C, C), F32),
            pltpu.VMEM((nh, C, dk // 2), F32),
            pltpu.VMEM((nh, C, dk // 2), F32),
            pltpu.VMEM((C, dk // 2), F32),
            pltpu.VMEM((C, dk // 2), F32),
            pltpu.VMEM((C, nh * dv), BF16),
        ], casts=casts)
    return out, cast


def _gla_proj_kernel(h_ref, nw_ref, w_ref, wg_ref, bg_ref, o_ref, la_ref, *, n_x_tiles, n_main):
    i = pl.program_id(0)
    hk = wg_ref.shape[1]
    dk = hk // GLA_HEADS
    step = 512
    assert n_main // step >= GLA_HEADS

    def project(rows):
        hn = _rms(h_ref[rows, :], nw_ref[...]).astype(BF16)
        z = _dot(hn, w_ref[:, n_main:n_main + GLA_RANK])
        xg = _dot(z.astype(BF16), wg_ref[...]) + bg_ref[...]
        for n, lo in enumerate(range(0, n_main, step)):
            y = _dot(hn, w_ref[:, lo:lo + step])
            if lo < hk:
                y = y * dk ** -0.5
            elif lo >= n_main - (n_main - 2 * hk) // 2:
                y = _silu(y)
            o_ref[rows, lo:lo + step] = y.astype(BF16)
            if n < GLA_HEADS:
                xh = xg[:, n * dk:(n + 1) * dk]
                ls = jnp.minimum(xh, 0.0) - jnp.log(1.0 + jnp.exp(-jnp.abs(xh)))
                la_ref[rows, n * dk:(n + 1) * dk] = ls * (1.0 / GLA_TAU)

    @pl.when(i < n_x_tiles)
    def _():
        project(slice(None))

    @pl.when(i == n_x_tiles)
    def _():
        o_ref[...] = jnp.zeros_like(o_ref)
        la_ref[...] = jnp.zeros_like(la_ref)
        project(slice(LEAD_ZERO, LEAD))


def _gla_proj(h, norm_w, w_in, w_gate, b_gate, *, tm, casts=()):
    rows, d = h.shape
    n = w_in.shape[1]
    n_main = n - GLA_RANK
    hk = w_gate.shape[1]
    (p, la), cast = _call(
        functools.partial(_gla_proj_kernel, n_x_tiles=rows // tm - 1, n_main=n_main),
        name="gla_proj", grid=(rows // tm,),
        in_specs=[pl.BlockSpec((tm, d), lambda i: (i, 0)), _const_spec((1, d)), _const_spec((d, n)),
                  _const_spec((GLA_RANK, hk)), _const_spec((1, hk))],
        args=[h, norm_w.reshape(1, d), w_in, w_gate, b_gate.reshape(1, hk)],
        out_specs=[pl.BlockSpec((tm, n_main), lambda i: (i, 0)), pl.BlockSpec((tm, hk), lambda i: (i, 0))],
        out_shapes=[jax.ShapeDtypeStruct((rows, n_main), BF16), jax.ShapeDtypeStruct((rows, hk), F32)],
        casts=casts)
    return p, la, cast


def _gla_core_kernel(p_ref, la_ref, h_ref, hn_ref, wo_ref, o_ref,
                     st_ref, stlead_ref, ball_ref, obuf_ref, *, dk, dv):
    bi, c = pl.program_id(0), pl.program_id(1)

    @pl.when((bi == 0) & (c == 0))
    def _():
        st_ref[...] = jnp.zeros_like(st_ref)

    def chunk(start, n, slot):
        rows = pl.ds(start, n)
        _gla_chunk(p_ref.at[rows, :], la_ref.at[rows, :], hn_ref, st_ref, ball_ref.at[slot, pl.ds(0, n), :],
                   obuf_ref.at[rows, :], dk=dk, dv=dv)

    @pl.when((bi == 0) & (c == 0))
    def _():
        o_ref[...] = jnp.zeros_like(o_ref)
        lead = LEAD_ZERO % p_ref.shape[0]
        chunk(lead, N_META, 0)
        o_ref[lead:lead + N_META, :] = (h_ref[lead:lead + N_META, :]
                                        + _dot(obuf_ref[lead:lead + N_META, :], wo_ref[...]))
        stlead_ref[...] = st_ref[...]

    @pl.when((bi > 0) & (c == 0))
    def _():
        st_ref[...] = stlead_ref[...]

    @pl.when(c > 0)
    def _():
        for n in range(p_ref.shape[0] // GLA_CHUNK):
            chunk(n * GLA_CHUNK, GLA_CHUNK, n)
        o_ref[...] = h_ref[...] + _dot(obuf_ref[...], wo_ref[...])


def _left_block_end_rows(b_ref, s):
    C, dk = b_ref.shape
    sub = 8
    bcast = lambda e, n: jnp.broadcast_to(b_ref[e:e + 1, :], (n, dk))
    if 2 * s >= sub:
        n = max(2 * s, sub)
        return jnp.concatenate([bcast(e, n) for e in range(s - 1, C, n)], axis=0)
    r = lax.broadcasted_iota(jnp.int32, (sub, dk), 0)
    tiles = []
    for t0 in range(0, C, sub):
        tile = bcast(t0 + s - 1, sub)
        for blk in range(2 * s, sub, 2 * s):
            tile = jnp.where(r >= blk, bcast(t0 + blk + s - 1, sub), tile)
        tiles.append(tile)
    return jnp.concatenate(tiles, axis=0)


def _gla_chunk(p_ref, la_ref, hn_ref, st_ref, ball_ref, obuf_ref, *, dk, dv):
    C = p_ref.shape[0]
    nh = GLA_HEADS
    k0, v0, g0 = nh * dk, 2 * nh * dk, 2 * nh * dk + nh * dv

    rowl = lax.broadcasted_iota(jnp.int32, (C, dk), 0)
    row = lax.broadcasted_iota(jnp.int32, (C, C), 0)
    col = lax.broadcasted_iota(jnp.int32, (C, C), 1)

    differ = jnp.bitwise_xor(row, col)
    level = jnp.full((C, C), -1, jnp.int32)
    for lv in range(C.bit_length() - 1):
        level = jnp.where((row > col) & (differ >= (1 << lv)), lv, level)

    a_all = la_ref[...]
    a1 = a_all.astype(BF16)
    r1 = a_all - a1.astype(F32)
    a2 = r1.astype(BF16)
    a3 = (r1 - a2.astype(F32)).astype(BF16)
    tri = (row >= col).astype(BF16)
    ball_ref[...] = _dot(tri, a1) + _dot(tri, a2) + _dot(tri, a3)

    for hd in range(nh):
        qb = p_ref[:, hd * dk:(hd + 1) * dk]
        kb = p_ref[:, k0 + hd * dk:k0 + (hd + 1) * dk]
        q = qb.astype(F32)
        k = kb.astype(F32)
        v = p_ref[:, v0 + hd * dv:v0 + (hd + 1) * dv]
        sg = p_ref[:, g0 + hd * dv:g0 + (hd + 1) * dv]
        b_ref = ball_ref.at[:, hd * dk:(hd + 1) * dk]
        b = b_ref[...]

        scores = jnp.where(col == row, _dot_nt(qb, kb), 0.0)
        for lv in range(C.bit_length() - 1):
            s = 1 << lv
            right = (rowl & s) != 0
            if s == 1:
                w = jnp.where(right, jnp.exp(la_ref[:, hd * dk:(hd + 1) * dk]), 1.0)
            else:
                w = jnp.exp(-jnp.abs(b - _left_block_end_rows(b_ref, s)))
            z = (jnp.where(right, q, k) * w).astype(BF16)
            scores = jnp.where(level == lv, _dot_nt(z, z), scores)

        btot = b_ref[C - 1:C, :]
        st = st_ref[hd]
        o = _dot(scores.astype(BF16), v) + _dot_nt((q * jnp.exp(b)).astype(BF16), st.astype(BF16))
        kt = (k * jnp.exp(btot - b)).astype(BF16)
        st_ref[hd] = st * jnp.exp(btot) + _dot_tn(v, kt)
        on = _rms(o, hn_ref[hd:hd + 1, :]) * sg.astype(F32)
        obuf_ref[:, hd * dv:(hd + 1) * dv] = on.astype(BF16)


def _gla_core(p, la, h, head_norm, w_out, *, batch, seq, casts=()):
    rows, d = h.shape
    n = p.shape[1]
    nh, dv = head_norm.shape
    hk = la.shape[1]
    dk = hk // nh
    per_step = GLA_CHUNKS_PER_STEP
    C = per_step * GLA_CHUNK
    grid, (p_spec, la_spec, h_spec), o_spec = _recurrence_specs(C, (n, hk, d), d, batch=batch, seq=seq)
    (out,), cast = _call(
        functools.partial(_gla_core_kernel, dk=dk, dv=dv), name="gla_core", grid=grid,
        in_specs=[p_spec, la_spec, h_spec, _const_spec((nh, dv)), _const_spec((nh * dv, d))],
        args=[p, la, h, head_norm, w_out],
        out_specs=[o_spec], out_shapes=[jax.ShapeDtypeStruct((rows, d), F32)],
        aliases={2: 0},
        scratch_shapes=[
            pltpu.VMEM((nh, dv, dk), F32),
            pltpu.VMEM((nh, dv, dk), F32),
            pltpu.VMEM((per_step, GLA_CHUNK, hk), F32),
            pltpu.VMEM((C, nh * dv), BF16),
        ], casts=casts)
    return out, cast


def kernel(x, meta_tokens, norm_ffn1, ffn1_w_in, ffn1_w_out, norm_mix, norm_ffn2, ffn2_w_in, ffn2_w_out, ret_w_in, ret_head_norm, ret_w_out, gla_w_in, gla_w_gate, gla_b_gate, gla_head_norm, gla_w_out, final_norm):
    batch, seq, d = x.shape
    depth = norm_ffn1.shape[0]
    tm = ROW_TILE
    assert meta_tokens.shape == (N_META, d) and seq % tm == 0 and tm >= LEAD

    tail = jnp.zeros((tm, d), x.dtype).at[LEAD_ZERO:LEAD].set(meta_tokens.astype(x.dtype))
    h = x.reshape(batch * seq, d)

    half = ret_w_in.shape[2] // 6 // RET_HEADS // 2
    assert half == LANES
    inv = 1.0 / (ROPE_BASE ** jnp.linspace(0.0, 1.0, half, dtype=F32))

    stages = []
    for i in range(depth):
        j = i // 2
        stages.append([(ffn1_w_in, i), (ffn1_w_out, i)])
        if i % 2 == 0:
            stages.append([(ret_w_in, j), (ret_w_out, j)])
        else:
            stages += [[(gla_w_in, j), (gla_w_gate, j)], [(gla_w_out, j)]]
        stages.append([(ffn2_w_in, i), (ffn2_w_out, i)])
    stages.append([])
    nxt = iter(stages[1:])

    w = _cast_weights(stages[0])
    for i in range(depth):
        j = i // 2
        h, w = _ffn(h, norm_ffn1[i], *w, tm=tm, tail=tail if i == 0 else None, casts=next(nxt))
        if i % 2 == 0:
            h, w = _ret_mixer(h, norm_mix[i], *w, ret_head_norm[j], inv, batch=batch, seq=seq,
                              casts=next(nxt))
        else:
            p, la, w = _gla_proj(h, norm_mix[i], *w, gla_b_gate[j], tm=tm, casts=next(nxt))
            h, w = _gla_core(p, la, h, gla_head_norm[j], *w, batch=batch, seq=seq, casts=next(nxt))
        h, w = _ffn(h, norm_ffn2[i], *w, tm=tm, final_w=final_norm if i == depth - 1 else None,
                    casts=next(nxt))
    return h.reshape(batch, seq, d)
```

```python
import functools
import math

import jax
import jax.numpy as jnp
from jax import lax
from jax.experimental import pallas as pl
from jax.experimental.pallas import tpu as pltpu

F32 = jnp.float32
BF16 = jnp.bfloat16

EPS = 1e-6
N_META = 16
LEAD = 256
LEAD_ZERO = LEAD - N_META
ROW_TILE = 1024
ROW_PASS = 512
ROPE_BASE = 10000.0
RET_HEADS = 4
GLA_HEADS = 4
GLA_RANK = 16
GLA_TAU = 16.0
RET_CHUNK = 256
GLA_CHUNK = 128
GLA_CHUNKS_PER_STEP = 4
LANES = 128
BF16_SUBLANES = 16
VMEM_LIMIT = 56 * 1024 * 1024

RET_LOG_GAMMA = tuple(math.log1p(-2.0 ** (-5.0 - h)) for h in range(RET_HEADS))


def _const_spec(shape):
    nd = len(shape)
    return pl.BlockSpec(shape, lambda *_: (0,) * nd, pipeline_mode=pl.Buffered(1))


def _rms(x, w):
    ms = jnp.mean(x * x, axis=-1, keepdims=True)
    return x * lax.rsqrt(ms + EPS) * w


def _silu(x):
    return x * jax.nn.sigmoid(x)


def _dot(a, b):
    return jnp.dot(a, b, preferred_element_type=F32)


def _dot_nt(a, b):
    return lax.dot_general(a, b, (((1,), (1,)), ((), ())), preferred_element_type=F32)


def _dot_tn(a, b):
    return lax.dot_general(a, b, (((0,), (0,)), ((), ())), preferred_element_type=F32)


def _cast_blocks(rows, n_steps):
    units = rows // BF16_SUBLANES
    assert rows % BF16_SUBLANES == 0
    return max(n for n in range(1, min(units, n_steps) + 1) if units % n == 0)


def _call(body, *, name, grid, in_specs, args, out_specs, out_shapes, scratch_shapes=(), aliases=None,
          casts=()):
    n_in, n_out, n_cast = len(args), len(out_shapes), len(casts)
    n_steps = math.prod(grid)
    linear = (lambda i: i) if len(grid) == 1 else (lambda b, c: b * grid[1] + c)
    in_specs, out_specs, out_shapes, args = list(in_specs), list(out_specs), list(out_shapes), list(args)
    for stack, layer in casts:
        _, rows, cols = stack.shape
        nb = _cast_blocks(rows, n_steps)
        rb = rows // nb
        blk = lambda *g, nb=nb: jnp.minimum(linear(*g), nb - 1)
        in_specs.append(pl.BlockSpec((None, rb, cols), lambda *g, blk=blk, layer=layer: (layer, blk(*g), 0)))
        out_specs.append(pl.BlockSpec((rb, cols), lambda *g, blk=blk: (blk(*g), 0)))
        out_shapes.append(jax.ShapeDtypeStruct((rows, cols), BF16))
        args.append(stack)

    def kern(*refs):
        ins, rest = refs[:n_in], refs[n_in:]
        cast_in, rest = rest[:n_cast], rest[n_cast:]
        outs, rest = rest[:n_out], rest[n_out:]
        cast_out, scratch = rest[:n_cast], rest[n_cast:]
        for src, dst in zip(cast_in, cast_out):
            dst[...] = src[...].astype(BF16)
        body(*ins, *outs, *scratch)

    res = pl.pallas_call(
        kern, grid=grid, in_specs=in_specs, out_specs=out_specs, out_shape=out_shapes,
        scratch_shapes=list(scratch_shapes), input_output_aliases=aliases or {},
        compiler_params=pltpu.CompilerParams(
            dimension_semantics=("arbitrary",) * len(grid), vmem_limit_bytes=VMEM_LIMIT),
        name=name,
    )(*args)
    return res[:n_out], res[n_out:]


def _cast_weights(casts, *, n_steps=8):
    _, out = _call(lambda: None, name="cast_weights", grid=(n_steps,), in_specs=[], args=[],
                   out_specs=[], out_shapes=[], casts=casts)
    return out


def _ffn_kernel(x_ref, *rest, n_x_tiles, first, final):
    rest = list(rest)
    tail_ref = rest.pop(0) if first else None
    nw_ref, wg_ref, wu_ref, wo_ref = rest[:4]
    fw_ref = rest[4] if final else None
    o_ref = rest[-1]
    i = pl.program_id(0)

    def ffn(x):
        xn = _rms(x, nw_ref[...]).astype(BF16)
        g = _dot(xn, wg_ref[...])
        u = _dot(xn, wu_ref[...])
        hid = (_silu(g) * u).astype(BF16)
        y = x + 0.5 * _dot(hid, wo_ref[...])
        return _rms(y, fw_ref[...]) if final else y

    @pl.when(i < n_x_tiles)
    def _():
        for lo in range(0, x_ref.shape[0], ROW_PASS):
            o_ref[lo:lo + ROW_PASS, :] = ffn(x_ref[lo:lo + ROW_PASS, :])

    if not final:
        @pl.when(i == n_x_tiles)
        def _():
            src = tail_ref if first else x_ref
            o_ref[...] = jnp.zeros_like(o_ref)
            o_ref[LEAD_ZERO:LEAD, :] = ffn(src[LEAD_ZERO:LEAD, :])


def _ffn(h, norm_w, w_in, w_out, *, tm, tail=None, final_w=None, casts=()):
    d = h.shape[1]
    dff = w_out.shape[0]
    first, final = tail is not None, final_w is not None
    n_x_tiles = (h.shape[0] if first else h.shape[0] - tm) // tm
    n_tiles = n_x_tiles if final else n_x_tiles + 1
    in_specs = [pl.BlockSpec((tm, d), lambda i: (jnp.minimum(i, n_x_tiles - 1), 0) if first else (i, 0))]
    args = [h]
    if first:
        in_specs.append(_const_spec((tm, d)))
        args.append(tail)
    in_specs += [
        _const_spec((1, d)),
        pl.BlockSpec((d, dff), lambda i: (0, 0), pipeline_mode=pl.Buffered(1)),
        pl.BlockSpec((d, dff), lambda i: (0, 1), pipeline_mode=pl.Buffered(1)),
        _const_spec((dff, d)),
    ]
    args += [norm_w.reshape(1, d), w_in, w_in, w_out]
    if final:
        in_specs.append(_const_spec((1, d)))
        args.append(final_w.reshape(1, d))
    (out,), cast = _call(
        functools.partial(_ffn_kernel, n_x_tiles=n_x_tiles, first=first, final=final),
        name="ffn_first" if first else "ffn_final" if final else "ffn",
        grid=(n_tiles,), in_specs=in_specs, args=args,
        out_specs=[pl.BlockSpec((tm, d), lambda i: (i, 0))],
        out_shapes=[jax.ShapeDtypeStruct((n_tiles * tm, d), F32)], casts=casts)
    return out, cast


def _chunk_block(b, c, *, chunks_per_batch, lead_block):
    first = jnp.where(b == 0, lead_block, b * chunks_per_batch)
    return jnp.where(c == 0, first, b * chunks_per_batch + c - 1)


def _recurrence_specs(C, widths, d, *, batch, seq):
    cpb = seq // C
    assert seq % C == 0 and LEAD_ZERO // C == (LEAD - 1) // C
    lead_block = batch * cpb + LEAD_ZERO // C
    idx = lambda b, c: (_chunk_block(b, c, chunks_per_batch=cpb, lead_block=lead_block), 0)
    return (batch, cpb + 1), [pl.BlockSpec((C, w), idx) for w in widths], pl.BlockSpec((C, d), idx)


def _ret_mixer_kernel(h_ref, nw_ref, w_ref, inv_ref, hn_ref, wo_ref, o_ref,
                      s_ref, slead_ref, dec_ref, dq_ref, dk_ref, cosr_ref, sinr_ref, obuf_ref, *, dk, dv):
    b, c = pl.program_id(0), pl.program_id(1)
    C = h_ref.shape[0]
    nh = RET_HEADS
    half = dk // 2
    k0, v0, g0 = nh * dk, 2 * nh * dk, 2 * nh * dk + nh * dv

    @pl.when((b == 0) & (c == 0))
    def _():
        s_ref[...] = jnp.zeros_like(s_ref)
        row = lax.broadcasted_iota(jnp.int32, (C, C), 0)
        col = lax.broadcasted_iota(jnp.int32, (C, C), 1)
        rel = (row - col).astype(F32)
        rowl = lax.broadcasted_iota(jnp.int32, (C, half), 0).astype(F32)
        ang = rowl * inv_ref[...]
        cosr_ref[...] = jnp.cos(ang)
        sinr_ref[...] = jnp.sin(ang)
        for hd in range(nh):
            lg = RET_LOG_GAMMA[hd]
            dec_ref[hd] = jnp.where(rel >= 0, jnp.exp(lg * jnp.maximum(rel, 0.0)), 0.0)
            dq_ref[hd] = jnp.exp(lg * (rowl + 1.0))
            dk_ref[hd] = jnp.exp(lg * (C - 1.0 - rowl))

    def chunk(rows, n, first_pos):
        ang0 = jnp.asarray(first_pos, F32) * inv_ref[...]
        cb, sb = jnp.cos(ang0), jnp.sin(ang0)
        cos = cb * cosr_ref[:n] - sb * sinr_ref[:n]
        sin = sb * cosr_ref[:n] + cb * sinr_ref[:n]

        def rotary(y):
            t1, t2 = y[:, :half], y[:, half:]
            return jnp.concatenate([t1 * cos - t2 * sin, t1 * sin + t2 * cos], axis=1)

        x = h_ref[rows, :]
        xn = _rms(x, nw_ref[...]).astype(BF16)
        for hd in range(nh):
            q = rotary(_dot(xn, w_ref[:, hd * dk:(hd + 1) * dk]))
            k = rotary(_dot(xn, w_ref[:, k0 + hd * dk:k0 + (hd + 1) * dk])) * dk ** -0.5
            v = _dot(xn, w_ref[:, v0 + hd * dv:v0 + (hd + 1) * dv]).astype(BF16)
            g = _dot(xn, w_ref[:, g0 + hd * dv:g0 + (hd + 1) * dv])
            qd = (q * jnp.concatenate([dq_ref[hd, :n]] * 2, axis=1)).astype(BF16)
            kd = (k * jnp.concatenate([dk_ref[hd, C - n:]] * 2, axis=1)).astype(BF16)
            s = (_dot_nt(q.astype(BF16), k.astype(BF16)) * dec_ref[hd, :n, :n]).astype(BF16)
            st = s_ref[hd]
            o = _dot(s, v) + _dot(qd, st.astype(BF16))
            s_ref[hd] = st * math.exp(RET_LOG_GAMMA[hd] * n) + _dot_tn(kd, v)
            on = _rms(o, hn_ref[hd:hd + 1, :]) * _silu(g)
            obuf_ref[:n, hd * dv:(hd + 1) * dv] = on.astype(BF16)
        o_ref[rows, :] = x + _dot(obuf_ref[:n], wo_ref[...])

    @pl.when((b == 0) & (c == 0))
    def _():
        o_ref[...] = jnp.zeros_like(o_ref)
        lead = slice(LEAD_ZERO % C, LEAD_ZERO % C + N_META)
        chunk(lead, N_META, 0)
        slead_ref[...] = s_ref[...]

    @pl.when((b > 0) & (c == 0))
    def _():
        s_ref[...] = slead_ref[...]

    @pl.when(c > 0)
    def _():
        chunk(slice(None), C, (c - 1) * C + N_META)


def _ret_mixer(h, norm_w, w_in, w_out, head_norm, inv, *, batch, seq, casts=()):
    rows, d = h.shape
    n = w_in.shape[1]
    nh, dv = head_norm.shape
    dk = (n - 2 * nh * dv) // (2 * nh)
    assert dk == 2 * LANES
    C = RET_CHUNK
    grid, (h_spec,), o_spec = _recurrence_specs(C, (d,), d, batch=batch, seq=seq)
    (out,), cast = _call(
        functools.partial(_ret_mixer_kernel, dk=dk, dv=dv), name="ret_mixer", grid=grid,
        in_specs=[h_spec, _const_spec((1, d)), _const_spec((d, n)), _const_spec((1, dk // 2)),
                  _const_spec((nh, dv)), _const_spec((nh * dv, d))],
        args=[h, norm_w.reshape(1, d), w_in, inv.reshape(1, dk // 2), head_norm, w_out],
        out_specs=[o_spec], out_shapes=[jax.ShapeDtypeStruct((rows, d), F32)],
        aliases={0: 0},
        scratch_shapes=[
            pltpu.VMEM((nh, dk, dv), F32),
            pltpu.VMEM((nh, dk, dv), F32),
            pltpu.VMEM((nh, C, C), F32),
            pltpu.VMEM((nh, C, dk // 2), F32),
            pltpu.VMEM((nh, C, dk // 2), F32),
            pltpu.VMEM((C, dk // 2), F32),
            pltpu.VMEM((C, dk // 2), F32),
            pltpu.VMEM((C, nh * dv), BF16),
        ], casts=casts)
    return out, cast


def _gla_proj_kernel(h_ref, nw_ref, w_ref, wg_ref, bg_ref, o_ref, la_ref, *, n_x_tiles, n_main):
    i = pl.program_id(0)
    hk = wg_ref.shape[1]
    dk = hk // GLA_HEADS
    step = 512
    assert n_main // step >= GLA_HEADS

    def project(rows):
        hn = _rms(h_ref[rows, :], nw_ref[...]).astype(BF16)
        z = _dot(hn, w_ref[:, n_main:n_main + GLA_RANK])
        xg = _dot(z.astype(BF16), wg_ref[...]) + bg_ref[...]
        for n, lo in enumerate(range(0, n_main, step)):
            y = _dot(hn, w_ref[:, lo:lo + step])
            if lo < hk:
                y = y * dk ** -0.5
            elif lo >= n_main - (n_main - 2 * hk) // 2:
                y = _silu(y)
            o_ref[rows, lo:lo + step] = y.astype(BF16)
            if n < GLA_HEADS:
                xh = xg[:, n * dk:(n + 1) * dk]
                ls = jnp.minimum(xh, 0.0) - jnp.log(1.0 + jnp.exp(-jnp.abs(xh)))
                la_ref[rows, n * dk:(n + 1) * dk] = ls * (1.0 / GLA_TAU)

    @pl.when(i < n_x_tiles)
    def _():
        for lo in range(0, h_ref.shape[0], ROW_PASS):
            project(slice(lo, lo + ROW_PASS))

    @pl.when(i == n_x_tiles)
    def _():
        o_ref[...] = jnp.zeros_like(o_ref)
        la_ref[...] = jnp.zeros_like(la_ref)
        project(slice(LEAD_ZERO, LEAD))


def _gla_proj(h, norm_w, w_in, w_gate, b_gate, *, tm, casts=()):
    rows, d = h.shape
    n = w_in.shape[1]
    n_main = n - GLA_RANK
    hk = w_gate.shape[1]
    (p, la), cast = _call(
        functools.partial(_gla_proj_kernel, n_x_tiles=rows // tm - 1, n_main=n_main),
        name="gla_proj", grid=(rows // tm,),
        in_specs=[pl.BlockSpec((tm, d), lambda i: (i, 0)), _const_spec((1, d)), _const_spec((d, n)),
                  _const_spec((GLA_RANK, hk)), _const_spec((1, hk))],
        args=[h, norm_w.reshape(1, d), w_in, w_gate, b_gate.reshape(1, hk)],
        out_specs=[pl.BlockSpec((tm, n_main), lambda i: (i, 0)), pl.BlockSpec((tm, hk), lambda i: (i, 0))],
        out_shapes=[jax.ShapeDtypeStruct((rows, n_main), BF16), jax.ShapeDtypeStruct((rows, hk), F32)],
        casts=casts)
    return p, la, cast


def _gla_core_kernel(p_ref, la_ref, h_ref, hn_ref, wo_ref, o_ref,
                     st_ref, stlead_ref, ball_ref, obuf_ref, *, dk, dv):
    bi, c = pl.program_id(0), pl.program_id(1)

    @pl.when((bi == 0) & (c == 0))
    def _():
        st_ref[...] = jnp.zeros_like(st_ref)

    def chunk(start, n, slot):
        rows = pl.ds(start, n)
        _gla_chunk(p_ref.at[rows, :], la_ref.at[rows, :], hn_ref, st_ref, ball_ref.at[slot, pl.ds(0, n), :],
                   obuf_ref.at[rows, :], dk=dk, dv=dv)

    @pl.when((bi == 0) & (c == 0))
    def _():
        o_ref[...] = jnp.zeros_like(o_ref)
        lead = LEAD_ZERO % p_ref.shape[0]
        chunk(lead, N_META, 0)
        o_ref[lead:lead + N_META, :] = (h_ref[lead:lead + N_META, :]
                                        + _dot(obuf_ref[lead:lead + N_META, :], wo_ref[...]))
        stlead_ref[...] = st_ref[...]

    @pl.when((bi > 0) & (c == 0))
    def _():
        st_ref[...] = stlead_ref[...]

    @pl.when(c > 0)
    def _():
        for n in range(p_ref.shape[0] // GLA_CHUNK):
            chunk(n * GLA_CHUNK, GLA_CHUNK, n)
        o_ref[...] = h_ref[...] + _dot(obuf_ref[...], wo_ref[...])


def _left_block_end_rows(b_ref, s):
    C, dk = b_ref.shape
    sub = 8
    bcast = lambda e, n: jnp.broadcast_to(b_ref[e:e + 1, :], (n, dk))
    if 2 * s >= sub:
        n = max(2 * s, sub)
        return jnp.concatenate([bcast(e, n) for e in range(s - 1, C, n)], axis=0)
    r = lax.broadcasted_iota(jnp.int32, (sub, dk), 0)
    tiles = []
    for t0 in range(0, C, sub):
        tile = bcast(t0 + s - 1, sub)
        for blk in range(2 * s, sub, 2 * s):
            tile = jnp.where(r >= blk, bcast(t0 + blk + s - 1, sub), tile)
        tiles.append(tile)
    return jnp.concatenate(tiles, axis=0)


def _gla_chunk(p_ref, la_ref, hn_ref, st_ref, ball_ref, obuf_ref, *, dk, dv):
    C = p_ref.shape[0]
    nh = GLA_HEADS
    k0, v0, g0 = nh * dk, 2 * nh * dk, 2 * nh * dk + nh * dv

    rowl = lax.broadcasted_iota(jnp.int32, (C, dk), 0)
    row = lax.broadcasted_iota(jnp.int32, (C, C), 0)
    col = lax.broadcasted_iota(jnp.int32, (C, C), 1)

    differ = jnp.bitwise_xor(row, col)
    level = jnp.full((C, C), -1, jnp.int32)
    for lv in range(C.bit_length() - 1):
        level = jnp.where((row > col) & (differ >= (1 << lv)), lv, level)

    a_all = la_ref[...]
    a1 = a_all.astype(BF16)
    r1 = a_all - a1.astype(F32)
    a2 = r1.astype(BF16)
    a3 = (r1 - a2.astype(F32)).astype(BF16)
    tri = (row >= col).astype(BF16)
    ball_ref[...] = _dot(tri, a1) + _dot(tri, a2) + _dot(tri, a3)

    for hd in range(nh):
        qb = p_ref[:, hd * dk:(hd + 1) * dk]
        kb = p_ref[:, k0 + hd * dk:k0 + (hd + 1) * dk]
        q = qb.astype(F32)
        k = kb.astype(F32)
        v = p_ref[:, v0 + hd * dv:v0 + (hd + 1) * dv]
        sg = p_ref[:, g0 + hd * dv:g0 + (hd + 1) * dv]
        b_ref = ball_ref.at[:, hd * dk:(hd + 1) * dk]
        b = b_ref[...]

        scores = jnp.where(col == row, _dot_nt(qb, kb), 0.0)
        for lv in range(C.bit_length() - 1):
            s = 1 << lv
            right = (rowl & s) != 0
            if s == 1:
                w = jnp.where(right, jnp.exp(la_ref[:, hd * dk:(hd + 1) * dk]), 1.0)
            else:
                w = jnp.exp(-jnp.abs(b - _left_block_end_rows(b_ref, s)))
            z = (jnp.where(right, q, k) * w).astype(BF16)
            scores = jnp.where(level == lv, _dot_nt(z, z), scores)

        btot = b_ref[C - 1:C, :]
        st = st_ref[hd]
        o = _dot(scores.astype(BF16), v) + _dot_nt((q * jnp.exp(b)).astype(BF16), st.astype(BF16))
        kt = (k * jnp.exp(btot - b)).astype(BF16)
        st_ref[hd] = st * jnp.exp(btot) + _dot_tn(v, kt)
        on = _rms(o, hn_ref[hd:hd + 1, :]) * sg.astype(F32)
        obuf_ref[:, hd * dv:(hd + 1) * dv] = on.astype(BF16)


def _gla_core(p, la, h, head_norm, w_out, *, batch, seq, casts=()):
    rows, d = h.shape
    n = p.shape[1]
    nh, dv = head_norm.shape
    hk = la.shape[1]
    dk = hk // nh
    per_step = GLA_CHUNKS_PER_STEP
    C = per_step * GLA_CHUNK
    grid, (p_spec, la_spec, h_spec), o_spec = _recurrence_specs(C, (n, hk, d), d, batch=batch, seq=seq)
    (out,), cast = _call(
        functools.partial(_gla_core_kernel, dk=dk, dv=dv), name="gla_core", grid=grid,
        in_specs=[p_spec, la_spec, h_spec, _const_spec((nh, dv)), _const_spec((nh * dv, d))],
        args=[p, la, h, head_norm, w_out],
        out_specs=[o_spec], out_shapes=[jax.ShapeDtypeStruct((rows, d), F32)],
        aliases={2: 0},
        scratch_shapes=[
            pltpu.VMEM((nh, dv, dk), F32),
            pltpu.VMEM((nh, dv, dk), F32),
            pltpu.VMEM((per_step, GLA_CHUNK, hk), F32),
            pltpu.VMEM((C, nh * dv), BF16),
        ], casts=casts)
    return out, cast


def kernel(x, meta_tokens, norm_ffn1, ffn1_w_in, ffn1_w_out, norm_mix, norm_ffn2, ffn2_w_in, ffn2_w_out, ret_w_in, ret_head_norm, ret_w_out, gla_w_in, gla_w_gate, gla_b_gate, gla_head_norm, gla_w_out, final_norm):
    batch, seq, d = x.shape
    depth = norm_ffn1.shape[0]
    tm = ROW_TILE
    assert meta_tokens.shape == (N_META, d) and seq % tm == 0 and tm >= LEAD and tm % ROW_PASS == 0

    tail = jnp.zeros((tm, d), x.dtype).at[LEAD_ZERO:LEAD].set(meta_tokens.astype(x.dtype))
    h = x.reshape(batch * seq, d)

    half = ret_w_in.shape[2] // 6 // RET_HEADS // 2
    assert half == LANES
    inv = 1.0 / (ROPE_BASE ** jnp.linspace(0.0, 1.0, half, dtype=F32))

    stages = []
    for i in range(depth):
        j = i // 2
        stages.append([(ffn1_w_in, i), (ffn1_w_out, i)])
        if i % 2 == 0:
            stages.append([(ret_w_in, j), (ret_w_out, j)])
        else:
            stages += [[(gla_w_in, j), (gla_w_gate, j)], [(gla_w_out, j)]]
        stages.append([(ffn2_w_in, i), (ffn2_w_out, i)])
    stages.append([])
    nxt = iter(stages[1:])

    w = _cast_weights(stages[0])
    for i in range(depth):
        j = i // 2
        h, w = _ffn(h, norm_ffn1[i], *w, tm=tm, tail=tail if i == 0 else None, casts=next(nxt))
        if i % 2 == 0:
            h, w = _ret_mixer(h, norm_mix[i], *w, ret_head_norm[j], inv, batch=batch, seq=seq,
                              casts=next(nxt))
        else:
            p, la, w = _gla_proj(h, norm_mix[i], *w, gla_b_gate[j], tm=tm, casts=next(nxt))
            h, w = _gla_core(p, la, h, gla_head_norm[j], *w, batch=batch, seq=seq, casts=next(nxt))
        h, w = _ffn(h, norm_ffn2[i], *w, tm=tm, final_w=final_norm if i == depth - 1 else None,
                    casts=next(nxt))
    return h.reshape(batch, seq, d)
```

```python
import functools
import math

import jax
import jax.numpy as jnp
from jax import lax
from jax.experimental import pallas as pl
from jax.experimental.pallas import tpu as pltpu

F32 = jnp.float32
BF16 = jnp.bfloat16

EPS = 1e-6
N_META = 16
LEAD = 256
LEAD_ZERO = LEAD - N_META
ROW_TILE = 1024
ROW_PASS = 512
ROPE_BASE = 10000.0
RET_HEADS = 4
GLA_HEADS = 4
GLA_RANK = 16
GLA_TAU = 16.0
RET_CHUNK = 256
RET_CHUNKS_PER_STEP = 2
GLA_CHUNK = 128
GLA_CHUNKS_PER_STEP = 4
LANES = 128
BF16_SUBLANES = 16
VMEM_LIMIT = 56 * 1024 * 1024

RET_LOG_GAMMA = tuple(math.log1p(-2.0 ** (-5.0 - h)) for h in range(RET_HEADS))


def _const_spec(shape):
    nd = len(shape)
    return pl.BlockSpec(shape, lambda *_: (0,) * nd, pipeline_mode=pl.Buffered(1))


def _rms(x, w):
    ms = jnp.mean(x * x, axis=-1, keepdims=True)
    return x * lax.rsqrt(ms + EPS) * w


def _silu(x):
    return x * jax.nn.sigmoid(x)


def _dot(a, b):
    return jnp.dot(a, b, preferred_element_type=F32)


def _dot_nt(a, b):
    return lax.dot_general(a, b, (((1,), (1,)), ((), ())), preferred_element_type=F32)


def _dot_tn(a, b):
    return lax.dot_general(a, b, (((0,), (0,)), ((), ())), preferred_element_type=F32)


def _cast_blocks(rows, n_steps):
    units = rows // BF16_SUBLANES
    assert rows % BF16_SUBLANES == 0
    return max(n for n in range(1, min(units, n_steps) + 1) if units % n == 0)


def _call(body, *, name, grid, in_specs, args, out_specs, out_shapes, scratch_shapes=(), aliases=None,
          casts=()):
    n_in, n_out, n_cast = len(args), len(out_shapes), len(casts)
    n_steps = math.prod(grid)
    linear = (lambda i: i) if len(grid) == 1 else (lambda b, c: b * grid[1] + c)
    in_specs, out_specs, out_shapes, args = list(in_specs), list(out_specs), list(out_shapes), list(args)
    for stack, layer in casts:
        _, rows, cols = stack.shape
        nb = _cast_blocks(rows, n_steps)
        rb = rows // nb
        blk = lambda *g, nb=nb: jnp.minimum(linear(*g), nb - 1)
        in_specs.append(pl.BlockSpec((None, rb, cols), lambda *g, blk=blk, layer=layer: (layer, blk(*g), 0)))
        out_specs.append(pl.BlockSpec((rb, cols), lambda *g, blk=blk: (blk(*g), 0)))
        out_shapes.append(jax.ShapeDtypeStruct((rows, cols), BF16))
        args.append(stack)

    def kern(*refs):
        ins, rest = refs[:n_in], refs[n_in:]
        cast_in, rest = rest[:n_cast], rest[n_cast:]
        outs, rest = rest[:n_out], rest[n_out:]
        cast_out, scratch = rest[:n_cast], rest[n_cast:]
        for src, dst in zip(cast_in, cast_out):
            dst[...] = src[...].astype(BF16)
        body(*ins, *outs, *scratch)

    res = pl.pallas_call(
        kern, grid=grid, in_specs=in_specs, out_specs=out_specs, out_shape=out_shapes,
        scratch_shapes=list(scratch_shapes), input_output_aliases=aliases or {},
        compiler_params=pltpu.CompilerParams(
            dimension_semantics=("arbitrary",) * len(grid), vmem_limit_bytes=VMEM_LIMIT),
        name=name,
    )(*args)
    return res[:n_out], res[n_out:]


def _cast_weights(casts, *, n_steps=8):
    _, out = _call(lambda: None, name="cast_weights", grid=(n_steps,), in_specs=[], args=[],
                   out_specs=[], out_shapes=[], casts=casts)
    return out


def _ffn_kernel(x_ref, *rest, n_x_tiles, first, final):
    rest = list(rest)
    tail_ref = rest.pop(0) if first else None
    nw_ref, wg_ref, wu_ref, wo_ref = rest[:4]
    fw_ref = rest[4] if final else None
    o_ref = rest[-1]
    i = pl.program_id(0)

    def ffn(x):
        xn = _rms(x, nw_ref[...]).astype(BF16)
        g = _dot(xn, wg_ref[...])
        u = _dot(xn, wu_ref[...])
        hid = (_silu(g) * u).astype(BF16)
        y = x + 0.5 * _dot(hid, wo_ref[...])
        return _rms(y, fw_ref[...]) if final else y

    @pl.when(i < n_x_tiles)
    def _():
        for lo in range(0, x_ref.shape[0], ROW_PASS):
            o_ref[lo:lo + ROW_PASS, :] = ffn(x_ref[lo:lo + ROW_PASS, :])

    if not final:
        @pl.when(i == n_x_tiles)
        def _():
            src = tail_ref if first else x_ref
            o_ref[...] = jnp.zeros_like(o_ref)
            o_ref[LEAD_ZERO:LEAD, :] = ffn(src[LEAD_ZERO:LEAD, :])


def _ffn(h, norm_w, w_in, w_out, *, tm, tail=None, final_w=None, casts=()):
    d = h.shape[1]
    dff = w_out.shape[0]
    first, final = tail is not None, final_w is not None
    n_x_tiles = (h.shape[0] if first else h.shape[0] - tm) // tm
    n_tiles = n_x_tiles if final else n_x_tiles + 1
    in_specs = [pl.BlockSpec((tm, d), lambda i: (jnp.minimum(i, n_x_tiles - 1), 0) if first else (i, 0))]
    args = [h]
    if first:
        in_specs.append(_const_spec((tm, d)))
        args.append(tail)
    in_specs += [
        _const_spec((1, d)),
        pl.BlockSpec((d, dff), lambda i: (0, 0), pipeline_mode=pl.Buffered(1)),
        pl.BlockSpec((d, dff), lambda i: (0, 1), pipeline_mode=pl.Buffered(1)),
        _const_spec((dff, d)),
    ]
    args += [norm_w.reshape(1, d), w_in, w_in, w_out]
    if final:
        in_specs.append(_const_spec((1, d)))
        args.append(final_w.reshape(1, d))
    (out,), cast = _call(
        functools.partial(_ffn_kernel, n_x_tiles=n_x_tiles, first=first, final=final),
        name="ffn_first" if first else "ffn_final" if final else "ffn",
        grid=(n_tiles,), in_specs=in_specs, args=args,
        out_specs=[pl.BlockSpec((tm, d), lambda i: (i, 0))],
        out_shapes=[jax.ShapeDtypeStruct((n_tiles * tm, d), F32)], casts=casts)
    return out, cast


def _chunk_block(b, c, *, chunks_per_batch, lead_block):
    first = jnp.where(b == 0, lead_block, b * chunks_per_batch)
    return jnp.where(c == 0, first, b * chunks_per_batch + c - 1)


def _recurrence_specs(C, widths, d, *, batch, seq):
    cpb = seq // C
    assert seq % C == 0 and LEAD_ZERO // C == (LEAD - 1) // C
    lead_block = batch * cpb + LEAD_ZERO // C
    idx = lambda b, c: (_chunk_block(b, c, chunks_per_batch=cpb, lead_block=lead_block), 0)
    return (batch, cpb + 1), [pl.BlockSpec((C, w), idx) for w in widths], pl.BlockSpec((C, d), idx)


def _ret_mixer_kernel(h_ref, nw_ref, w_ref, inv_ref, hn_ref, wo_ref, o_ref,
                      s_ref, slead_ref, dec_ref, dq_ref, dk_ref, cosr_ref, sinr_ref, obuf_ref, *, dk, dv):
    b, c = pl.program_id(0), pl.program_id(1)
    C = RET_CHUNK
    block_rows = h_ref.shape[0]
    nh = RET_HEADS
    half = dk // 2
    k0, v0, g0 = nh * dk, 2 * nh * dk, 2 * nh * dk + nh * dv

    @pl.when((b == 0) & (c == 0))
    def _():
        s_ref[...] = jnp.zeros_like(s_ref)
        row = lax.broadcasted_iota(jnp.int32, (C, C), 0)
        col = lax.broadcasted_iota(jnp.int32, (C, C), 1)
        rel = (row - col).astype(F32)
        rowl = lax.broadcasted_iota(jnp.int32, (C, half), 0).astype(F32)
        ang = rowl * inv_ref[...]
        cosr_ref[...] = jnp.cos(ang)
        sinr_ref[...] = jnp.sin(ang)
        for hd in range(nh):
            lg = RET_LOG_GAMMA[hd]
            dec_ref[hd] = jnp.where(rel >= 0, jnp.exp(lg * jnp.maximum(rel, 0.0)), 0.0)
            dq_ref[hd] = jnp.exp(lg * (rowl + 1.0))
            dk_ref[hd] = jnp.exp(lg * (C - 1.0 - rowl))

    def chunk(rows, n, first_pos):
        ang0 = jnp.asarray(first_pos, F32) * inv_ref[...]
        cb, sb = jnp.cos(ang0), jnp.sin(ang0)
        cos = cb * cosr_ref[:n] - sb * sinr_ref[:n]
        sin = sb * cosr_ref[:n] + cb * sinr_ref[:n]

        def rotary(y):
            t1, t2 = y[:, :half], y[:, half:]
            return jnp.concatenate([t1 * cos - t2 * sin, t1 * sin + t2 * cos], axis=1)

        x = h_ref[rows, :]
        xn = _rms(x, nw_ref[...]).astype(BF16)
        for hd in range(nh):
            q = rotary(_dot(xn, w_ref[:, hd * dk:(hd + 1) * dk]))
            k = rotary(_dot(xn, w_ref[:, k0 + hd * dk:k0 + (hd + 1) * dk])) * dk ** -0.5
            v = _dot(xn, w_ref[:, v0 + hd * dv:v0 + (hd + 1) * dv]).astype(BF16)
            g = _dot(xn, w_ref[:, g0 + hd * dv:g0 + (hd + 1) * dv])
            qd = (q * jnp.concatenate([dq_ref[hd, :n]] * 2, axis=1)).astype(BF16)
            kd = (k * jnp.concatenate([dk_ref[hd, C - n:]] * 2, axis=1)).astype(BF16)
            s = (_dot_nt(q.astype(BF16), k.astype(BF16)) * dec_ref[hd, :n, :n]).astype(BF16)
            st = s_ref[hd]
            o = _dot(s, v) + _dot(qd, st.astype(BF16))
            s_ref[hd] = st * math.exp(RET_LOG_GAMMA[hd] * n) + _dot_tn(kd, v)
            on = _rms(o, hn_ref[hd:hd + 1, :]) * _silu(g)
            obuf_ref[:n, hd * dv:(hd + 1) * dv] = on.astype(BF16)
        o_ref[rows, :] = x + _dot(obuf_ref[:n], wo_ref[...])

    @pl.when((b == 0) & (c == 0))
    def _():
        o_ref[...] = jnp.zeros_like(o_ref)
        lead = slice(LEAD_ZERO % block_rows, LEAD_ZERO % block_rows + N_META)
        chunk(lead, N_META, 0)
        slead_ref[...] = s_ref[...]

    @pl.when((b > 0) & (c == 0))
    def _():
        s_ref[...] = slead_ref[...]

    @pl.when(c > 0)
    def _():
        for lo in range(0, block_rows, C):
            chunk(slice(lo, lo + C), C, (c - 1) * block_rows + lo + N_META)


def _ret_mixer(h, norm_w, w_in, w_out, head_norm, inv, *, batch, seq, casts=()):
    rows, d = h.shape
    n = w_in.shape[1]
    nh, dv = head_norm.shape
    dk = (n - 2 * nh * dv) // (2 * nh)
    assert dk == 2 * LANES
    C = RET_CHUNK
    grid, (h_spec,), o_spec = _recurrence_specs(RET_CHUNKS_PER_STEP * C, (d,), d, batch=batch, seq=seq)
    (out,), cast = _call(
        functools.partial(_ret_mixer_kernel, dk=dk, dv=dv), name="ret_mixer", grid=grid,
        in_specs=[h_spec, _const_spec((1, d)), _const_spec((d, n)), _const_spec((1, dk // 2)),
                  _const_spec((nh, dv)), _const_spec((nh * dv, d))],
        args=[h, norm_w.reshape(1, d), w_in, inv.reshape(1, dk // 2), head_norm, w_out],
        out_specs=[o_spec], out_shapes=[jax.ShapeDtypeStruct((rows, d), F32)],
        aliases={0: 0},
        scratch_shapes=[
            pltpu.VMEM((nh, dk, dv), F32),
            pltpu.VMEM((nh, dk, dv), F32),
            pltpu.VMEM((nh, C, C), F32),
            pltpu.VMEM((nh, C, dk // 2), F32),
            pltpu.VMEM((nh, C, dk // 2), F32),
            pltpu.VMEM((C, dk // 2), F32),
            pltpu.VMEM((C, dk // 2), F32),
            pltpu.VMEM((C, nh * dv), BF16),
        ], casts=casts)
    return out, cast


def _gla_proj_kernel(h_ref, nw_ref, w_ref, wg_ref, bg_ref, o_ref, la_ref, *, n_x_tiles, n_main):
    i = pl.program_id(0)
    hk = wg_ref.shape[1]
    dk = hk // GLA_HEADS
    step = 512
    assert n_main // step >= GLA_HEADS

    def project(rows):
        hn = _rms(h_ref[rows, :], nw_ref[...]).astype(BF16)
        z = _dot(hn, w_ref[:, n_main:n_main + GLA_RANK])
        xg = _dot(z.astype(BF16), wg_ref[...]) + bg_ref[...]
        for n, lo in enumerate(range(0, n_main, step)):
            y = _dot(hn, w_ref[:, lo:lo + step])
            if lo < hk:
                y = y * dk ** -0.5
            elif lo >= n_main - (n_main - 2 * hk) // 2:
                y = _silu(y)
            o_ref[rows, lo:lo + step] = y.astype(BF16)
            if n < GLA_HEADS:
                xh = xg[:, n * dk:(n + 1) * dk]
                ls = jnp.minimum(xh, 0.0) - jnp.log(1.0 + jnp.exp(-jnp.abs(xh)))
                la_ref[rows, n * dk:(n + 1) * dk] = ls * (1.0 / GLA_TAU)

    @pl.when(i < n_x_tiles)
    def _():
        for lo in range(0, h_ref.shape[0], ROW_PASS):
            project(slice(lo, lo + ROW_PASS))

    @pl.when(i == n_x_tiles)
    def _():
        o_ref[...] = jnp.zeros_like(o_ref)
        la_ref[...] = jnp.zeros_like(la_ref)
        project(slice(LEAD_ZERO, LEAD))


def _gla_proj(h, norm_w, w_in, w_gate, b_gate, *, tm, casts=()):
    rows, d = h.shape
    n = w_in.shape[1]
    n_main = n - GLA_RANK
    hk = w_gate.shape[1]
    (p, la), cast = _call(
        functools.partial(_gla_proj_kernel, n_x_tiles=rows // tm - 1, n_main=n_main),
        name="gla_proj", grid=(rows // tm,),
        in_specs=[pl.BlockSpec((tm, d), lambda i: (i, 0)), _const_spec((1, d)), _const_spec((d, n)),
                  _const_spec((GLA_RANK, hk)), _const_spec((1, hk))],
        args=[h, norm_w.reshape(1, d), w_in, w_gate, b_gate.reshape(1, hk)],
        out_specs=[pl.BlockSpec((tm, n_main), lambda i: (i, 0)), pl.BlockSpec((tm, hk), lambda i: (i, 0))],
        out_shapes=[jax.ShapeDtypeStruct((rows, n_main), BF16), jax.ShapeDtypeStruct((rows, hk), F32)],
        casts=casts)
    return p, la, cast


def _gla_core_kernel(p_ref, la_ref, h_ref, hn_ref, wo_ref, o_ref,
                     st_ref, stlead_ref, ball_ref, obuf_ref, *, dk, dv):
    bi, c = pl.program_id(0), pl.program_id(1)

    @pl.when((bi == 0) & (c == 0))
    def _():
        st_ref[...] = jnp.zeros_like(st_ref)

    def chunk(start, n, slot):
        rows = pl.ds(start, n)
        _gla_chunk(p_ref.at[rows, :], la_ref.at[rows, :], hn_ref, st_ref, ball_ref.at[slot, pl.ds(0, n), :],
                   obuf_ref.at[rows, :], dk=dk, dv=dv)

    @pl.when((bi == 0) & (c == 0))
    def _():
        o_ref[...] = jnp.zeros_like(o_ref)
        lead = LEAD_ZERO % p_ref.shape[0]
        chunk(lead, N_META, 0)
        o_ref[lead:lead + N_META, :] = (h_ref[lead:lead + N_META, :]
                                        + _dot(obuf_ref[lead:lead + N_META, :], wo_ref[...]))
        stlead_ref[...] = st_ref[...]

    @pl.when((bi > 0) & (c == 0))
    def _():
        st_ref[...] = stlead_ref[...]

    @pl.when(c > 0)
    def _():
        for n in range(p_ref.shape[0] // GLA_CHUNK):
            chunk(n * GLA_CHUNK, GLA_CHUNK, n)
        o_ref[...] = h_ref[...] + _dot(obuf_ref[...], wo_ref[...])


def _left_block_end_rows(b_ref, s):
    C, dk = b_ref.shape
    sub = 8
    bcast = lambda e, n: jnp.broadcast_to(b_ref[e:e + 1, :], (n, dk))
    if 2 * s >= sub:
        n = max(2 * s, sub)
        return jnp.concatenate([bcast(e, n) for e in range(s - 1, C, n)], axis=0)
    r = lax.broadcasted_iota(jnp.int32, (sub, dk), 0)
    tiles = []
    for t0 in range(0, C, sub):
        tile = bcast(t0 + s - 1, sub)
        for blk in range(2 * s, sub, 2 * s):
            tile = jnp.where(r >= blk, bcast(t0 + blk + s - 1, sub), tile)
        tiles.append(tile)
    return jnp.concatenate(tiles, axis=0)


def _gla_chunk(p_ref, la_ref, hn_ref, st_ref, ball_ref, obuf_ref, *, dk, dv):
    C = p_ref.shape[0]
    nh = GLA_HEADS
    k0, v0, g0 = nh * dk, 2 * nh * dk, 2 * nh * dk + nh * dv

    rowl = lax.broadcasted_iota(jnp.int32, (C, dk), 0)
    row = lax.broadcasted_iota(jnp.int32, (C, C), 0)
    col = lax.broadcasted_iota(jnp.int32, (C, C), 1)

    differ = jnp.bitwise_xor(row, col)
    level = jnp.full((C, C), -1, jnp.int32)
    for lv in range(C.bit_length() - 1):
        level = jnp.where((row > col) & (differ >= (1 << lv)), lv, level)

    a_all = la_ref[...]
    a1 = a_all.astype(BF16)
    r1 = a_all - a1.astype(F32)
    a2 = r1.astype(BF16)
    a3 = (r1 - a2.astype(F32)).astype(BF16)
    tri = (row >= col).astype(BF16)
    ball_ref[...] = _dot(tri, a1) + _dot(tri, a2) + _dot(tri, a3)

    for hd in range(nh):
        qb = p_ref[:, hd * dk:(hd + 1) * dk]
        kb = p_ref[:, k0 + hd * dk:k0 + (hd + 1) * dk]
        q = qb.astype(F32)
        k = kb.astype(F32)
        v = p_ref[:, v0 + hd * dv:v0 + (hd + 1) * dv]
        sg = p_ref[:, g0 + hd * dv:g0 + (hd + 1) * dv]
        b_ref = ball_ref.at[:, hd * dk:(hd + 1) * dk]
        b = b_ref[...]

        scores = jnp.where(col == row, _dot_nt(qb, kb), 0.0)
        for lv in range(C.bit_length() - 1):
            s = 1 << lv
            right = (rowl & s) != 0
            if s == 1:
                w = jnp.where(right, jnp.exp(la_ref[:, hd * dk:(hd + 1) * dk]), 1.0)
            else:
                w = jnp.exp(-jnp.abs(b - _left_block_end_rows(b_ref, s)))
            z = (jnp.where(right, q, k) * w).astype(BF16)
            scores = jnp.where(level == lv, _dot_nt(z, z), scores)

        btot = b_ref[C - 1:C, :]
        st = st_ref[hd]
        o = _dot(scores.astype(BF16), v) + _dot_nt((q * jnp.exp(b)).astype(BF16), st.astype(BF16))
        kt = (k * jnp.exp(btot - b)).astype(BF16)
        st_ref[hd] = st * jnp.exp(btot) + _dot_tn(v, kt)
        on = _rms(o, hn_ref[hd:hd + 1, :]) * sg.astype(F32)
        obuf_ref[:, hd * dv:(hd + 1) * dv] = on.astype(BF16)


def _gla_core(p, la, h, head_norm, w_out, *, batch, seq, casts=()):
    rows, d = h.shape
    n = p.shape[1]
    nh, dv = head_norm.shape
    hk = la.shape[1]
    dk = hk // nh
    per_step = GLA_CHUNKS_PER_STEP
    C = per_step * GLA_CHUNK
    grid, (p_spec, la_spec, h_spec), o_spec = _recurrence_specs(C, (n, hk, d), d, batch=batch, seq=seq)
    (out,), cast = _call(
        functools.partial(_gla_core_kernel, dk=dk, dv=dv), name="gla_core", grid=grid,
        in_specs=[p_spec, la_spec, h_spec, _const_spec((nh, dv)), _const_spec((nh * dv, d))],
        args=[p, la, h, head_norm, w_out],
        out_specs=[o_spec], out_shapes=[jax.ShapeDtypeStruct((rows, d), F32)],
        aliases={2: 0},
        scratch_shapes=[
            pltpu.VMEM((nh, dv, dk), F32),
            pltpu.VMEM((nh, dv, dk), F32),
            pltpu.VMEM((per_step, GLA_CHUNK, hk), F32),
            pltpu.VMEM((C, nh * dv), BF16),
        ], casts=casts)
    return out, cast


def kernel(x, meta_tokens, norm_ffn1, ffn1_w_in, ffn1_w_out, norm_mix, norm_ffn2, ffn2_w_in, ffn2_w_out, ret_w_in, ret_head_norm, ret_w_out, gla_w_in, gla_w_gate, gla_b_gate, gla_head_norm, gla_w_out, final_norm):
    batch, seq, d = x.shape
    depth = norm_ffn1.shape[0]
    tm = ROW_TILE
    assert meta_tokens.shape == (N_META, d) and seq % tm == 0 and tm >= LEAD and tm % ROW_PASS == 0

    tail = jnp.zeros((tm, d), x.dtype).at[LEAD_ZERO:LEAD].set(meta_tokens.astype(x.dtype))
    h = x.reshape(batch * seq, d)

    half = ret_w_in.shape[2] // 6 // RET_HEADS // 2
    assert half == LANES
    inv = 1.0 / (ROPE_BASE ** jnp.linspace(0.0, 1.0, half, dtype=F32))

    stages = []
    for i in range(depth):
        j = i // 2
        stages.append([(ffn1_w_in, i), (ffn1_w_out, i)])
        if i % 2 == 0:
            stages.append([(ret_w_in, j), (ret_w_out, j)])
        else:
            stages += [[(gla_w_in, j), (gla_w_gate, j)], [(gla_w_out, j)]]
        stages.append([(ffn2_w_in, i), (ffn2_w_out, i)])
    stages.append([])
    nxt = iter(stages[1:])

    w = _cast_weights(stages[0])
    for i in range(depth):
        j = i // 2
        h, w = _ffn(h, norm_ffn1[i], *w, tm=tm, tail=tail if i == 0 else None, casts=next(nxt))
        if i % 2 == 0:
            h, w = _ret_mixer(h, norm_mix[i], *w, ret_head_norm[j], inv, batch=batch, seq=seq,
                              casts=next(nxt))
        else:
            p, la, w = _gla_proj(h, norm_mix[i], *w, gla_b_gate[j], tm=tm, casts=next(nxt))
            h, w = _gla_core(p, la, h, gla_head_norm[j], *w, batch=batch, seq=seq, casts=next(nxt))
        h, w = _ffn(h, norm_ffn2[i], *w, tm=tm, final_w=final_norm if i == depth - 1 else None,
                    casts=next(nxt))
    return h.reshape(batch, seq, d)
```

```python
import functools
import math

import jax
import jax.numpy as jnp
from jax import lax
from jax.experimental import pallas as pl
from jax.experimental.pallas import tpu as pltpu

F32 = jnp.float32
BF16 = jnp.bfloat16

EPS = 1e-6
N_META = 16
LEAD = 256
LEAD_ZERO = LEAD - N_META
ROW_TILE = 1024
ROW_PASS = 512
ROPE_BASE = 10000.0
RET_HEADS = 4
GLA_HEADS = 4
GLA_RANK = 16
GLA_TAU = 16.0
RET_CHUNK = 256
RET_CHUNKS_PER_STEP = 4
GLA_CHUNK = 128
GLA_CHUNKS_PER_STEP = 4
LANES = 128
BF16_SUBLANES = 16
VMEM_LIMIT = 56 * 1024 * 1024

RET_LOG_GAMMA = tuple(math.log1p(-2.0 ** (-5.0 - h)) for h in range(RET_HEADS))


def _const_spec(shape):
    nd = len(shape)
    return pl.BlockSpec(shape, lambda *_: (0,) * nd, pipeline_mode=pl.Buffered(1))


def _rms(x, w):
    ms = jnp.mean(x * x, axis=-1, keepdims=True)
    return x * lax.rsqrt(ms + EPS) * w


def _silu(x):
    return x * jax.nn.sigmoid(x)


def _dot(a, b):
    return jnp.dot(a, b, preferred_element_type=F32)


def _dot_nt(a, b):
    return lax.dot_general(a, b, (((1,), (1,)), ((), ())), preferred_element_type=F32)


def _dot_tn(a, b):
    return lax.dot_general(a, b, (((0,), (0,)), ((), ())), preferred_element_type=F32)


def _cast_blocks(rows, n_steps):
    units = rows // BF16_SUBLANES
    assert rows % BF16_SUBLANES == 0
    return max(n for n in range(1, min(units, n_steps) + 1) if units % n == 0)


def _call(body, *, name, grid, in_specs, args, out_specs, out_shapes, scratch_shapes=(), aliases=None,
          casts=()):
    n_in, n_out, n_cast = len(args), len(out_shapes), len(casts)
    n_steps = math.prod(grid)
    linear = (lambda i: i) if len(grid) == 1 else (lambda b, c: b * grid[1] + c)
    in_specs, out_specs, out_shapes, args = list(in_specs), list(out_specs), list(out_shapes), list(args)
    for stack, layer in casts:
        _, rows, cols = stack.shape
        nb = _cast_blocks(rows, n_steps)
        rb = rows // nb
        blk = lambda *g, nb=nb: jnp.minimum(linear(*g), nb - 1)
        in_specs.append(pl.BlockSpec((None, rb, cols), lambda *g, blk=blk, layer=layer: (layer, blk(*g), 0)))
        out_specs.append(pl.BlockSpec((rb, cols), lambda *g, blk=blk: (blk(*g), 0)))
        out_shapes.append(jax.ShapeDtypeStruct((rows, cols), BF16))
        args.append(stack)

    def kern(*refs):
        ins, rest = refs[:n_in], refs[n_in:]
        cast_in, rest = rest[:n_cast], rest[n_cast:]
        outs, rest = rest[:n_out], rest[n_out:]
        cast_out, scratch = rest[:n_cast], rest[n_cast:]
        for src, dst in zip(cast_in, cast_out):
            dst[...] = src[...].astype(BF16)
        body(*ins, *outs, *scratch)

    res = pl.pallas_call(
        kern, grid=grid, in_specs=in_specs, out_specs=out_specs, out_shape=out_shapes,
        scratch_shapes=list(scratch_shapes), input_output_aliases=aliases or {},
        compiler_params=pltpu.CompilerParams(
            dimension_semantics=("arbitrary",) * len(grid), vmem_limit_bytes=VMEM_LIMIT),
        name=name,
    )(*args)
    return res[:n_out], res[n_out:]


def _cast_weights(casts, *, n_steps=8):
    _, out = _call(lambda: None, name="cast_weights", grid=(n_steps,), in_specs=[], args=[],
                   out_specs=[], out_shapes=[], casts=casts)
    return out


def _ffn_kernel(x_ref, *rest, n_x_tiles, first, final):
    rest = list(rest)
    tail_ref = rest.pop(0) if first else None
    nw_ref, wg_ref, wu_ref, wo_ref = rest[:4]
    fw_ref = rest[4] if final else None
    o_ref = rest[-1]
    i = pl.program_id(0)

    def ffn(x):
        xn = _rms(x, nw_ref[...]).astype(BF16)
        g = _dot(xn, wg_ref[...])
        u = _dot(xn, wu_ref[...])
        hid = (_silu(g) * u).astype(BF16)
        y = x + 0.5 * _dot(hid, wo_ref[...])
        return _rms(y, fw_ref[...]) if final else y

    @pl.when(i < n_x_tiles)
    def _():
        for lo in range(0, x_ref.shape[0], ROW_PASS):
            o_ref[lo:lo + ROW_PASS, :] = ffn(x_ref[lo:lo + ROW_PASS, :])

    if not final:
        @pl.when(i == n_x_tiles)
        def _():
            src = tail_ref if first else x_ref
            o_ref[...] = jnp.zeros_like(o_ref)
            o_ref[LEAD_ZERO:LEAD, :] = ffn(src[LEAD_ZERO:LEAD, :])


def _ffn(h, norm_w, w_in, w_out, *, tm, tail=None, final_w=None, casts=()):
    d = h.shape[1]
    dff = w_out.shape[0]
    first, final = tail is not None, final_w is not None
    n_x_tiles = (h.shape[0] if first else h.shape[0] - tm) // tm
    n_tiles = n_x_tiles if final else n_x_tiles + 1
    in_specs = [pl.BlockSpec((tm, d), lambda i: (jnp.minimum(i, n_x_tiles - 1), 0) if first else (i, 0))]
    args = [h]
    if first:
        in_specs.append(_const_spec((tm, d)))
        args.append(tail)
    in_specs += [
        _const_spec((1, d)),
        pl.BlockSpec((d, dff), lambda i: (0, 0), pipeline_mode=pl.Buffered(1)),
        pl.BlockSpec((d, dff), lambda i: (0, 1), pipeline_mode=pl.Buffered(1)),
        _const_spec((dff, d)),
    ]
    args += [norm_w.reshape(1, d), w_in, w_in, w_out]
    if final:
        in_specs.append(_const_spec((1, d)))
        args.append(final_w.reshape(1, d))
    (out,), cast = _call(
        functools.partial(_ffn_kernel, n_x_tiles=n_x_tiles, first=first, final=final),
        name="ffn_first" if first else "ffn_final" if final else "ffn",
        grid=(n_tiles,), in_specs=in_specs, args=args,
        out_specs=[pl.BlockSpec((tm, d), lambda i: (i, 0))],
        out_shapes=[jax.ShapeDtypeStruct((n_tiles * tm, d), F32)], casts=casts)
    return out, cast


def _chunk_block(b, c, *, chunks_per_batch, lead_block):
    first = jnp.where(b == 0, lead_block, b * chunks_per_batch)
    return jnp.where(c == 0, first, b * chunks_per_batch + c - 1)


def _recurrence_specs(C, widths, d, *, batch, seq):
    cpb = seq // C
    assert seq % C == 0 and LEAD_ZERO // C == (LEAD - 1) // C
    lead_block = batch * cpb + LEAD_ZERO // C
    idx = lambda b, c: (_chunk_block(b, c, chunks_per_batch=cpb, lead_block=lead_block), 0)
    return (batch, cpb + 1), [pl.BlockSpec((C, w), idx) for w in widths], pl.BlockSpec((C, d), idx)


def _ret_mixer_kernel(h_ref, nw_ref, w_ref, inv_ref, hn_ref, wo_ref, o_ref,
                      s_ref, slead_ref, dec_ref, dq_ref, dk_ref, cosr_ref, sinr_ref, obuf_ref, *, dk, dv):
    b, c = pl.program_id(0), pl.program_id(1)
    C = RET_CHUNK
    block_rows = h_ref.shape[0]
    nh = RET_HEADS
    half = dk // 2
    k0, v0, g0 = nh * dk, 2 * nh * dk, 2 * nh * dk + nh * dv

    @pl.when((b == 0) & (c == 0))
    def _():
        s_ref[...] = jnp.zeros_like(s_ref)
        row = lax.broadcasted_iota(jnp.int32, (C, C), 0)
        col = lax.broadcasted_iota(jnp.int32, (C, C), 1)
        rel = (row - col).astype(F32)
        rowl = lax.broadcasted_iota(jnp.int32, (C, half), 0).astype(F32)
        ang = rowl * inv_ref[...]
        cosr_ref[...] = jnp.cos(ang)
        sinr_ref[...] = jnp.sin(ang)
        for hd in range(nh):
            lg = RET_LOG_GAMMA[hd]
            dec_ref[hd] = jnp.where(rel >= 0, jnp.exp(lg * jnp.maximum(rel, 0.0)), 0.0)
            dq_ref[hd] = jnp.exp(lg * (rowl + 1.0))
            dk_ref[hd] = jnp.exp(lg * (C - 1.0 - rowl))

    def chunk(rows, n, first_pos):
        ang0 = jnp.asarray(first_pos, F32) * inv_ref[...]
        cb, sb = jnp.cos(ang0), jnp.sin(ang0)
        cos = cb * cosr_ref[:n] - sb * sinr_ref[:n]
        sin = sb * cosr_ref[:n] + cb * sinr_ref[:n]

        def rotary(y):
            t1, t2 = y[:, :half], y[:, half:]
            return jnp.concatenate([t1 * cos - t2 * sin, t1 * sin + t2 * cos], axis=1)

        x = h_ref[rows, :]
        xn = _rms(x, nw_ref[...]).astype(BF16)
        for hd in range(nh):
            q = rotary(_dot(xn, w_ref[:, hd * dk:(hd + 1) * dk]))
            k = rotary(_dot(xn, w_ref[:, k0 + hd * dk:k0 + (hd + 1) * dk])) * dk ** -0.5
            v = _dot(xn, w_ref[:, v0 + hd * dv:v0 + (hd + 1) * dv]).astype(BF16)
            g = _dot(xn, w_ref[:, g0 + hd * dv:g0 + (hd + 1) * dv])
            qd = (q * jnp.concatenate([dq_ref[hd, :n]] * 2, axis=1)).astype(BF16)
            kd = (k * jnp.concatenate([dk_ref[hd, C - n:]] * 2, axis=1)).astype(BF16)
            s = (_dot_nt(q.astype(BF16), k.astype(BF16)) * dec_ref[hd, :n, :n]).astype(BF16)
            st = s_ref[hd]
            o = _dot(s, v) + _dot(qd, st.astype(BF16))
            s_ref[hd] = st * math.exp(RET_LOG_GAMMA[hd] * n) + _dot_tn(kd, v)
            on = _rms(o, hn_ref[hd:hd + 1, :]) * _silu(g)
            obuf_ref[:n, hd * dv:(hd + 1) * dv] = on.astype(BF16)
        o_ref[rows, :] = x + _dot(obuf_ref[:n], wo_ref[...])

    @pl.when((b == 0) & (c == 0))
    def _():
        o_ref[...] = jnp.zeros_like(o_ref)
        lead = slice(LEAD_ZERO % block_rows, LEAD_ZERO % block_rows + N_META)
        chunk(lead, N_META, 0)
        slead_ref[...] = s_ref[...]

    @pl.when((b > 0) & (c == 0))
    def _():
        s_ref[...] = slead_ref[...]

    @pl.when(c > 0)
    def _():
        for lo in range(0, block_rows, C):
            chunk(slice(lo, lo + C), C, (c - 1) * block_rows + lo + N_META)


def _ret_mixer(h, norm_w, w_in, w_out, head_norm, inv, *, batch, seq, casts=()):
    rows, d = h.shape
    n = w_in.shape[1]
    nh, dv = head_norm.shape
    dk = (n - 2 * nh * dv) // (2 * nh)
    assert dk == 2 * LANES
    C = RET_CHUNK
    grid, (h_spec,), o_spec = _recurrence_specs(RET_CHUNKS_PER_STEP * C, (d,), d, batch=batch, seq=seq)
    (out,), cast = _call(
        functools.partial(_ret_mixer_kernel, dk=dk, dv=dv), name="ret_mixer", grid=grid,
        in_specs=[h_spec, _const_spec((1, d)), _const_spec((d, n)), _const_spec((1, dk // 2)),
                  _const_spec((nh, dv)), _const_spec((nh * dv, d))],
        args=[h, norm_w.reshape(1, d), w_in, inv.reshape(1, dk // 2), head_norm, w_out],
        out_specs=[o_spec], out_shapes=[jax.ShapeDtypeStruct((rows, d), F32)],
        aliases={0: 0},
        scratch_shapes=[
            pltpu.VMEM((nh, dk, dv), F32),
            pltpu.VMEM((nh, dk, dv), F32),
            pltpu.VMEM((nh, C, C), F32),
            pltpu.VMEM((nh, C, dk // 2), F32),
            pltpu.VMEM((nh, C, dk // 2), F32),
            pltpu.VMEM((C, dk // 2), F32),
            pltpu.VMEM((C, dk // 2), F32),
            pltpu.VMEM((C, nh * dv), BF16),
        ], casts=casts)
    return out, cast


def _gla_proj_kernel(h_ref, nw_ref, w_ref, wg_ref, bg_ref, o_ref, la_ref, *, n_x_tiles, n_main):
    i = pl.program_id(0)
    hk = wg_ref.shape[1]
    dk = hk // GLA_HEADS
    step = 512
    assert n_main // step >= GLA_HEADS

    def project(rows):
        hn = _rms(h_ref[rows, :], nw_ref[...]).astype(BF16)
        z = _dot(hn, w_ref[:, n_main:n_main + GLA_RANK])
        xg = _dot(z.astype(BF16), wg_ref[...]) + bg_ref[...]
        for n, lo in enumerate(range(0, n_main, step)):
            y = _dot(hn, w_ref[:, lo:lo + step])
            if lo < hk:
                y = y * dk ** -0.5
            elif lo >= n_main - (n_main - 2 * hk) // 2:
                y = _silu(y)
            o_ref[rows, lo:lo + step] = y.astype(BF16)
            if n < GLA_HEADS:
                xh = xg[:, n * dk:(n + 1) * dk]
                ls = jnp.minimum(xh, 0.0) - jnp.log(1.0 + jnp.exp(-jnp.abs(xh)))
                la_ref[rows, n * dk:(n + 1) * dk] = ls * (1.0 / GLA_TAU)

    @pl.when(i < n_x_tiles)
    def _():
        for lo in range(0, h_ref.shape[0], ROW_PASS):
            project(slice(lo, lo + ROW_PASS))

    @pl.when(i == n_x_tiles)
    def _():
        o_ref[...] = jnp.zeros_like(o_ref)
        la_ref[...] = jnp.zeros_like(la_ref)
        project(slice(LEAD_ZERO, LEAD))


def _gla_proj(h, norm_w, w_in, w_gate, b_gate, *, tm, casts=()):
    rows, d = h.shape
    n = w_in.shape[1]
    n_main = n - GLA_RANK
    hk = w_gate.shape[1]
    (p, la), cast = _call(
        functools.partial(_gla_proj_kernel, n_x_tiles=rows // tm - 1, n_main=n_main),
        name="gla_proj", grid=(rows // tm,),
        in_specs=[pl.BlockSpec((tm, d), lambda i: (i, 0)), _const_spec((1, d)), _const_spec((d, n)),
                  _const_spec((GLA_RANK, hk)), _const_spec((1, hk))],
        args=[h, norm_w.reshape(1, d), w_in, w_gate, b_gate.reshape(1, hk)],
        out_specs=[pl.BlockSpec((tm, n_main), lambda i: (i, 0)), pl.BlockSpec((tm, hk), lambda i: (i, 0))],
        out_shapes=[jax.ShapeDtypeStruct((rows, n_main), BF16), jax.ShapeDtypeStruct((rows, hk), F32)],
        casts=casts)
    return p, la, cast


def _gla_core_kernel(p_ref, la_ref, h_ref, hn_ref, wo_ref, o_ref,
                     st_ref, stlead_ref, ball_ref, obuf_ref, *, dk, dv):
    bi, c = pl.program_id(0), pl.program_id(1)

    @pl.when((bi == 0) & (c == 0))
    def _():
        st_ref[...] = jnp.zeros_like(st_ref)

    def chunk(start, n, slot):
        rows = pl.ds(start, n)
        _gla_chunk(p_ref.at[rows, :], la_ref.at[rows, :], hn_ref, st_ref, ball_ref.at[slot, pl.ds(0, n), :],
                   obuf_ref.at[rows, :], dk=dk, dv=dv)

    @pl.when((bi == 0) & (c == 0))
    def _():
        o_ref[...] = jnp.zeros_like(o_ref)
        lead = LEAD_ZERO % p_ref.shape[0]
        chunk(lead, N_META, 0)
        o_ref[lead:lead + N_META, :] = (h_ref[lead:lead + N_META, :]
                                        + _dot(obuf_ref[lead:lead + N_META, :], wo_ref[...]))
        stlead_ref[...] = st_ref[...]

    @pl.when((bi > 0) & (c == 0))
    def _():
        st_ref[...] = stlead_ref[...]

    @pl.when(c > 0)
    def _():
        for n in range(p_ref.shape[0] // GLA_CHUNK):
            chunk(n * GLA_CHUNK, GLA_CHUNK, n)
        o_ref[...] = h_ref[...] + _dot(obuf_ref[...], wo_ref[...])


def _left_block_end_rows(b_ref, s):
    C, dk = b_ref.shape
    sub = 8
    bcast = lambda e, n: jnp.broadcast_to(b_ref[e:e + 1, :], (n, dk))
    if 2 * s >= sub:
        n = max(2 * s, sub)
        return jnp.concatenate([bcast(e, n) for e in range(s - 1, C, n)], axis=0)
    r = lax.broadcasted_iota(jnp.int32, (sub, dk), 0)
    tiles = []
    for t0 in range(0, C, sub):
        tile = bcast(t0 + s - 1, sub)
        for blk in range(2 * s, sub, 2 * s):
            tile = jnp.where(r >= blk, bcast(t0 + blk + s - 1, sub), tile)
        tiles.append(tile)
    return jnp.concatenate(tiles, axis=0)


def _gla_chunk(p_ref, la_ref, hn_ref, st_ref, ball_ref, obuf_ref, *, dk, dv):
    C = p_ref.shape[0]
    nh = GLA_HEADS
    k0, v0, g0 = nh * dk, 2 * nh * dk, 2 * nh * dk + nh * dv

    rowl = lax.broadcasted_iota(jnp.int32, (C, dk), 0)
    row = lax.broadcasted_iota(jnp.int32, (C, C), 0)
    col = lax.broadcasted_iota(jnp.int32, (C, C), 1)

    differ = jnp.bitwise_xor(row, col)
    level = jnp.full((C, C), -1, jnp.int32)
    for lv in range(C.bit_length() - 1):
        level = jnp.where((row > col) & (differ >= (1 << lv)), lv, level)

    a_all = la_ref[...]
    a1 = a_all.astype(BF16)
    r1 = a_all - a1.astype(F32)
    a2 = r1.astype(BF16)
    a3 = (r1 - a2.astype(F32)).astype(BF16)
    tri = (row >= col).astype(BF16)
    ball_ref[...] = _dot(tri, a1) + _dot(tri, a2) + _dot(tri, a3)

    for hd in range(nh):
        qb = p_ref[:, hd * dk:(hd + 1) * dk]
        kb = p_ref[:, k0 + hd * dk:k0 + (hd + 1) * dk]
        q = qb.astype(F32)
        k = kb.astype(F32)
        v = p_ref[:, v0 + hd * dv:v0 + (hd + 1) * dv]
        sg = p_ref[:, g0 + hd * dv:g0 + (hd + 1) * dv]
        b_ref = ball_ref.at[:, hd * dk:(hd + 1) * dk]
        b = b_ref[...]

        scores = jnp.where(col == row, _dot_nt(qb, kb), 0.0)
        for lv in range(C.bit_length() - 1):
            s = 1 << lv
            right = (rowl & s) != 0
            if s == 1:
                w = jnp.where(right, jnp.exp(la_ref[:, hd * dk:(hd + 1) * dk]), 1.0)
            else:
                w = jnp.exp(-jnp.abs(b - _left_block_end_rows(b_ref, s)))
            z = (jnp.where(right, q, k) * w).astype(BF16)
            scores = jnp.where(level == lv, _dot_nt(z, z), scores)

        btot = b_ref[C - 1:C, :]
        st = st_ref[hd]
        o = _dot(scores.astype(BF16), v) + _dot_nt((q * jnp.exp(b)).astype(BF16), st.astype(BF16))
        kt = (k * jnp.exp(btot - b)).astype(BF16)
        st_ref[hd] = st * jnp.exp(btot) + _dot_tn(v, kt)
        on = _rms(o, hn_ref[hd:hd + 1, :]) * sg.astype(F32)
        obuf_ref[:, hd * dv:(hd + 1) * dv] = on.astype(BF16)


def _gla_core(p, la, h, head_norm, w_out, *, batch, seq, casts=()):
    rows, d = h.shape
    n = p.shape[1]
    nh, dv = head_norm.shape
    hk = la.shape[1]
    dk = hk // nh
    per_step = GLA_CHUNKS_PER_STEP
    C = per_step * GLA_CHUNK
    grid, (p_spec, la_spec, h_spec), o_spec = _recurrence_specs(C, (n, hk, d), d, batch=batch, seq=seq)
    (out,), cast = _call(
        functools.partial(_gla_core_kernel, dk=dk, dv=dv), name="gla_core", grid=grid,
        in_specs=[p_spec, la_spec, h_spec, _const_spec((nh, dv)), _const_spec((nh * dv, d))],
        args=[p, la, h, head_norm, w_out],
        out_specs=[o_spec], out_shapes=[jax.ShapeDtypeStruct((rows, d), F32)],
        aliases={2: 0},
        scratch_shapes=[
            pltpu.VMEM((nh, dv, dk), F32),
            pltpu.VMEM((nh, dv, dk), F32),
            pltpu.VMEM((per_step, GLA_CHUNK, hk), F32),
            pltpu.VMEM((C, nh * dv), BF16),
        ], casts=casts)
    return out, cast


def kernel(x, meta_tokens, norm_ffn1, ffn1_w_in, ffn1_w_out, norm_mix, norm_ffn2, ffn2_w_in, ffn2_w_out, ret_w_in, ret_head_norm, ret_w_out, gla_w_in, gla_w_gate, gla_b_gate, gla_head_norm, gla_w_out, final_norm):
    batch, seq, d = x.shape
    depth = norm_ffn1.shape[0]
    tm = ROW_TILE
    assert meta_tokens.shape == (N_META, d) and seq % tm == 0 and tm >= LEAD and tm % ROW_PASS == 0

    tail = jnp.zeros((tm, d), x.dtype).at[LEAD_ZERO:LEAD].set(meta_tokens.astype(x.dtype))
    h = x.reshape(batch * seq, d)

    half = ret_w_in.shape[2] // 6 // RET_HEADS // 2
    assert half == LANES
    inv = 1.0 / (ROPE_BASE ** jnp.linspace(0.0, 1.0, half, dtype=F32))

    stages = []
    for i in range(depth):
        j = i // 2
        stages.append([(ffn1_w_in, i), (ffn1_w_out, i)])
        if i % 2 == 0:
            stages.append([(ret_w_in, j), (ret_w_out, j)])
        else:
            stages += [[(gla_w_in, j), (gla_w_gate, j)], [(gla_w_out, j)]]
        stages.append([(ffn2_w_in, i), (ffn2_w_out, i)])
    stages.append([])
    nxt = iter(stages[1:])

    w = _cast_weights(stages[0])
    for i in range(depth):
        j = i // 2
        h, w = _ffn(h, norm_ffn1[i], *w, tm=tm, tail=tail if i == 0 else None, casts=next(nxt))
        if i % 2 == 0:
            h, w = _ret_mixer(h, norm_mix[i], *w, ret_head_norm[j], inv, batch=batch, seq=seq,
                              casts=next(nxt))
        else:
            p, la, w = _gla_proj(h, norm_mix[i], *w, gla_b_gate[j], tm=tm, casts=next(nxt))
            h, w = _gla_core(p, la, h, gla_head_norm[j], *w, batch=batch, seq=seq, casts=next(nxt))
        h, w = _ffn(h, norm_ffn2[i], *w, tm=tm, final_w=final_norm if i == depth - 1 else None,
                    casts=next(nxt))
    return h.reshape(batch, seq, d)
```

```python
import functools
import math

import jax
import jax.numpy as jnp
from jax import lax
from jax.experimental import pallas as pl
from jax.experimental.pallas import tpu as pltpu

F32 = jnp.float32
BF16 = jnp.bfloat16

EPS = 1e-6
N_META = 16
LEAD = 256
LEAD_ZERO = LEAD - N_META
ROW_TILE = 1024
ROW_PASS = 512
ROPE_BASE = 10000.0
RET_HEADS = 4
GLA_HEADS = 4
GLA_RANK = 16
GLA_TAU = 16.0
RET_CHUNK = 256
RET_CHUNKS_PER_STEP = 2
GLA_CHUNK = 128
GLA_CHUNKS_PER_STEP = 4
LANES = 128
BF16_SUBLANES = 16
VMEM_LIMIT = 56 * 1024 * 1024

RET_LOG_GAMMA = tuple(math.log1p(-2.0 ** (-5.0 - h)) for h in range(RET_HEADS))


def _const_spec(shape):
    nd = len(shape)
    return pl.BlockSpec(shape, lambda *_: (0,) * nd, pipeline_mode=pl.Buffered(1))


def _rms(x, w):
    ms = jnp.mean(x * x, axis=-1, keepdims=True)
    return x * lax.rsqrt(ms + EPS) * w


def _silu(x):
    return x * jax.nn.sigmoid(x)


def _dot(a, b):
    return jnp.dot(a, b, preferred_element_type=F32)


def _dot_nt(a, b):
    return lax.dot_general(a, b, (((1,), (1,)), ((), ())), preferred_element_type=F32)


def _dot_tn(a, b):
    return lax.dot_general(a, b, (((0,), (0,)), ((), ())), preferred_element_type=F32)


def _cast_blocks(rows, n_steps):
    units = rows // BF16_SUBLANES
    assert rows % BF16_SUBLANES == 0
    return max(n for n in range(1, min(units, n_steps) + 1) if units % n == 0)


def _call(body, *, name, grid, in_specs, args, out_specs, out_shapes, scratch_shapes=(), aliases=None,
          casts=()):
    n_in, n_out, n_cast = len(args), len(out_shapes), len(casts)
    n_steps = math.prod(grid)
    linear = (lambda i: i) if len(grid) == 1 else (lambda b, c: b * grid[1] + c)
    in_specs, out_specs, out_shapes, args = list(in_specs), list(out_specs), list(out_shapes), list(args)
    transposed = []
    for stack, layer, *flags in casts:
        transposed.append(bool(flags))
        if flags:
            _, cols, rows = stack.shape
            cb = 2 * LANES
            nb = pl.cdiv(cols, cb)
            assert nb <= n_steps
            blk = lambda *g, nb=nb: jnp.minimum(linear(*g), nb - 1)
            in_specs.append(pl.BlockSpec((None, cb, rows), lambda *g, blk=blk, layer=layer: (layer, blk(*g), 0)))
            out_specs.append(pl.BlockSpec((rows, cb), lambda *g, blk=blk: (0, blk(*g))))
        else:
            _, rows, cols = stack.shape
            nb = _cast_blocks(rows, n_steps)
            rb = rows // nb
            blk = lambda *g, nb=nb: jnp.minimum(linear(*g), nb - 1)
            in_specs.append(pl.BlockSpec((None, rb, cols), lambda *g, blk=blk, layer=layer: (layer, blk(*g), 0)))
            out_specs.append(pl.BlockSpec((rb, cols), lambda *g, blk=blk: (blk(*g), 0)))
        out_shapes.append(jax.ShapeDtypeStruct((rows, cols), BF16))
        args.append(stack)

    def kern(*refs):
        ins, rest = refs[:n_in], refs[n_in:]
        cast_in, rest = rest[:n_cast], rest[n_cast:]
        outs, rest = rest[:n_out], rest[n_out:]
        cast_out, scratch = rest[:n_cast], rest[n_cast:]
        for src, dst, t in zip(cast_in, cast_out, transposed):
            dst[...] = (src[...].T if t else src[...]).astype(BF16)
        body(*ins, *outs, *scratch)

    res = pl.pallas_call(
        kern, grid=grid, in_specs=in_specs, out_specs=out_specs, out_shape=out_shapes,
        scratch_shapes=list(scratch_shapes), input_output_aliases=aliases or {},
        compiler_params=pltpu.CompilerParams(
            dimension_semantics=("arbitrary",) * len(grid), vmem_limit_bytes=VMEM_LIMIT),
        name=name,
    )(*args)
    return res[:n_out], res[n_out:]


def _cast_weights(casts, *, n_steps=8):
    _, out = _call(lambda: None, name="cast_weights", grid=(n_steps,), in_specs=[], args=[],
                   out_specs=[], out_shapes=[], casts=casts)
    return out


def _ffn_kernel(x_ref, *rest, n_x_tiles, first, final):
    rest = list(rest)
    tail_ref = rest.pop(0) if first else None
    nw_ref, wg_ref, wu_ref, wo_ref = rest[:4]
    fw_ref = rest[4] if final else None
    o_ref = rest[-1]
    i = pl.program_id(0)

    def ffn(x):
        xn = _rms(x, nw_ref[...]).astype(BF16)
        g = _dot(xn, wg_ref[...])
        u = _dot(xn, wu_ref[...])
        hid = (_silu(g) * u).astype(BF16)
        y = x + 0.5 * _dot(hid, wo_ref[...])
        return _rms(y, fw_ref[...]) if final else y

    @pl.when(i < n_x_tiles)
    def _():
        for lo in range(0, x_ref.shape[0], ROW_PASS):
            o_ref[lo:lo + ROW_PASS, :] = ffn(x_ref[lo:lo + ROW_PASS, :])

    if not final:
        @pl.when(i == n_x_tiles)
        def _():
            src = tail_ref if first else x_ref
            o_ref[...] = jnp.zeros_like(o_ref)
            o_ref[LEAD_ZERO:LEAD, :] = ffn(src[LEAD_ZERO:LEAD, :])


def _ffn(h, norm_w, w_in, w_out, *, tm, tail=None, final_w=None, casts=()):
    d = h.shape[1]
    dff = w_out.shape[0]
    first, final = tail is not None, final_w is not None
    n_x_tiles = (h.shape[0] if first else h.shape[0] - tm) // tm
    n_tiles = n_x_tiles if final else n_x_tiles + 1
    in_specs = [pl.BlockSpec((tm, d), lambda i: (jnp.minimum(i, n_x_tiles - 1), 0) if first else (i, 0))]
    args = [h]
    if first:
        in_specs.append(_const_spec((tm, d)))
        args.append(tail)
    in_specs += [
        _const_spec((1, d)),
        pl.BlockSpec((d, dff), lambda i: (0, 0), pipeline_mode=pl.Buffered(1)),
        pl.BlockSpec((d, dff), lambda i: (0, 1), pipeline_mode=pl.Buffered(1)),
        _const_spec((dff, d)),
    ]
    args += [norm_w.reshape(1, d), w_in, w_in, w_out]
    if final:
        in_specs.append(_const_spec((1, d)))
        args.append(final_w.reshape(1, d))
    (out,), cast = _call(
        functools.partial(_ffn_kernel, n_x_tiles=n_x_tiles, first=first, final=final),
        name="ffn_first" if first else "ffn_final" if final else "ffn",
        grid=(n_tiles,), in_specs=in_specs, args=args,
        out_specs=[pl.BlockSpec((tm, d), lambda i: (i, 0))],
        out_shapes=[jax.ShapeDtypeStruct((n_tiles * tm, d), F32)], casts=casts)
    return out, cast


def _chunk_block(b, c, *, chunks_per_batch, lead_block):
    first = jnp.where(b == 0, lead_block, b * chunks_per_batch)
    return jnp.where(c == 0, first, b * chunks_per_batch + c - 1)


def _recurrence_specs(C, widths, d, *, batch, seq):
    cpb = seq // C
    assert seq % C == 0 and LEAD_ZERO // C == (LEAD - 1) // C
    lead_block = batch * cpb + LEAD_ZERO // C
    idx = lambda b, c: (_chunk_block(b, c, chunks_per_batch=cpb, lead_block=lead_block), 0)
    return (batch, cpb + 1), [pl.BlockSpec((C, w), idx) for w in widths], pl.BlockSpec((C, d), idx)


def _ret_mixer_kernel(h_ref, nw_ref, w_ref, inv_ref, hn_ref, wo_ref, o_ref,
                      s_ref, slead_ref, dec_ref, dq_ref, dk_ref, cosr_ref, sinr_ref, obuf_ref, *, dk, dv):
    b, c = pl.program_id(0), pl.program_id(1)
    C = RET_CHUNK
    block_rows = h_ref.shape[0]
    nh = RET_HEADS
    half = dk // 2
    k0, v0, g0 = nh * dk, 2 * nh * dk, 2 * nh * dk + nh * dv

    @pl.when((b == 0) & (c == 0))
    def _():
        s_ref[...] = jnp.zeros_like(s_ref)
        row = lax.broadcasted_iota(jnp.int32, (C, C), 0)
        col = lax.broadcasted_iota(jnp.int32, (C, C), 1)
        rel = (row - col).astype(F32)
        rowl = lax.broadcasted_iota(jnp.int32, (C, half), 0).astype(F32)
        ang = rowl * inv_ref[...]
        cosr_ref[...] = jnp.cos(ang)
        sinr_ref[...] = jnp.sin(ang)
        for hd in range(nh):
            lg = RET_LOG_GAMMA[hd]
            dec_ref[hd] = jnp.where(rel >= 0, jnp.exp(lg * jnp.maximum(rel, 0.0)), 0.0)
            dq_ref[hd] = jnp.exp(lg * (rowl + 1.0))
            dk_ref[hd] = jnp.exp(lg * (C - 1.0 - rowl))

    def chunk(rows, n, first_pos):
        ang0 = jnp.asarray(first_pos, F32) * inv_ref[...]
        cb, sb = jnp.cos(ang0), jnp.sin(ang0)
        cos = cb * cosr_ref[:n] - sb * sinr_ref[:n]
        sin = sb * cosr_ref[:n] + cb * sinr_ref[:n]

        def rotary(y):
            t1, t2 = y[:, :half], y[:, half:]
            return jnp.concatenate([t1 * cos - t2 * sin, t1 * sin + t2 * cos], axis=1)

        x = h_ref[rows, :]
        xn = _rms(x, nw_ref[...]).astype(BF16)
        for hd in range(nh):
            q = rotary(_dot(xn, w_ref[:, hd * dk:(hd + 1) * dk]))
            k = rotary(_dot(xn, w_ref[:, k0 + hd * dk:k0 + (hd + 1) * dk])) * dk ** -0.5
            v = _dot(xn, w_ref[:, v0 + hd * dv:v0 + (hd + 1) * dv]).astype(BF16)
            g = _dot(xn, w_ref[:, g0 + hd * dv:g0 + (hd + 1) * dv])
            qd = (q * jnp.concatenate([dq_ref[hd, :n]] * 2, axis=1)).astype(BF16)
            kd = (k * jnp.concatenate([dk_ref[hd, C - n:]] * 2, axis=1)).astype(BF16)
            s = (_dot_nt(q.astype(BF16), k.astype(BF16)) * dec_ref[hd, :n, :n]).astype(BF16)
            st = s_ref[hd]
            o = _dot(s, v) + _dot(qd, st.astype(BF16))
            s_ref[hd] = st * math.exp(RET_LOG_GAMMA[hd] * n) + _dot_tn(kd, v)
            on = _rms(o, hn_ref[hd:hd + 1, :]) * _silu(g)
            obuf_ref[:n, hd * dv:(hd + 1) * dv] = on.astype(BF16)
        o_ref[rows, :] = x + _dot(obuf_ref[:n], wo_ref[...])

    @pl.when((b == 0) & (c == 0))
    def _():
        o_ref[...] = jnp.zeros_like(o_ref)
        lead = slice(LEAD_ZERO % block_rows, LEAD_ZERO % block_rows + N_META)
        chunk(lead, N_META, 0)
        slead_ref[...] = s_ref[...]

    @pl.when((b > 0) & (c == 0))
    def _():
        s_ref[...] = slead_ref[...]

    @pl.when(c > 0)
    def _():
        for lo in range(0, block_rows, C):
            chunk(slice(lo, lo + C), C, (c - 1) * block_rows + lo + N_META)


def _ret_mixer(h, norm_w, w_in, w_out, head_norm, inv, *, batch, seq, casts=()):
    rows, d = h.shape
    n = w_in.shape[1]
    nh, dv = head_norm.shape
    dk = (n - 2 * nh * dv) // (2 * nh)
    assert dk == 2 * LANES
    C = RET_CHUNK
    grid, (h_spec,), o_spec = _recurrence_specs(RET_CHUNKS_PER_STEP * C, (d,), d, batch=batch, seq=seq)
    (out,), cast = _call(
        functools.partial(_ret_mixer_kernel, dk=dk, dv=dv), name="ret_mixer", grid=grid,
        in_specs=[h_spec, _const_spec((1, d)), _const_spec((d, n)), _const_spec((1, dk // 2)),
                  _const_spec((nh, dv)), _const_spec((nh * dv, d))],
        args=[h, norm_w.reshape(1, d), w_in, inv.reshape(1, dk // 2), head_norm, w_out],
        out_specs=[o_spec], out_shapes=[jax.ShapeDtypeStruct((rows, d), F32)],
        aliases={0: 0},
        scratch_shapes=[
            pltpu.VMEM((nh, dk, dv), F32),
            pltpu.VMEM((nh, dk, dv), F32),
            pltpu.VMEM((nh, C, C), F32),
            pltpu.VMEM((nh, C, dk // 2), F32),
            pltpu.VMEM((nh, C, dk // 2), F32),
            pltpu.VMEM((C, dk // 2), F32),
            pltpu.VMEM((C, dk // 2), F32),
            pltpu.VMEM((C, nh * dv), BF16),
        ], casts=casts)
    return out, cast


def _gla_proj_kernel(h_ref, nw_ref, w_ref, wg_ref, bg_ref, o_ref, la_ref, *, n_x_tiles, n_main):
    i = pl.program_id(0)
    hk = wg_ref.shape[1]
    dk = hk // GLA_HEADS
    step = 512
    assert n_main // step >= GLA_HEADS

    def project(rows):
        hn = _rms(h_ref[rows, :], nw_ref[...]).astype(BF16)
        z = _dot(hn, w_ref[:, n_main:n_main + GLA_RANK])
        xg = _dot(z.astype(BF16), wg_ref[...]) + bg_ref[...]
        for n, lo in enumerate(range(0, n_main, step)):
            y = _dot(hn, w_ref[:, lo:lo + step])
            if lo < hk:
                y = y * dk ** -0.5
            elif lo >= n_main - (n_main - 2 * hk) // 2:
                y = _silu(y)
            o_ref[rows, lo:lo + step] = y.astype(BF16)
            if n < GLA_HEADS:
                xh = xg[:, n * dk:(n + 1) * dk]
                ls = jnp.minimum(xh, 0.0) - jnp.log(1.0 + jnp.exp(-jnp.abs(xh)))
                la_ref[rows, n * dk:(n + 1) * dk] = ls * (1.0 / GLA_TAU)

    @pl.when(i < n_x_tiles)
    def _():
        for lo in range(0, h_ref.shape[0], ROW_PASS):
            project(slice(lo, lo + ROW_PASS))

    @pl.when(i == n_x_tiles)
    def _():
        o_ref[...] = jnp.zeros_like(o_ref)
        la_ref[...] = jnp.zeros_like(la_ref)
        project(slice(LEAD_ZERO, LEAD))


def _gla_proj(h, norm_w, w_in, w_gate, b_gate, *, tm, casts=()):
    rows, d = h.shape
    n = w_in.shape[1]
    n_main = n - GLA_RANK
    hk = w_gate.shape[1]
    (p, la), cast = _call(
        functools.partial(_gla_proj_kernel, n_x_tiles=rows // tm - 1, n_main=n_main),
        name="gla_proj", grid=(rows // tm,),
        in_specs=[pl.BlockSpec((tm, d), lambda i: (i, 0)), _const_spec((1, d)), _const_spec((d, n)),
                  _const_spec((GLA_RANK, hk)), _const_spec((1, hk))],
        args=[h, norm_w.reshape(1, d), w_in, w_gate, b_gate.reshape(1, hk)],
        out_specs=[pl.BlockSpec((tm, n_main), lambda i: (i, 0)), pl.BlockSpec((tm, hk), lambda i: (i, 0))],
        out_shapes=[jax.ShapeDtypeStruct((rows, n_main), BF16), jax.ShapeDtypeStruct((rows, hk), F32)],
        casts=casts)
    return p, la, cast


def _gla_core_kernel(p_ref, la_ref, h_ref, hn_ref, wo_ref, o_ref,
                     st_ref, stlead_ref, ball_ref, obuf_ref, *, dk, dv):
    bi, c = pl.program_id(0), pl.program_id(1)

    @pl.when((bi == 0) & (c == 0))
    def _():
        st_ref[...] = jnp.zeros_like(st_ref)

    def chunk(start, n, slot):
        rows = pl.ds(start, n)
        _gla_chunk(p_ref.at[rows, :], la_ref.at[rows, :], hn_ref, st_ref, ball_ref.at[slot, pl.ds(0, n), :],
                   obuf_ref.at[rows, :], dk=dk, dv=dv)

    @pl.when((bi == 0) & (c == 0))
    def _():
        o_ref[...] = jnp.zeros_like(o_ref)
        lead = LEAD_ZERO % p_ref.shape[0]
        chunk(lead, N_META, 0)
        o_ref[lead:lead + N_META, :] = (h_ref[lead:lead + N_META, :]
                                        + _dot(obuf_ref[lead:lead + N_META, :], wo_ref[...]))
        stlead_ref[...] = st_ref[...]

    @pl.when((bi > 0) & (c == 0))
    def _():
        st_ref[...] = stlead_ref[...]

    @pl.when(c > 0)
    def _():
        for n in range(p_ref.shape[0] // GLA_CHUNK):
            chunk(n * GLA_CHUNK, GLA_CHUNK, n)
        o_ref[...] = h_ref[...] + _dot(obuf_ref[...], wo_ref[...])


def _left_block_end_rows(b_ref, s):
    C, dk = b_ref.shape
    sub = 8
    bcast = lambda e, n: jnp.broadcast_to(b_ref[e:e + 1, :], (n, dk))
    if 2 * s >= sub:
        n = max(2 * s, sub)
        return jnp.concatenate([bcast(e, n) for e in range(s - 1, C, n)], axis=0)
    r = lax.broadcasted_iota(jnp.int32, (sub, dk), 0)
    tiles = []
    for t0 in range(0, C, sub):
        tile = bcast(t0 + s - 1, sub)
        for blk in range(2 * s, sub, 2 * s):
            tile = jnp.where(r >= blk, bcast(t0 + blk + s - 1, sub), tile)
        tiles.append(tile)
    return jnp.concatenate(tiles, axis=0)


def _gla_chunk(p_ref, la_ref, hn_ref, st_ref, ball_ref, obuf_ref, *, dk, dv):
    C = p_ref.shape[0]
    nh = GLA_HEADS
    k0, v0, g0 = nh * dk, 2 * nh * dk, 2 * nh * dk + nh * dv

    rowl = lax.broadcasted_iota(jnp.int32, (C, dk), 0)
    row = lax.broadcasted_iota(jnp.int32, (C, C), 0)
    col = lax.broadcasted_iota(jnp.int32, (C, C), 1)

    differ = jnp.bitwise_xor(row, col)
    level = jnp.full((C, C), -1, jnp.int32)
    for lv in range(C.bit_length() - 1):
        level = jnp.where((row > col) & (differ >= (1 << lv)), lv, level)

    a_all = la_ref[...]
    a1 = a_all.astype(BF16)
    r1 = a_all - a1.astype(F32)
    a2 = r1.astype(BF16)
    a3 = (r1 - a2.astype(F32)).astype(BF16)
    tri = (row >= col).astype(BF16)
    ball_ref[...] = _dot(tri, a1) + _dot(tri, a2) + _dot(tri, a3)

    for hd in range(nh):
        qb = p_ref[:, hd * dk:(hd + 1) * dk]
        kb = p_ref[:, k0 + hd * dk:k0 + (hd + 1) * dk]
        q = qb.astype(F32)
        k = kb.astype(F32)
        v = p_ref[:, v0 + hd * dv:v0 + (hd + 1) * dv]
        sg = p_ref[:, g0 + hd * dv:g0 + (hd + 1) * dv]
        b_ref = ball_ref.at[:, hd * dk:(hd + 1) * dk]
        b = b_ref[...]

        scores = jnp.where(col == row, _dot_nt(qb, kb), 0.0)
        for lv in range(C.bit_length() - 1):
            s = 1 << lv
            right = (rowl & s) != 0
            if s == 1:
                w = jnp.where(right, jnp.exp(la_ref[:, hd * dk:(hd + 1) * dk]), 1.0)
            else:
                w = jnp.exp(-jnp.abs(b - _left_block_end_rows(b_ref, s)))
            z = (jnp.where(right, q, k) * w).astype(BF16)
            scores = jnp.where(level == lv, _dot_nt(z, z), scores)

        btot = b_ref[C - 1:C, :]
        st = st_ref[hd]
        o = _dot(scores.astype(BF16), v) + _dot_nt((q * jnp.exp(b)).astype(BF16), st.astype(BF16))
        kt = (k * jnp.exp(btot - b)).astype(BF16)
        st_ref[hd] = st * jnp.exp(btot) + _dot_tn(v, kt)
        on = _rms(o, hn_ref[hd:hd + 1, :]) * sg.astype(F32)
        obuf_ref[:, hd * dv:(hd + 1) * dv] = on.astype(BF16)


def _gla_core(p, la, h, head_norm, w_out, *, batch, seq, casts=()):
    rows, d = h.shape
    n = p.shape[1]
    nh, dv = head_norm.shape
    hk = la.shape[1]
    dk = hk // nh
    per_step = GLA_CHUNKS_PER_STEP
    C = per_step * GLA_CHUNK
    grid, (p_spec, la_spec, h_spec), o_spec = _recurrence_specs(C, (n, hk, d), d, batch=batch, seq=seq)
    (out,), cast = _call(
        functools.partial(_gla_core_kernel, dk=dk, dv=dv), name="gla_core", grid=grid,
        in_specs=[p_spec, la_spec, h_spec, _const_spec((nh, dv)), _const_spec((nh * dv, d))],
        args=[p, la, h, head_norm, w_out],
        out_specs=[o_spec], out_shapes=[jax.ShapeDtypeStruct((rows, d), F32)],
        aliases={2: 0},
        scratch_shapes=[
            pltpu.VMEM((nh, dv, dk), F32),
            pltpu.VMEM((nh, dv, dk), F32),
            pltpu.VMEM((per_step, GLA_CHUNK, hk), F32),
            pltpu.VMEM((C, nh * dv), BF16),
        ], casts=casts)
    return out, cast


def kernel(x, meta_tokens, norm_ffn1, ffn1_w_in, ffn1_w_out, norm_mix, norm_ffn2, ffn2_w_in, ffn2_w_out, ret_w_in, ret_head_norm, ret_w_out, gla_w_in, gla_w_gate, gla_b_gate, gla_head_norm, gla_w_out, final_norm):
    batch, seq, d = x.shape
    depth = norm_ffn1.shape[0]
    tm = ROW_TILE
    assert meta_tokens.shape == (N_META, d) and seq % tm == 0 and tm >= LEAD and tm % ROW_PASS == 0

    tail = jnp.zeros((tm, d), x.dtype).at[LEAD_ZERO:LEAD].set(meta_tokens.astype(x.dtype))
    h = x.reshape(batch * seq, d)

    half = ret_w_in.shape[2] // 6 // RET_HEADS // 2
    assert half == LANES
    inv = 1.0 / (ROPE_BASE ** jnp.linspace(0.0, 1.0, half, dtype=F32))

    gla_w_in_t = jnp.swapaxes(gla_w_in, 1, 2)

    stages = []
    for i in range(depth):
        j = i // 2
        stages.append([(ffn1_w_in, i), (ffn1_w_out, i)])
        if i % 2 == 0:
            stages.append([(ret_w_in, j), (ret_w_out, j)])
        else:
            stages += [[(gla_w_in_t, j, "transposed"), (gla_w_gate, j)], [(gla_w_out, j)]]
        stages.append([(ffn2_w_in, i), (ffn2_w_out, i)])
    stages.append([])
    nxt = iter(stages[1:])

    w = _cast_weights(stages[0])
    for i in range(depth):
        j = i // 2
        h, w = _ffn(h, norm_ffn1[i], *w, tm=tm, tail=tail if i == 0 else None, casts=next(nxt))
        if i % 2 == 0:
            h, w = _ret_mixer(h, norm_mix[i], *w, ret_head_norm[j], inv, batch=batch, seq=seq,
                              casts=next(nxt))
        else:
            p, la, w = _gla_proj(h, norm_mix[i], *w, gla_b_gate[j], tm=tm, casts=next(nxt))
            h, w = _gla_core(p, la, h, gla_head_norm[j], *w, batch=batch, seq=seq, casts=next(nxt))
        h, w = _ffn(h, norm_ffn2[i], *w, tm=tm, final_w=final_norm if i == depth - 1 else None,
                    casts=next(nxt))
    return h.reshape(batch, seq, d)
```

```python
import functools
import math

import jax
import jax.numpy as jnp
from jax import lax
from jax.experimental import pallas as pl
from jax.experimental.pallas import tpu as pltpu

F32 = jnp.float32
BF16 = jnp.bfloat16

EPS = 1e-6
N_META = 16
LEAD = 256
LEAD_ZERO = LEAD - N_META
ROW_TILE = 1024
ROW_PASS = 512
ROPE_BASE = 10000.0
RET_HEADS = 4
GLA_HEADS = 4
GLA_RANK = 16
GLA_TAU = 16.0
RET_CHUNK = 256
RET_CHUNKS_PER_STEP = 2
GLA_CHUNK = 128
GLA_CHUNKS_PER_STEP = 4
LANES = 128
BF16_SUBLANES = 16
VMEM_LIMIT = 56 * 1024 * 1024

RET_LOG_GAMMA = tuple(math.log1p(-2.0 ** (-5.0 - h)) for h in range(RET_HEADS))


def _const_spec(shape):
    nd = len(shape)
    return pl.BlockSpec(shape, lambda *_: (0,) * nd, pipeline_mode=pl.Buffered(1))


def _rms(x, w):
    ms = jnp.mean(x * x, axis=-1, keepdims=True)
    return x * lax.rsqrt(ms + EPS) * w


def _silu(x):
    return x * jax.nn.sigmoid(x)


def _dot(a, b):
    return jnp.dot(a, b, preferred_element_type=F32)


def _dot_nt(a, b):
    return lax.dot_general(a, b, (((1,), (1,)), ((), ())), preferred_element_type=F32)


def _dot_tn(a, b):
    return lax.dot_general(a, b, (((0,), (0,)), ((), ())), preferred_element_type=F32)


def _cast_blocks(rows, n_steps):
    units = rows // BF16_SUBLANES
    assert rows % BF16_SUBLANES == 0
    return max(n for n in range(1, min(units, n_steps) + 1) if units % n == 0)


def _call(body, *, name, grid, in_specs, args, out_specs, out_shapes, scratch_shapes=(), aliases=None,
          casts=()):
    n_in, n_out, n_cast = len(args), len(out_shapes), len(casts)
    n_steps = math.prod(grid)
    linear = (lambda i: i) if len(grid) == 1 else (lambda b, c: b * grid[1] + c)
    in_specs, out_specs, out_shapes, args = list(in_specs), list(out_specs), list(out_shapes), list(args)
    transposed = []
    for stack, layer, *flags in casts:
        transposed.append(bool(flags))
        if flags:
            _, cols, rows = stack.shape
            cb = 2 * LANES
            nb = pl.cdiv(cols, cb)
            assert nb <= n_steps
            blk = lambda *g, nb=nb: jnp.minimum(linear(*g), nb - 1)
            in_specs.append(pl.BlockSpec((None, cb, rows), lambda *g, blk=blk, layer=layer: (layer, blk(*g), 0)))
            out_specs.append(pl.BlockSpec((rows, cb), lambda *g, blk=blk: (0, blk(*g))))
        else:
            _, rows, cols = stack.shape
            nb = _cast_blocks(rows, n_steps)
            rb = rows // nb
            blk = lambda *g, nb=nb: jnp.minimum(linear(*g), nb - 1)
            in_specs.append(pl.BlockSpec((None, rb, cols), lambda *g, blk=blk, layer=layer: (layer, blk(*g), 0)))
            out_specs.append(pl.BlockSpec((rb, cols), lambda *g, blk=blk: (blk(*g), 0)))
        out_shapes.append(jax.ShapeDtypeStruct((rows, cols), BF16))
        args.append(stack)

    def kern(*refs):
        ins, rest = refs[:n_in], refs[n_in:]
        cast_in, rest = rest[:n_cast], rest[n_cast:]
        outs, rest = rest[:n_out], rest[n_out:]
        cast_out, scratch = rest[:n_cast], rest[n_cast:]
        for src, dst, t in zip(cast_in, cast_out, transposed):
            dst[...] = (src[...].T if t else src[...]).astype(BF16)
        body(*ins, *outs, *scratch)

    res = pl.pallas_call(
        kern, grid=grid, in_specs=in_specs, out_specs=out_specs, out_shape=out_shapes,
        scratch_shapes=list(scratch_shapes), input_output_aliases=aliases or {},
        compiler_params=pltpu.CompilerParams(
            dimension_semantics=("arbitrary",) * len(grid), vmem_limit_bytes=VMEM_LIMIT),
        name=name,
    )(*args)
    return res[:n_out], res[n_out:]


def _cast_weights(casts, *, n_steps=8):
    _, out = _call(lambda: None, name="cast_weights", grid=(n_steps,), in_specs=[], args=[],
                   out_specs=[], out_shapes=[], casts=casts)
    return out


def _ffn_kernel(x_ref, *rest, n_x_tiles, first, final):
    rest = list(rest)
    meta_ref = rest.pop(0) if first else None
    nw_ref, wg_ref, wu_ref, wo_ref = rest[:4]
    fw_ref = rest[4] if final else None
    o_ref = rest[-1]
    i = pl.program_id(0)

    def ffn(x):
        xn = _rms(x, nw_ref[...]).astype(BF16)
        g = _dot(xn, wg_ref[...])
        u = _dot(xn, wu_ref[...])
        hid = (_silu(g) * u).astype(BF16)
        y = x + 0.5 * _dot(hid, wo_ref[...])
        return _rms(y, fw_ref[...]) if final else y

    @pl.when(i < n_x_tiles)
    def _():
        for lo in range(0, x_ref.shape[0], ROW_PASS):
            o_ref[lo:lo + ROW_PASS, :] = ffn(x_ref[lo:lo + ROW_PASS, :])

    if not final:
        @pl.when(i == n_x_tiles)
        def _():
            o_ref[...] = jnp.zeros_like(o_ref)
            o_ref[LEAD_ZERO:LEAD, :] = ffn(meta_ref[...] if first else x_ref[LEAD_ZERO:LEAD, :])


def _ffn(h, norm_w, w_in, w_out, *, tm, meta=None, final_w=None, casts=()):
    d = h.shape[1]
    dff = w_out.shape[0]
    first, final = meta is not None, final_w is not None
    n_x_tiles = (h.shape[0] if first else h.shape[0] - tm) // tm
    n_tiles = n_x_tiles if final else n_x_tiles + 1
    in_specs = [pl.BlockSpec((tm, d), lambda i: (jnp.minimum(i, n_x_tiles - 1), 0) if first else (i, 0))]
    args = [h]
    if first:
        in_specs.append(_const_spec((N_META, d)))
        args.append(meta)
    in_specs += [
        _const_spec((1, d)),
        pl.BlockSpec((d, dff), lambda i: (0, 0), pipeline_mode=pl.Buffered(1)),
        pl.BlockSpec((d, dff), lambda i: (0, 1), pipeline_mode=pl.Buffered(1)),
        _const_spec((dff, d)),
    ]
    args += [norm_w.reshape(1, d), w_in, w_in, w_out]
    if final:
        in_specs.append(_const_spec((1, d)))
        args.append(final_w.reshape(1, d))
    (out,), cast = _call(
        functools.partial(_ffn_kernel, n_x_tiles=n_x_tiles, first=first, final=final),
        name="ffn_first" if first else "ffn_final" if final else "ffn",
        grid=(n_tiles,), in_specs=in_specs, args=args,
        out_specs=[pl.BlockSpec((tm, d), lambda i: (i, 0))],
        out_shapes=[jax.ShapeDtypeStruct((n_tiles * tm, d), F32)], casts=casts)
    return out, cast


def _chunk_block(b, c, *, chunks_per_batch, lead_block):
    first = jnp.where(b == 0, lead_block, b * chunks_per_batch)
    return jnp.where(c == 0, first, b * chunks_per_batch + c - 1)


def _recurrence_specs(C, widths, d, *, batch, seq):
    cpb = seq // C
    assert seq % C == 0 and LEAD_ZERO // C == (LEAD - 1) // C
    lead_block = batch * cpb + LEAD_ZERO // C
    idx = lambda b, c: (_chunk_block(b, c, chunks_per_batch=cpb, lead_block=lead_block), 0)
    return (batch, cpb + 1), [pl.BlockSpec((C, w), idx) for w in widths], pl.BlockSpec((C, d), idx)


def _ret_mixer_kernel(h_ref, nw_ref, w_ref, hn_ref, wo_ref, o_ref,
                      s_ref, slead_ref, dec_ref, dq_ref, dk_ref, cosr_ref, sinr_ref, obuf_ref, inv_ref,
                      *, dk, dv):
    b, c = pl.program_id(0), pl.program_id(1)
    C = RET_CHUNK
    block_rows = h_ref.shape[0]
    nh = RET_HEADS
    half = dk // 2
    k0, v0, g0 = nh * dk, 2 * nh * dk, 2 * nh * dk + nh * dv

    @pl.when((b == 0) & (c == 0))
    def _():
        s_ref[...] = jnp.zeros_like(s_ref)
        row = lax.broadcasted_iota(jnp.int32, (C, C), 0)
        col = lax.broadcasted_iota(jnp.int32, (C, C), 1)
        rel = (row - col).astype(F32)
        rowl = lax.broadcasted_iota(jnp.int32, (C, half), 0).astype(F32)
        lane = lax.broadcasted_iota(jnp.int32, (1, half), 1).astype(F32)
        inv_ref[...] = jnp.exp(lane * (-math.log(ROPE_BASE) / (half - 1)))
        ang = rowl * inv_ref[...]
        cosr_ref[...] = jnp.cos(ang)
        sinr_ref[...] = jnp.sin(ang)
        for hd in range(nh):
            lg = RET_LOG_GAMMA[hd]
            dec_ref[hd] = jnp.where(rel >= 0, jnp.exp(lg * jnp.maximum(rel, 0.0)), 0.0)
            dq_ref[hd] = jnp.exp(lg * (rowl + 1.0))
            dk_ref[hd] = jnp.exp(lg * (C - 1.0 - rowl))

    def chunk(rows, n, first_pos):
        ang0 = jnp.asarray(first_pos, F32) * inv_ref[...]
        cb, sb = jnp.cos(ang0), jnp.sin(ang0)
        cos = cb * cosr_ref[:n] - sb * sinr_ref[:n]
        sin = sb * cosr_ref[:n] + cb * sinr_ref[:n]

        def rotary(y):
            t1, t2 = y[:, :half], y[:, half:]
            return jnp.concatenate([t1 * cos - t2 * sin, t1 * sin + t2 * cos], axis=1)

        x = h_ref[rows, :]
        xn = _rms(x, nw_ref[...]).astype(BF16)
        for hd in range(nh):
            q = rotary(_dot(xn, w_ref[:, hd * dk:(hd + 1) * dk]))
            k = rotary(_dot(xn, w_ref[:, k0 + hd * dk:k0 + (hd + 1) * dk])) * dk ** -0.5
            v = _dot(xn, w_ref[:, v0 + hd * dv:v0 + (hd + 1) * dv]).astype(BF16)
            g = _dot(xn, w_ref[:, g0 + hd * dv:g0 + (hd + 1) * dv])
            qd = (q * jnp.concatenate([dq_ref[hd, :n]] * 2, axis=1)).astype(BF16)
            kd = (k * jnp.concatenate([dk_ref[hd, C - n:]] * 2, axis=1)).astype(BF16)
            s = (_dot_nt(q.astype(BF16), k.astype(BF16)) * dec_ref[hd, :n, :n]).astype(BF16)
            st = s_ref[hd]
            o = _dot(s, v) + _dot(qd, st.astype(BF16))
            s_ref[hd] = st * math.exp(RET_LOG_GAMMA[hd] * n) + _dot_tn(kd, v)
            on = _rms(o, hn_ref[hd:hd + 1, :]) * _silu(g)
            obuf_ref[:n, hd * dv:(hd + 1) * dv] = on.astype(BF16)
        o_ref[rows, :] = x + _dot(obuf_ref[:n], wo_ref[...])

    @pl.when((b == 0) & (c == 0))
    def _():
        o_ref[...] = jnp.zeros_like(o_ref)
        lead = slice(LEAD_ZERO % block_rows, LEAD_ZERO % block_rows + N_META)
        chunk(lead, N_META, 0)
        slead_ref[...] = s_ref[...]

    @pl.when((b > 0) & (c == 0))
    def _():
        s_ref[...] = slead_ref[...]

    @pl.when(c > 0)
    def _():
        for lo in range(0, block_rows, C):
            chunk(slice(lo, lo + C), C, (c - 1) * block_rows + lo + N_META)


def _ret_mixer(h, norm_w, w_in, w_out, head_norm, *, batch, seq, casts=()):
    rows, d = h.shape
    n = w_in.shape[1]
    nh, dv = head_norm.shape
    dk = (n - 2 * nh * dv) // (2 * nh)
    assert dk == 2 * LANES
    C = RET_CHUNK
    grid, (h_spec,), o_spec = _recurrence_specs(RET_CHUNKS_PER_STEP * C, (d,), d, batch=batch, seq=seq)
    (out,), cast = _call(
        functools.partial(_ret_mixer_kernel, dk=dk, dv=dv), name="ret_mixer", grid=grid,
        in_specs=[h_spec, _const_spec((1, d)), _const_spec((d, n)), _const_spec((nh, dv)),
                  _const_spec((nh * dv, d))],
        args=[h, norm_w.reshape(1, d), w_in, head_norm, w_out],
        out_specs=[o_spec], out_shapes=[jax.ShapeDtypeStruct((rows, d), F32)],
        aliases={0: 0},
        scratch_shapes=[
            pltpu.VMEM((nh, dk, dv), F32),
            pltpu.VMEM((nh, dk, dv), F32),
            pltpu.VMEM((nh, C, C), F32),
            pltpu.VMEM((nh, C, dk // 2), F32),
            pltpu.VMEM((nh, C, dk // 2), F32),
            pltpu.VMEM((C, dk // 2), F32),
            pltpu.VMEM((C, dk // 2), F32),
            pltpu.VMEM((C, nh * dv), BF16),
            pltpu.VMEM((1, dk // 2), F32),
        ], casts=casts)
    return out, cast


def _gla_proj_kernel(h_ref, nw_ref, w_ref, wg_ref, bg_ref, o_ref, la_ref, *, n_x_tiles, n_main):
    i = pl.program_id(0)
    hk = wg_ref.shape[1]
    dk = hk // GLA_HEADS
    step = 512
    assert n_main // step >= GLA_HEADS

    def project(rows):
        hn = _rms(h_ref[rows, :], nw_ref[...]).astype(BF16)
        z = _dot(hn, w_ref[:, n_main:n_main + GLA_RANK])
        xg = _dot(z.astype(BF16), wg_ref[...]) + bg_ref[...]
        for n, lo in enumerate(range(0, n_main, step)):
            y = _dot(hn, w_ref[:, lo:lo + step])
            if lo < hk:
                y = y * dk ** -0.5
            elif lo >= n_main - (n_main - 2 * hk) // 2:
                y = _silu(y)
            o_ref[rows, lo:lo + step] = y.astype(BF16)
            if n < GLA_HEADS:
                xh = xg[:, n * dk:(n + 1) * dk]
                ls = jnp.minimum(xh, 0.0) - jnp.log(1.0 + jnp.exp(-jnp.abs(xh)))
                la_ref[rows, n * dk:(n + 1) * dk] = ls * (1.0 / GLA_TAU)

    @pl.when(i < n_x_tiles)
    def _():
        for lo in range(0, h_ref.shape[0], ROW_PASS):
            project(slice(lo, lo + ROW_PASS))

    @pl.when(i == n_x_tiles)
    def _():
        o_ref[...] = jnp.zeros_like(o_ref)
        la_ref[...] = jnp.zeros_like(la_ref)
        project(slice(LEAD_ZERO, LEAD))


def _gla_proj(h, norm_w, w_in, w_gate, b_gate, *, tm, casts=()):
    rows, d = h.shape
    n = w_in.shape[1]
    n_main = n - GLA_RANK
    hk = w_gate.shape[1]
    (p, la), cast = _call(
        functools.partial(_gla_proj_kernel, n_x_tiles=rows // tm - 1, n_main=n_main),
        name="gla_proj", grid=(rows // tm,),
        in_specs=[pl.BlockSpec((tm, d), lambda i: (i, 0)), _const_spec((1, d)), _const_spec((d, n)),
                  _const_spec((GLA_RANK, hk)), _const_spec((1, hk))],
        args=[h, norm_w.reshape(1, d), w_in, w_gate, b_gate.reshape(1, hk)],
        out_specs=[pl.BlockSpec((tm, n_main), lambda i: (i, 0)), pl.BlockSpec((tm, hk), lambda i: (i, 0))],
        out_shapes=[jax.ShapeDtypeStruct((rows, n_main), BF16), jax.ShapeDtypeStruct((rows, hk), F32)],
        casts=casts)
    return p, la, cast


def _gla_core_kernel(p_ref, la_ref, h_ref, hn_ref, wo_ref, o_ref,
                     st_ref, stlead_ref, ball_ref, obuf_ref, *, dk, dv):
    bi, c = pl.program_id(0), pl.program_id(1)

    @pl.when((bi == 0) & (c == 0))
    def _():
        st_ref[...] = jnp.zeros_like(st_ref)

    def chunk(start, n, slot):
        rows = pl.ds(start, n)
        _gla_chunk(p_ref.at[rows, :], la_ref.at[rows, :], hn_ref, st_ref, ball_ref.at[slot, pl.ds(0, n), :],
                   obuf_ref.at[rows, :], dk=dk, dv=dv)

    @pl.when((bi == 0) & (c == 0))
    def _():
        o_ref[...] = jnp.zeros_like(o_ref)
        lead = LEAD_ZERO % p_ref.shape[0]
        chunk(lead, N_META, 0)
        o_ref[lead:lead + N_META, :] = (h_ref[lead:lead + N_META, :]
                                        + _dot(obuf_ref[lead:lead + N_META, :], wo_ref[...]))
        stlead_ref[...] = st_ref[...]

    @pl.when((bi > 0) & (c == 0))
    def _():
        st_ref[...] = stlead_ref[...]

    @pl.when(c > 0)
    def _():
        for n in range(p_ref.shape[0] // GLA_CHUNK):
            chunk(n * GLA_CHUNK, GLA_CHUNK, n)
        o_ref[...] = h_ref[...] + _dot(obuf_ref[...], wo_ref[...])


def _left_block_end_rows(b_ref, s):
    C, dk = b_ref.shape
    sub = 8
    bcast = lambda e, n: jnp.broadcast_to(b_ref[e:e + 1, :], (n, dk))
    if 2 * s >= sub:
        n = max(2 * s, sub)
        return jnp.concatenate([bcast(e, n) for e in range(s - 1, C, n)], axis=0)
    r = lax.broadcasted_iota(jnp.int32, (sub, dk), 0)
    tiles = []
    for t0 in range(0, C, sub):
        tile = bcast(t0 + s - 1, sub)
        for blk in range(2 * s, sub, 2 * s):
            tile = jnp.where(r >= blk, bcast(t0 + blk + s - 1, sub), tile)
        tiles.append(tile)
    return jnp.concatenate(tiles, axis=0)


def _gla_chunk(p_ref, la_ref, hn_ref, st_ref, ball_ref, obuf_ref, *, dk, dv):
    C = p_ref.shape[0]
    nh = GLA_HEADS
    k0, v0, g0 = nh * dk, 2 * nh * dk, 2 * nh * dk + nh * dv

    rowl = lax.broadcasted_iota(jnp.int32, (C, dk), 0)
    row = lax.broadcasted_iota(jnp.int32, (C, C), 0)
    col = lax.broadcasted_iota(jnp.int32, (C, C), 1)

    differ = jnp.bitwise_xor(row, col)
    level = jnp.full((C, C), -1, jnp.int32)
    for lv in range(C.bit_length() - 1):
        level = jnp.where((row > col) & (differ >= (1 << lv)), lv, level)

    a_all = la_ref[...]
    a1 = a_all.astype(BF16)
    r1 = a_all - a1.astype(F32)
    a2 = r1.astype(BF16)
    a3 = (r1 - a2.astype(F32)).astype(BF16)
    tri = (row >= col).astype(BF16)
    ball_ref[...] = _dot(tri, a1) + _dot(tri, a2) + _dot(tri, a3)

    for hd in range(nh):
        qb = p_ref[:, hd * dk:(hd + 1) * dk]
        kb = p_ref[:, k0 + hd * dk:k0 + (hd + 1) * dk]
        q = qb.astype(F32)
        k = kb.astype(F32)
        v = p_ref[:, v0 + hd * dv:v0 + (hd + 1) * dv]
        sg = p_ref[:, g0 + hd * dv:g0 + (hd + 1) * dv]
        b_ref = ball_ref.at[:, hd * dk:(hd + 1) * dk]
        b = b_ref[...]

        scores = jnp.where(col == row, _dot_nt(qb, kb), 0.0)
        for lv in range(C.bit_length() - 1):
            s = 1 << lv
            right = (rowl & s) != 0
            if s == 1:
                w = jnp.where(right, jnp.exp(la_ref[:, hd * dk:(hd + 1) * dk]), 1.0)
            else:
                w = jnp.exp(-jnp.abs(b - _left_block_end_rows(b_ref, s)))
            z = (jnp.where(right, q, k) * w).astype(BF16)
            scores = jnp.where(level == lv, _dot_nt(z, z), scores)

        btot = b_ref[C - 1:C, :]
        st = st_ref[hd]
        o = _dot(scores.astype(BF16), v) + _dot_nt((q * jnp.exp(b)).astype(BF16), st.astype(BF16))
        kt = (k * jnp.exp(btot - b)).astype(BF16)
        st_ref[hd] = st * jnp.exp(btot) + _dot_tn(v, kt)
        on = _rms(o, hn_ref[hd:hd + 1, :]) * sg.astype(F32)
        obuf_ref[:, hd * dv:(hd + 1) * dv] = on.astype(BF16)


def _gla_core(p, la, h, head_norm, w_out, *, batch, seq, casts=()):
    rows, d = h.shape
    n = p.shape[1]
    nh, dv = head_norm.shape
    hk = la.shape[1]
    dk = hk // nh
    per_step = GLA_CHUNKS_PER_STEP
    C = per_step * GLA_CHUNK
    grid, (p_spec, la_spec, h_spec), o_spec = _recurrence_specs(C, (n, hk, d), d, batch=batch, seq=seq)
    (out,), cast = _call(
        functools.partial(_gla_core_kernel, dk=dk, dv=dv), name="gla_core", grid=grid,
        in_specs=[p_spec, la_spec, h_spec, _const_spec((nh, dv)), _const_spec((nh * dv, d))],
        args=[p, la, h, head_norm, w_out],
        out_specs=[o_spec], out_shapes=[jax.ShapeDtypeStruct((rows, d), F32)],
        aliases={2: 0},
        scratch_shapes=[
            pltpu.VMEM((nh, dv, dk), F32),
            pltpu.VMEM((nh, dv, dk), F32),
            pltpu.VMEM((per_step, GLA_CHUNK, hk), F32),
            pltpu.VMEM((C, nh * dv), BF16),
        ], casts=casts)
    return out, cast


def kernel(x, meta_tokens, norm_ffn1, ffn1_w_in, ffn1_w_out, norm_mix, norm_ffn2, ffn2_w_in, ffn2_w_out, ret_w_in, ret_head_norm, ret_w_out, gla_w_in, gla_w_gate, gla_b_gate, gla_head_norm, gla_w_out, final_norm):
    batch, seq, d = x.shape
    depth = norm_ffn1.shape[0]
    tm = ROW_TILE
    assert meta_tokens.shape == (N_META, d) and seq % tm == 0 and tm >= LEAD and tm % ROW_PASS == 0

    h = x.reshape(batch * seq, d)

    gla_w_in_t = jnp.swapaxes(gla_w_in, 1, 2)

    stages = []
    for i in range(depth):
        j = i // 2
        stages.append([(ffn1_w_in, i), (ffn1_w_out, i)])
        if i % 2 == 0:
            stages.append([(ret_w_in, j), (ret_w_out, j)])
        else:
            stages += [[(gla_w_in_t, j, "transposed"), (gla_w_gate, j)], [(gla_w_out, j)]]
        stages.append([(ffn2_w_in, i), (ffn2_w_out, i)])
    stages.append([])
    nxt = iter(stages[1:])

    w = _cast_weights(stages[0])
    for i in range(depth):
        j = i // 2
        h, w = _ffn(h, norm_ffn1[i], *w, tm=tm, meta=meta_tokens.astype(x.dtype) if i == 0 else None,
                    casts=next(nxt))
        if i % 2 == 0:
            h, w = _ret_mixer(h, norm_mix[i], *w, ret_head_norm[j], batch=batch, seq=seq, casts=next(nxt))
        else:
            p, la, w = _gla_proj(h, norm_mix[i], *w, gla_b_gate[j], tm=tm, casts=next(nxt))
            h, w = _gla_core(p, la, h, gla_head_norm[j], *w, batch=batch, seq=seq, casts=next(nxt))
        h, w = _ffn(h, norm_ffn2[i], *w, tm=tm, final_w=final_norm if i == depth - 1 else None,
                    casts=next(nxt))
    return h.reshape(batch, seq, d)
```

```python
import functools
import math

import jax
import jax.numpy as jnp
from jax import lax
from jax.experimental import pallas as pl
from jax.experimental.pallas import tpu as pltpu

F32 = jnp.float32
BF16 = jnp.bfloat16

EPS = 1e-6
N_META = 16
LEAD = 256
LEAD_ZERO = LEAD - N_META
ROW_TILE = 1024
ROW_PASS = 512
ROPE_BASE = 10000.0
RET_HEADS = 4
GLA_HEADS = 4
GLA_RANK = 16
GLA_TAU = 16.0
RET_CHUNK = 256
RET_CHUNKS_PER_STEP = 2
GLA_CHUNK = 128
GLA_CHUNKS_PER_STEP = 4
LANES = 128
BF16_SUBLANES = 16
VMEM_LIMIT = 56 * 1024 * 1024

RET_LOG_GAMMA = tuple(math.log1p(-2.0 ** (-5.0 - h)) for h in range(RET_HEADS))


def _const_spec(shape):
    nd = len(shape)
    return pl.BlockSpec(shape, lambda *_: (0,) * nd, pipeline_mode=pl.Buffered(1))


def _rms(x, w):
    ms = jnp.mean(x * x, axis=-1, keepdims=True)
    return x * lax.rsqrt(ms + EPS) * w


def _silu(x):
    return x * jax.nn.sigmoid(x)


def _dot(a, b):
    return jnp.dot(a, b, preferred_element_type=F32)


def _dot_nt(a, b):
    return lax.dot_general(a, b, (((1,), (1,)), ((), ())), preferred_element_type=F32)


def _dot_tn(a, b):
    return lax.dot_general(a, b, (((0,), (0,)), ((), ())), preferred_element_type=F32)


def _cast_blocks(rows, n_steps):
    units = rows // BF16_SUBLANES
    assert rows % BF16_SUBLANES == 0
    return max(n for n in range(1, min(units, n_steps) + 1) if units % n == 0)


def _call(body, *, name, grid, in_specs, args, out_specs, out_shapes, scratch_shapes=(), aliases=None,
          casts=()):
    n_in, n_out, n_cast = len(args), len(out_shapes), len(casts)
    n_steps = math.prod(grid)
    linear = (lambda i: i) if len(grid) == 1 else (lambda b, c: b * grid[1] + c)
    in_specs, out_specs, out_shapes, args = list(in_specs), list(out_specs), list(out_shapes), list(args)
    transposed = []
    for stack, layer, *flags in casts:
        transposed.append(bool(flags))
        if flags:
            _, cols, rows = stack.shape
            cb = 2 * LANES
            nb = pl.cdiv(cols, cb)
            assert nb <= n_steps
            blk = lambda *g, nb=nb: jnp.minimum(linear(*g), nb - 1)
            in_specs.append(pl.BlockSpec((None, cb, rows), lambda *g, blk=blk, layer=layer: (layer, blk(*g), 0)))
            out_specs.append(pl.BlockSpec((rows, cb), lambda *g, blk=blk: (0, blk(*g))))
        else:
            _, rows, cols = stack.shape
            nb = _cast_blocks(rows, n_steps)
            rb = rows // nb
            blk = lambda *g, nb=nb: jnp.minimum(linear(*g), nb - 1)
            in_specs.append(pl.BlockSpec((None, rb, cols), lambda *g, blk=blk, layer=layer: (layer, blk(*g), 0)))
            out_specs.append(pl.BlockSpec((rb, cols), lambda *g, blk=blk: (blk(*g), 0)))
        out_shapes.append(jax.ShapeDtypeStruct((rows, cols), BF16))
        args.append(stack)

    def kern(*refs):
        ins, rest = refs[:n_in], refs[n_in:]
        cast_in, rest = rest[:n_cast], rest[n_cast:]
        outs, rest = rest[:n_out], rest[n_out:]
        cast_out, scratch = rest[:n_cast], rest[n_cast:]
        for src, dst, t in zip(cast_in, cast_out, transposed):
            dst[...] = (src[...].T if t else src[...]).astype(BF16)
        body(*ins, *outs, *scratch)

    res = pl.pallas_call(
        kern, grid=grid, in_specs=in_specs, out_specs=out_specs, out_shape=out_shapes,
        scratch_shapes=list(scratch_shapes), input_output_aliases=aliases or {},
        compiler_params=pltpu.CompilerParams(
            dimension_semantics=("arbitrary",) * len(grid), vmem_limit_bytes=VMEM_LIMIT),
        name=name,
    )(*args)
    return res[:n_out], res[n_out:]


def _cast_weights(casts, *, n_steps=8):
    _, out = _call(lambda: None, name="cast_weights", grid=(n_steps,), in_specs=[], args=[],
                   out_specs=[], out_shapes=[], casts=casts)
    return out


def _ffn_kernel(x_ref, *rest, n_x_tiles, layer, first, final):
    rest = list(rest)
    meta_ref = rest.pop(0) if first else None
    nw_ref, wg_ref, wu_ref, wo_ref = rest[:4]
    fw_ref = rest[4] if final else None
    o_ref = rest[-1]
    i = pl.program_id(0)

    def ffn(x):
        xn = _rms(x, nw_ref[layer:layer + 1, :]).astype(BF16)
        g = _dot(xn, wg_ref[...])
        u = _dot(xn, wu_ref[...])
        hid = (_silu(g) * u).astype(BF16)
        y = x + 0.5 * _dot(hid, wo_ref[...])
        return _rms(y, fw_ref[...]) if final else y

    @pl.when(i < n_x_tiles)
    def _():
        for lo in range(0, x_ref.shape[0], ROW_PASS):
            o_ref[lo:lo + ROW_PASS, :] = ffn(x_ref[lo:lo + ROW_PASS, :])

    if not final:
        @pl.when(i == n_x_tiles)
        def _():
            o_ref[...] = jnp.zeros_like(o_ref)
            o_ref[LEAD_ZERO:LEAD, :] = ffn(meta_ref[...] if first else x_ref[LEAD_ZERO:LEAD, :])


def _ffn(h, norm, layer, w_in, w_out, *, tm, meta=None, final_w=None, casts=()):
    d = h.shape[1]
    dff = w_out.shape[0]
    first, final = meta is not None, final_w is not None
    n_x_tiles = (h.shape[0] if first else h.shape[0] - tm) // tm
    n_tiles = n_x_tiles if final else n_x_tiles + 1
    in_specs = [pl.BlockSpec((tm, d), lambda i: (jnp.minimum(i, n_x_tiles - 1), 0) if first else (i, 0))]
    args = [h]
    if first:
        in_specs.append(_const_spec((N_META, d)))
        args.append(meta)
    in_specs += [
        _const_spec(norm.shape),
        pl.BlockSpec((d, dff), lambda i: (0, 0), pipeline_mode=pl.Buffered(1)),
        pl.BlockSpec((d, dff), lambda i: (0, 1), pipeline_mode=pl.Buffered(1)),
        _const_spec((dff, d)),
    ]
    args += [norm, w_in, w_in, w_out]
    if final:
        in_specs.append(_const_spec((1, d)))
        args.append(final_w.reshape(1, d))
    (out,), cast = _call(
        functools.partial(_ffn_kernel, n_x_tiles=n_x_tiles, layer=layer, first=first, final=final),
        name="ffn_first" if first else "ffn_final" if final else "ffn",
        grid=(n_tiles,), in_specs=in_specs, args=args,
        out_specs=[pl.BlockSpec((tm, d), lambda i: (i, 0))],
        out_shapes=[jax.ShapeDtypeStruct((n_tiles * tm, d), F32)], casts=casts)
    return out, cast


def _chunk_block(b, c, *, chunks_per_batch, lead_block):
    first = jnp.where(b == 0, lead_block, b * chunks_per_batch)
    return jnp.where(c == 0, first, b * chunks_per_batch + c - 1)


def _recurrence_specs(C, widths, d, *, batch, seq):
    cpb = seq // C
    assert seq % C == 0 and LEAD_ZERO // C == (LEAD - 1) // C
    lead_block = batch * cpb + LEAD_ZERO // C
    idx = lambda b, c: (_chunk_block(b, c, chunks_per_batch=cpb, lead_block=lead_block), 0)
    return (batch, cpb + 1), [pl.BlockSpec((C, w), idx) for w in widths], pl.BlockSpec((C, d), idx)


def _ret_mixer_kernel(h_ref, nw_ref, w_ref, hn_ref, wo_ref, o_ref,
                      s_ref, slead_ref, dec_ref, dq_ref, dk_ref, cosr_ref, sinr_ref, obuf_ref, inv_ref,
                      *, layer, dk, dv):
    b, c = pl.program_id(0), pl.program_id(1)
    C = RET_CHUNK
    block_rows = h_ref.shape[0]
    nh = RET_HEADS
    half = dk // 2
    k0, v0, g0 = nh * dk, 2 * nh * dk, 2 * nh * dk + nh * dv

    @pl.when((b == 0) & (c == 0))
    def _():
        s_ref[...] = jnp.zeros_like(s_ref)
        row = lax.broadcasted_iota(jnp.int32, (C, C), 0)
        col = lax.broadcasted_iota(jnp.int32, (C, C), 1)
        rel = (row - col).astype(F32)
        rowl = lax.broadcasted_iota(jnp.int32, (C, half), 0).astype(F32)
        lane = lax.broadcasted_iota(jnp.int32, (1, half), 1).astype(F32)
        inv_ref[...] = jnp.exp(lane * (-math.log(ROPE_BASE) / (half - 1)))
        ang = rowl * inv_ref[...]
        cosr_ref[...] = jnp.cos(ang)
        sinr_ref[...] = jnp.sin(ang)
        for hd in range(nh):
            lg = RET_LOG_GAMMA[hd]
            dec_ref[hd] = jnp.where(rel >= 0, jnp.exp(lg * jnp.maximum(rel, 0.0)), 0.0)
            dq_ref[hd] = jnp.exp(lg * (rowl + 1.0))
            dk_ref[hd] = jnp.exp(lg * (C - 1.0 - rowl))

    def chunk(rows, n, first_pos):
        ang0 = jnp.asarray(first_pos, F32) * inv_ref[...]
        cb, sb = jnp.cos(ang0), jnp.sin(ang0)
        cos = cb * cosr_ref[:n] - sb * sinr_ref[:n]
        sin = sb * cosr_ref[:n] + cb * sinr_ref[:n]

        def rotary(y):
            t1, t2 = y[:, :half], y[:, half:]
            return jnp.concatenate([t1 * cos - t2 * sin, t1 * sin + t2 * cos], axis=1)

        x = h_ref[rows, :]
        xn = _rms(x, nw_ref[layer:layer + 1, :]).astype(BF16)
        for hd in range(nh):
            q = rotary(_dot(xn, w_ref[:, hd * dk:(hd + 1) * dk]))
            k = rotary(_dot(xn, w_ref[:, k0 + hd * dk:k0 + (hd + 1) * dk])) * dk ** -0.5
            v = _dot(xn, w_ref[:, v0 + hd * dv:v0 + (hd + 1) * dv]).astype(BF16)
            g = _dot(xn, w_ref[:, g0 + hd * dv:g0 + (hd + 1) * dv])
            qd = (q * jnp.concatenate([dq_ref[hd, :n]] * 2, axis=1)).astype(BF16)
            kd = (k * jnp.concatenate([dk_ref[hd, C - n:]] * 2, axis=1)).astype(BF16)
            s = (_dot_nt(q.astype(BF16), k.astype(BF16)) * dec_ref[hd, :n, :n]).astype(BF16)
            st = s_ref[hd]
            o = _dot(s, v) + _dot(qd, st.astype(BF16))
            s_ref[hd] = st * math.exp(RET_LOG_GAMMA[hd] * n) + _dot_tn(kd, v)
            on = _rms(o, hn_ref[hd:hd + 1, :]) * _silu(g)
            obuf_ref[:n, hd * dv:(hd + 1) * dv] = on.astype(BF16)
        o_ref[rows, :] = x + _dot(obuf_ref[:n], wo_ref[...])

    @pl.when((b == 0) & (c == 0))
    def _():
        o_ref[...] = jnp.zeros_like(o_ref)
        lead = slice(LEAD_ZERO % block_rows, LEAD_ZERO % block_rows + N_META)
        chunk(lead, N_META, 0)
        slead_ref[...] = s_ref[...]

    @pl.when((b > 0) & (c == 0))
    def _():
        s_ref[...] = slead_ref[...]

    @pl.when(c > 0)
    def _():
        for lo in range(0, block_rows, C):
            chunk(slice(lo, lo + C), C, (c - 1) * block_rows + lo + N_META)


def _ret_mixer(h, norm, layer, w_in, w_out, head_norm, *, batch, seq, casts=()):
    rows, d = h.shape
    n = w_in.shape[1]
    nh, dv = head_norm.shape
    dk = (n - 2 * nh * dv) // (2 * nh)
    assert dk == 2 * LANES
    C = RET_CHUNK
    grid, (h_spec,), o_spec = _recurrence_specs(RET_CHUNKS_PER_STEP * C, (d,), d, batch=batch, seq=seq)
    (out,), cast = _call(
        functools.partial(_ret_mixer_kernel, layer=layer, dk=dk, dv=dv), name="ret_mixer", grid=grid,
        in_specs=[h_spec, _const_spec(norm.shape), _const_spec((d, n)), _const_spec((nh, dv)),
                  _const_spec((nh * dv, d))],
        args=[h, norm, w_in, head_norm, w_out],
        out_specs=[o_spec], out_shapes=[jax.ShapeDtypeStruct((rows, d), F32)],
        aliases={0: 0},
        scratch_shapes=[
            pltpu.VMEM((nh, dk, dv), F32),
            pltpu.VMEM((nh, dk, dv), F32),
            pltpu.VMEM((nh, C, C), F32),
            pltpu.VMEM((nh, C, dk // 2), F32),
            pltpu.VMEM((nh, C, dk // 2), F32),
            pltpu.VMEM((C, dk // 2), F32),
            pltpu.VMEM((C, dk // 2), F32),
            pltpu.VMEM((C, nh * dv), BF16),
            pltpu.VMEM((1, dk // 2), F32),
        ], casts=casts)
    return out, cast


def _gla_proj_kernel(h_ref, nw_ref, w_ref, wg_ref, bg_ref, o_ref, la_ref, *, n_x_tiles, layer, n_main):
    i = pl.program_id(0)
    hk = wg_ref.shape[1]
    dk = hk // GLA_HEADS
    step = 512
    assert n_main // step >= GLA_HEADS

    def project(rows):
        hn = _rms(h_ref[rows, :], nw_ref[layer:layer + 1, :]).astype(BF16)
        z = _dot(hn, w_ref[:, n_main:n_main + GLA_RANK])
        xg = _dot(z.astype(BF16), wg_ref[...]) + bg_ref[...]
        for n, lo in enumerate(range(0, n_main, step)):
            y = _dot(hn, w_ref[:, lo:lo + step])
            if lo < hk:
                y = y * dk ** -0.5
            elif lo >= n_main - (n_main - 2 * hk) // 2:
                y = _silu(y)
            o_ref[rows, lo:lo + step] = y.astype(BF16)
            if n < GLA_HEADS:
                xh = xg[:, n * dk:(n + 1) * dk]
                ls = jnp.minimum(xh, 0.0) - jnp.log(1.0 + jnp.exp(-jnp.abs(xh)))
                la_ref[rows, n * dk:(n + 1) * dk] = ls * (1.0 / GLA_TAU)

    @pl.when(i < n_x_tiles)
    def _():
        for lo in range(0, h_ref.shape[0], ROW_PASS):
            project(slice(lo, lo + ROW_PASS))

    @pl.when(i == n_x_tiles)
    def _():
        o_ref[...] = jnp.zeros_like(o_ref)
        la_ref[...] = jnp.zeros_like(la_ref)
        project(slice(LEAD_ZERO, LEAD))


def _gla_proj(h, norm, layer, w_in, w_gate, b_gate, *, tm, casts=()):
    rows, d = h.shape
    n = w_in.shape[1]
    n_main = n - GLA_RANK
    hk = w_gate.shape[1]
    (p, la), cast = _call(
        functools.partial(_gla_proj_kernel, n_x_tiles=rows // tm - 1, layer=layer, n_main=n_main),
        name="gla_proj", grid=(rows // tm,),
        in_specs=[pl.BlockSpec((tm, d), lambda i: (i, 0)), _const_spec(norm.shape), _const_spec((d, n)),
                  _const_spec((GLA_RANK, hk)), _const_spec((1, hk))],
        args=[h, norm, w_in, w_gate, b_gate.reshape(1, hk)],
        out_specs=[pl.BlockSpec((tm, n_main), lambda i: (i, 0)), pl.BlockSpec((tm, hk), lambda i: (i, 0))],
        out_shapes=[jax.ShapeDtypeStruct((rows, n_main), BF16), jax.ShapeDtypeStruct((rows, hk), F32)],
        casts=casts)
    return p, la, cast


def _gla_core_kernel(p_ref, la_ref, h_ref, hn_ref, wo_ref, o_ref,
                     st_ref, stlead_ref, ball_ref, obuf_ref, *, dk, dv):
    bi, c = pl.program_id(0), pl.program_id(1)

    @pl.when((bi == 0) & (c == 0))
    def _():
        st_ref[...] = jnp.zeros_like(st_ref)

    def chunk(start, n, slot):
        rows = pl.ds(start, n)
        _gla_chunk(p_ref.at[rows, :], la_ref.at[rows, :], hn_ref, st_ref, ball_ref.at[slot, pl.ds(0, n), :],
                   obuf_ref.at[rows, :], dk=dk, dv=dv)

    @pl.when((bi == 0) & (c == 0))
    def _():
        o_ref[...] = jnp.zeros_like(o_ref)
        lead = LEAD_ZERO % p_ref.shape[0]
        chunk(lead, N_META, 0)
        o_ref[lead:lead + N_META, :] = (h_ref[lead:lead + N_META, :]
                                        + _dot(obuf_ref[lead:lead + N_META, :], wo_ref[...]))
        stlead_ref[...] = st_ref[...]

    @pl.when((bi > 0) & (c == 0))
    def _():
        st_ref[...] = stlead_ref[...]

    @pl.when(c > 0)
    def _():
        for n in range(p_ref.shape[0] // GLA_CHUNK):
            chunk(n * GLA_CHUNK, GLA_CHUNK, n)
        o_ref[...] = h_ref[...] + _dot(obuf_ref[...], wo_ref[...])


def _left_block_end_rows(b_ref, s):
    C, dk = b_ref.shape
    sub = 8
    bcast = lambda e, n: jnp.broadcast_to(b_ref[e:e + 1, :], (n, dk))
    if 2 * s >= sub:
        n = max(2 * s, sub)
        return jnp.concatenate([bcast(e, n) for e in range(s - 1, C, n)], axis=0)
    r = lax.broadcasted_iota(jnp.int32, (sub, dk), 0)
    tiles = []
    for t0 in range(0, C, sub):
        tile = bcast(t0 + s - 1, sub)
        for blk in range(2 * s, sub, 2 * s):
            tile = jnp.where(r >= blk, bcast(t0 + blk + s - 1, sub), tile)
        tiles.append(tile)
    return jnp.concatenate(tiles, axis=0)


def _gla_chunk(p_ref, la_ref, hn_ref, st_ref, ball_ref, obuf_ref, *, dk, dv):
    C = p_ref.shape[0]
    nh = GLA_HEADS
    k0, v0, g0 = nh * dk, 2 * nh * dk, 2 * nh * dk + nh * dv

    rowl = lax.broadcasted_iota(jnp.int32, (C, dk), 0)
    row = lax.broadcasted_iota(jnp.int32, (C, C), 0)
    col = lax.broadcasted_iota(jnp.int32, (C, C), 1)

    differ = jnp.bitwise_xor(row, col)
    level = jnp.full((C, C), -1, jnp.int32)
    for lv in range(C.bit_length() - 1):
        level = jnp.where((row > col) & (differ >= (1 << lv)), lv, level)

    a_all = la_ref[...]
    a1 = a_all.astype(BF16)
    r1 = a_all - a1.astype(F32)
    a2 = r1.astype(BF16)
    a3 = (r1 - a2.astype(F32)).astype(BF16)
    tri = (row >= col).astype(BF16)
    ball_ref[...] = _dot(tri, a1) + _dot(tri, a2) + _dot(tri, a3)

    for hd in range(nh):
        qb = p_ref[:, hd * dk:(hd + 1) * dk]
        kb = p_ref[:, k0 + hd * dk:k0 + (hd + 1) * dk]
        q = qb.astype(F32)
        k = kb.astype(F32)
        v = p_ref[:, v0 + hd * dv:v0 + (hd + 1) * dv]
        sg = p_ref[:, g0 + hd * dv:g0 + (hd + 1) * dv]
        b_ref = ball_ref.at[:, hd * dk:(hd + 1) * dk]
        b = b_ref[...]

        scores = jnp.where(col == row, _dot_nt(qb, kb), 0.0)
        for lv in range(C.bit_length() - 1):
            s = 1 << lv
            right = (rowl & s) != 0
            if s == 1:
                w = jnp.where(right, jnp.exp(la_ref[:, hd * dk:(hd + 1) * dk]), 1.0)
            else:
                w = jnp.exp(-jnp.abs(b - _left_block_end_rows(b_ref, s)))
            z = (jnp.where(right, q, k) * w).astype(BF16)
            scores = jnp.where(level == lv, _dot_nt(z, z), scores)

        btot = b_ref[C - 1:C, :]
        st = st_ref[hd]
        o = _dot(scores.astype(BF16), v) + _dot_nt((q * jnp.exp(b)).astype(BF16), st.astype(BF16))
        kt = (k * jnp.exp(btot - b)).astype(BF16)
        st_ref[hd] = st * jnp.exp(btot) + _dot_tn(v, kt)
        on = _rms(o, hn_ref[hd:hd + 1, :]) * sg.astype(F32)
        obuf_ref[:, hd * dv:(hd + 1) * dv] = on.astype(BF16)


def _gla_core(p, la, h, head_norm, w_out, *, batch, seq, casts=()):
    rows, d = h.shape
    n = p.shape[1]
    nh, dv = head_norm.shape
    hk = la.shape[1]
    dk = hk // nh
    per_step = GLA_CHUNKS_PER_STEP
    C = per_step * GLA_CHUNK
    grid, (p_spec, la_spec, h_spec), o_spec = _recurrence_specs(C, (n, hk, d), d, batch=batch, seq=seq)
    (out,), cast = _call(
        functools.partial(_gla_core_kernel, dk=dk, dv=dv), name="gla_core", grid=grid,
        in_specs=[p_spec, la_spec, h_spec, _const_spec((nh, dv)), _const_spec((nh * dv, d))],
        args=[p, la, h, head_norm, w_out],
        out_specs=[o_spec], out_shapes=[jax.ShapeDtypeStruct((rows, d), F32)],
        aliases={2: 0},
        scratch_shapes=[
            pltpu.VMEM((nh, dv, dk), F32),
            pltpu.VMEM((nh, dv, dk), F32),
            pltpu.VMEM((per_step, GLA_CHUNK, hk), F32),
            pltpu.VMEM((C, nh * dv), BF16),
        ], casts=casts)
    return out, cast


def kernel(x, meta_tokens, norm_ffn1, ffn1_w_in, ffn1_w_out, norm_mix, norm_ffn2, ffn2_w_in, ffn2_w_out, ret_w_in, ret_head_norm, ret_w_out, gla_w_in, gla_w_gate, gla_b_gate, gla_head_norm, gla_w_out, final_norm):
    batch, seq, d = x.shape
    depth = norm_ffn1.shape[0]
    tm = ROW_TILE
    assert meta_tokens.shape == (N_META, d) and seq % tm == 0 and tm >= LEAD and tm % ROW_PASS == 0

    h = x.reshape(batch * seq, d)

    gla_w_in_t = jnp.swapaxes(gla_w_in, 1, 2)

    stages = []
    for i in range(depth):
        j = i // 2
        stages.append([(ffn1_w_in, i), (ffn1_w_out, i)])
        if i % 2 == 0:
            stages.append([(ret_w_in, j), (ret_w_out, j)])
        else:
            stages += [[(gla_w_in_t, j, "transposed"), (gla_w_gate, j)], [(gla_w_out, j)]]
        stages.append([(ffn2_w_in, i), (ffn2_w_out, i)])
    stages.append([])
    nxt = iter(stages[1:])

    w = _cast_weights(stages[0])
    for i in range(depth):
        j = i // 2
        h, w = _ffn(h, norm_ffn1, i, *w, tm=tm, meta=meta_tokens.astype(x.dtype) if i == 0 else None,
                    casts=next(nxt))
        if i % 2 == 0:
            h, w = _ret_mixer(h, norm_mix, i, *w, ret_head_norm[j], batch=batch, seq=seq, casts=next(nxt))
        else:
            p, la, w = _gla_proj(h, norm_mix, i, *w, gla_b_gate[j], tm=tm, casts=next(nxt))
            h, w = _gla_core(p, la, h, gla_head_norm[j], *w, batch=batch, seq=seq, casts=next(nxt))
        h, w = _ffn(h, norm_ffn2, i, *w, tm=tm, final_w=final_norm if i == depth - 1 else None,
                    casts=next(nxt))
    return h.reshape(batch, seq, d)
```

```python
import functools
import math

import jax
import jax.numpy as jnp
from jax import lax
from jax.experimental import pallas as pl
from jax.experimental.pallas import tpu as pltpu

F32 = jnp.float32
BF16 = jnp.bfloat16

EPS = 1e-6
N_META = 16
LEAD = 256
LEAD_ZERO = LEAD - N_META
ROW_TILE = 1024
ROW_PASS = 512
ROPE_BASE = 10000.0
RET_HEADS = 4
GLA_HEADS = 4
GLA_RANK = 16
GLA_TAU = 16.0
RET_CHUNK = 256
RET_CHUNKS_PER_STEP = 2
GLA_CHUNK = 128
GLA_CHUNKS_PER_STEP = 8
LANES = 128
BF16_SUBLANES = 16
VMEM_LIMIT = 56 * 1024 * 1024

RET_LOG_GAMMA = tuple(math.log1p(-2.0 ** (-5.0 - h)) for h in range(RET_HEADS))


def _const_spec(shape):
    nd = len(shape)
    return pl.BlockSpec(shape, lambda *_: (0,) * nd, pipeline_mode=pl.Buffered(1))


def _rms(x, w):
    ms = jnp.mean(x * x, axis=-1, keepdims=True)
    return x * lax.rsqrt(ms + EPS) * w


def _silu(x):
    return x * jax.nn.sigmoid(x)


def _dot(a, b):
    return jnp.dot(a, b, preferred_element_type=F32)


def _dot_nt(a, b):
    return lax.dot_general(a, b, (((1,), (1,)), ((), ())), preferred_element_type=F32)


def _dot_tn(a, b):
    return lax.dot_general(a, b, (((0,), (0,)), ((), ())), preferred_element_type=F32)


def _cast_blocks(rows, n_steps):
    units = rows // BF16_SUBLANES
    assert rows % BF16_SUBLANES == 0
    return max(n for n in range(1, min(units, n_steps) + 1) if units % n == 0)


def _call(body, *, name, grid, in_specs, args, out_specs, out_shapes, scratch_shapes=(), aliases=None,
          casts=()):
    n_in, n_out, n_cast = len(args), len(out_shapes), len(casts)
    n_steps = math.prod(grid)
    linear = (lambda i: i) if len(grid) == 1 else (lambda b, c: b * grid[1] + c)
    in_specs, out_specs, out_shapes, args = list(in_specs), list(out_specs), list(out_shapes), list(args)
    transposed = []
    for stack, layer, *flags in casts:
        transposed.append(bool(flags))
        if flags:
            _, cols, rows = stack.shape
            cb = 2 * LANES
            nb = pl.cdiv(cols, cb)
            assert nb <= n_steps
            blk = lambda *g, nb=nb: jnp.minimum(linear(*g), nb - 1)
            in_specs.append(pl.BlockSpec((None, cb, rows), lambda *g, blk=blk, layer=layer: (layer, blk(*g), 0)))
            out_specs.append(pl.BlockSpec((rows, cb), lambda *g, blk=blk: (0, blk(*g))))
        else:
            _, rows, cols = stack.shape
            nb = _cast_blocks(rows, n_steps)
            rb = rows // nb
            blk = lambda *g, nb=nb: jnp.minimum(linear(*g), nb - 1)
            in_specs.append(pl.BlockSpec((None, rb, cols), lambda *g, blk=blk, layer=layer: (layer, blk(*g), 0)))
            out_specs.append(pl.BlockSpec((rb, cols), lambda *g, blk=blk: (blk(*g), 0)))
        out_shapes.append(jax.ShapeDtypeStruct((rows, cols), BF16))
        args.append(stack)

    def kern(*refs):
        ins, rest = refs[:n_in], refs[n_in:]
        cast_in, rest = rest[:n_cast], rest[n_cast:]
        outs, rest = rest[:n_out], rest[n_out:]
        cast_out, scratch = rest[:n_cast], rest[n_cast:]
        for src, dst, t in zip(cast_in, cast_out, transposed):
            dst[...] = (src[...].T if t else src[...]).astype(BF16)
        body(*ins, *outs, *scratch)

    res = pl.pallas_call(
        kern, grid=grid, in_specs=in_specs, out_specs=out_specs, out_shape=out_shapes,
        scratch_shapes=list(scratch_shapes), input_output_aliases=aliases or {},
        compiler_params=pltpu.CompilerParams(
            dimension_semantics=("arbitrary",) * len(grid), vmem_limit_bytes=VMEM_LIMIT),
        name=name,
    )(*args)
    return res[:n_out], res[n_out:]


def _cast_weights(casts, *, n_steps=8):
    _, out = _call(lambda: None, name="cast_weights", grid=(n_steps,), in_specs=[], args=[],
                   out_specs=[], out_shapes=[], casts=casts)
    return out


def _ffn_kernel(x_ref, *rest, n_x_tiles, layer, first, final):
    rest = list(rest)
    meta_ref = rest.pop(0) if first else None
    nw_ref, wg_ref, wu_ref, wo_ref = rest[:4]
    fw_ref = rest[4] if final else None
    o_ref = rest[-1]
    i = pl.program_id(0)

    def ffn(x):
        xn = _rms(x, nw_ref[layer:layer + 1, :]).astype(BF16)
        g = _dot(xn, wg_ref[...])
        u = _dot(xn, wu_ref[...])
        hid = (_silu(g) * u).astype(BF16)
        y = x + 0.5 * _dot(hid, wo_ref[...])
        return _rms(y, fw_ref[...]) if final else y

    @pl.when(i < n_x_tiles)
    def _():
        for lo in range(0, x_ref.shape[0], ROW_PASS):
            o_ref[lo:lo + ROW_PASS, :] = ffn(x_ref[lo:lo + ROW_PASS, :])

    if not final:
        @pl.when(i == n_x_tiles)
        def _():
            o_ref[...] = jnp.zeros_like(o_ref)
            o_ref[LEAD_ZERO:LEAD, :] = ffn(meta_ref[...] if first else x_ref[LEAD_ZERO:LEAD, :])


def _ffn(h, norm, layer, w_in, w_out, *, tm, meta=None, final_w=None, casts=()):
    d = h.shape[1]
    dff = w_out.shape[0]
    first, final = meta is not None, final_w is not None
    n_x_tiles = (h.shape[0] if first else h.shape[0] - tm) // tm
    n_tiles = n_x_tiles if final else n_x_tiles + 1
    in_specs = [pl.BlockSpec((tm, d), lambda i: (jnp.minimum(i, n_x_tiles - 1), 0) if first else (i, 0))]
    args = [h]
    if first:
        in_specs.append(_const_spec((N_META, d)))
        args.append(meta)
    in_specs += [
        _const_spec(norm.shape),
        pl.BlockSpec((d, dff), lambda i: (0, 0), pipeline_mode=pl.Buffered(1)),
        pl.BlockSpec((d, dff), lambda i: (0, 1), pipeline_mode=pl.Buffered(1)),
        _const_spec((dff, d)),
    ]
    args += [norm, w_in, w_in, w_out]
    if final:
        in_specs.append(_const_spec((1, d)))
        args.append(final_w.reshape(1, d))
    (out,), cast = _call(
        functools.partial(_ffn_kernel, n_x_tiles=n_x_tiles, layer=layer, first=first, final=final),
        name="ffn_first" if first else "ffn_final" if final else "ffn",
        grid=(n_tiles,), in_specs=in_specs, args=args,
        out_specs=[pl.BlockSpec((tm, d), lambda i: (i, 0))],
        out_shapes=[jax.ShapeDtypeStruct((n_tiles * tm, d), F32)], casts=casts)
    return out, cast


def _chunk_block(b, c, *, chunks_per_batch, lead_block):
    first = jnp.where(b == 0, lead_block, b * chunks_per_batch)
    return jnp.where(c == 0, first, b * chunks_per_batch + c - 1)


def _recurrence_specs(C, widths, d, *, batch, seq):
    cpb = seq // C
    assert seq % C == 0 and LEAD_ZERO // C == (LEAD - 1) // C
    lead_block = batch * cpb + LEAD_ZERO // C
    idx = lambda b, c: (_chunk_block(b, c, chunks_per_batch=cpb, lead_block=lead_block), 0)
    return (batch, cpb + 1), [pl.BlockSpec((C, w), idx) for w in widths], pl.BlockSpec((C, d), idx)


def _ret_mixer_kernel(h_ref, nw_ref, w_ref, hn_ref, wo_ref, o_ref,
                      s_ref, slead_ref, dec_ref, dq_ref, dk_ref, cosr_ref, sinr_ref, obuf_ref, inv_ref,
                      *, layer, dk, dv):
    b, c = pl.program_id(0), pl.program_id(1)
    C = RET_CHUNK
    block_rows = h_ref.shape[0]
    nh = RET_HEADS
    half = dk // 2
    k0, v0, g0 = nh * dk, 2 * nh * dk, 2 * nh * dk + nh * dv

    @pl.when((b == 0) & (c == 0))
    def _():
        s_ref[...] = jnp.zeros_like(s_ref)
        row = lax.broadcasted_iota(jnp.int32, (C, C), 0)
        col = lax.broadcasted_iota(jnp.int32, (C, C), 1)
        rel = (row - col).astype(F32)
        rowl = lax.broadcasted_iota(jnp.int32, (C, half), 0).astype(F32)
        lane = lax.broadcasted_iota(jnp.int32, (1, half), 1).astype(F32)
        inv_ref[...] = jnp.exp(lane * (-math.log(ROPE_BASE) / (half - 1)))
        ang = rowl * inv_ref[...]
        cosr_ref[...] = jnp.cos(ang)
        sinr_ref[...] = jnp.sin(ang)
        for hd in range(nh):
            lg = RET_LOG_GAMMA[hd]
            dec_ref[hd] = jnp.where(rel >= 0, jnp.exp(lg * jnp.maximum(rel, 0.0)), 0.0)
            dq_ref[hd] = jnp.exp(lg * (rowl + 1.0))
            dk_ref[hd] = jnp.exp(lg * (C - 1.0 - rowl))

    def chunk(rows, n, first_pos):
        ang0 = jnp.asarray(first_pos, F32) * inv_ref[...]
        cb, sb = jnp.cos(ang0), jnp.sin(ang0)
        cos = cb * cosr_ref[:n] - sb * sinr_ref[:n]
        sin = sb * cosr_ref[:n] + cb * sinr_ref[:n]

        def rotary(y):
            t1, t2 = y[:, :half], y[:, half:]
            return jnp.concatenate([t1 * cos - t2 * sin, t1 * sin + t2 * cos], axis=1)

        x = h_ref[rows, :]
        xn = _rms(x, nw_ref[layer:layer + 1, :]).astype(BF16)
        for hd in range(nh):
            q = rotary(_dot(xn, w_ref[:, hd * dk:(hd + 1) * dk]))
            k = rotary(_dot(xn, w_ref[:, k0 + hd * dk:k0 + (hd + 1) * dk])) * dk ** -0.5
            v = _dot(xn, w_ref[:, v0 + hd * dv:v0 + (hd + 1) * dv]).astype(BF16)
            g = _dot(xn, w_ref[:, g0 + hd * dv:g0 + (hd + 1) * dv])
            qd = (q * jnp.concatenate([dq_ref[hd, :n]] * 2, axis=1)).astype(BF16)
            kd = (k * jnp.concatenate([dk_ref[hd, C - n:]] * 2, axis=1)).astype(BF16)
            s = (_dot_nt(q.astype(BF16), k.astype(BF16)) * dec_ref[hd, :n, :n]).astype(BF16)
            st = s_ref[hd]
            o = _dot(s, v) + _dot(qd, st.astype(BF16))
            s_ref[hd] = st * math.exp(RET_LOG_GAMMA[hd] * n) + _dot_tn(kd, v)
            on = _rms(o, hn_ref[hd:hd + 1, :]) * _silu(g)
            obuf_ref[:n, hd * dv:(hd + 1) * dv] = on.astype(BF16)
        o_ref[rows, :] = x + _dot(obuf_ref[:n], wo_ref[...])

    @pl.when((b == 0) & (c == 0))
    def _():
        o_ref[...] = jnp.zeros_like(o_ref)
        lead = slice(LEAD_ZERO % block_rows, LEAD_ZERO % block_rows + N_META)
        chunk(lead, N_META, 0)
        slead_ref[...] = s_ref[...]

    @pl.when((b > 0) & (c == 0))
    def _():
        s_ref[...] = slead_ref[...]

    @pl.when(c > 0)
    def _():
        for lo in range(0, block_rows, C):
            chunk(slice(lo, lo + C), C, (c - 1) * block_rows + lo + N_META)


def _ret_mixer(h, norm, layer, w_in, w_out, head_norm, *, batch, seq, casts=()):
    rows, d = h.shape
    n = w_in.shape[1]
    nh, dv = head_norm.shape
    dk = (n - 2 * nh * dv) // (2 * nh)
    assert dk == 2 * LANES
    C = RET_CHUNK
    grid, (h_spec,), o_spec = _recurrence_specs(RET_CHUNKS_PER_STEP * C, (d,), d, batch=batch, seq=seq)
    (out,), cast = _call(
        functools.partial(_ret_mixer_kernel, layer=layer, dk=dk, dv=dv), name="ret_mixer", grid=grid,
        in_specs=[h_spec, _const_spec(norm.shape), _const_spec((d, n)), _const_spec((nh, dv)),
                  _const_spec((nh * dv, d))],
        args=[h, norm, w_in, head_norm, w_out],
        out_specs=[o_spec], out_shapes=[jax.ShapeDtypeStruct((rows, d), F32)],
        aliases={0: 0},
        scratch_shapes=[
            pltpu.VMEM((nh, dk, dv), F32),
            pltpu.VMEM((nh, dk, dv), F32),
            pltpu.VMEM((nh, C, C), F32),
            pltpu.VMEM((nh, C, dk // 2), F32),
            pltpu.VMEM((nh, C, dk // 2), F32),
            pltpu.VMEM((C, dk // 2), F32),
            pltpu.VMEM((C, dk // 2), F32),
            pltpu.VMEM((C, nh * dv), BF16),
            pltpu.VMEM((1, dk // 2), F32),
        ], casts=casts)
    return out, cast


def _gla_proj_kernel(h_ref, nw_ref, w_ref, wg_ref, bg_ref, o_ref, la_ref, *, n_x_tiles, layer, n_main):
    i = pl.program_id(0)
    hk = wg_ref.shape[1]
    dk = hk // GLA_HEADS
    step = 512
    assert n_main // step >= GLA_HEADS

    def project(rows):
        hn = _rms(h_ref[rows, :], nw_ref[layer:layer + 1, :]).astype(BF16)
        z = _dot(hn, w_ref[:, n_main:n_main + GLA_RANK])
        xg = _dot(z.astype(BF16), wg_ref[...]) + bg_ref[...]
        for n, lo in enumerate(range(0, n_main, step)):
            y = _dot(hn, w_ref[:, lo:lo + step])
            if lo < hk:
                y = y * dk ** -0.5
            elif lo >= n_main - (n_main - 2 * hk) // 2:
                y = _silu(y)
            o_ref[rows, lo:lo + step] = y.astype(BF16)
            if n < GLA_HEADS:
                xh = xg[:, n * dk:(n + 1) * dk]
                ls = jnp.minimum(xh, 0.0) - jnp.log(1.0 + jnp.exp(-jnp.abs(xh)))
                la_ref[rows, n * dk:(n + 1) * dk] = ls * (1.0 / GLA_TAU)

    @pl.when(i < n_x_tiles)
    def _():
        for lo in range(0, h_ref.shape[0], ROW_PASS):
            project(slice(lo, lo + ROW_PASS))

    @pl.when(i == n_x_tiles)
    def _():
        o_ref[...] = jnp.zeros_like(o_ref)
        la_ref[...] = jnp.zeros_like(la_ref)
        project(slice(LEAD_ZERO, LEAD))


def _gla_proj(h, norm, layer, w_in, w_gate, b_gate, *, tm, casts=()):
    rows, d = h.shape
    n = w_in.shape[1]
    n_main = n - GLA_RANK
    hk = w_gate.shape[1]
    (p, la), cast = _call(
        functools.partial(_gla_proj_kernel, n_x_tiles=rows // tm - 1, layer=layer, n_main=n_main),
        name="gla_proj", grid=(rows // tm,),
        in_specs=[pl.BlockSpec((tm, d), lambda i: (i, 0)), _const_spec(norm.shape), _const_spec((d, n)),
                  _const_spec((GLA_RANK, hk)), _const_spec((1, hk))],
        args=[h, norm, w_in, w_gate, b_gate.reshape(1, hk)],
        out_specs=[pl.BlockSpec((tm, n_main), lambda i: (i, 0)), pl.BlockSpec((tm, hk), lambda i: (i, 0))],
        out_shapes=[jax.ShapeDtypeStruct((rows, n_main), BF16), jax.ShapeDtypeStruct((rows, hk), F32)],
        casts=casts)
    return p, la, cast


def _gla_core_kernel(p_ref, la_ref, h_ref, hn_ref, wo_ref, o_ref,
                     st_ref, stlead_ref, ball_ref, obuf_ref, *, dk, dv):
    bi, c = pl.program_id(0), pl.program_id(1)

    @pl.when((bi == 0) & (c == 0))
    def _():
        st_ref[...] = jnp.zeros_like(st_ref)

    def chunk(start, n, slot):
        rows = pl.ds(start, n)
        _gla_chunk(p_ref.at[rows, :], la_ref.at[rows, :], hn_ref, st_ref, ball_ref.at[slot, pl.ds(0, n), :],
                   obuf_ref.at[rows, :], dk=dk, dv=dv)

    @pl.when((bi == 0) & (c == 0))
    def _():
        o_ref[...] = jnp.zeros_like(o_ref)
        lead = LEAD_ZERO % p_ref.shape[0]
        chunk(lead, N_META, 0)
        o_ref[lead:lead + N_META, :] = (h_ref[lead:lead + N_META, :]
                                        + _dot(obuf_ref[lead:lead + N_META, :], wo_ref[...]))
        stlead_ref[...] = st_ref[...]

    @pl.when((bi > 0) & (c == 0))
    def _():
        st_ref[...] = stlead_ref[...]

    @pl.when(c > 0)
    def _():
        for n in range(p_ref.shape[0] // GLA_CHUNK):
            chunk(n * GLA_CHUNK, GLA_CHUNK, n)
        o_ref[...] = h_ref[...] + _dot(obuf_ref[...], wo_ref[...])


def _left_block_end_rows(b_ref, s):
    C, dk = b_ref.shape
    sub = 8
    bcast = lambda e, n: jnp.broadcast_to(b_ref[e:e + 1, :], (n, dk))
    if 2 * s >= sub:
        n = max(2 * s, sub)
        return jnp.concatenate([bcast(e, n) for e in range(s - 1, C, n)], axis=0)
    r = lax.broadcasted_iota(jnp.int32, (sub, dk), 0)
    tiles = []
    for t0 in range(0, C, sub):
        tile = bcast(t0 + s - 1, sub)
        for blk in range(2 * s, sub, 2 * s):
            tile = jnp.where(r >= blk, bcast(t0 + blk + s - 1, sub), tile)
        tiles.append(tile)
    return jnp.concatenate(tiles, axis=0)


def _gla_chunk(p_ref, la_ref, hn_ref, st_ref, ball_ref, obuf_ref, *, dk, dv):
    C = p_ref.shape[0]
    nh = GLA_HEADS
    k0, v0, g0 = nh * dk, 2 * nh * dk, 2 * nh * dk + nh * dv

    rowl = lax.broadcasted_iota(jnp.int32, (C, dk), 0)
    row = lax.broadcasted_iota(jnp.int32, (C, C), 0)
    col = lax.broadcasted_iota(jnp.int32, (C, C), 1)

    differ = jnp.bitwise_xor(row, col)
    level = jnp.full((C, C), -1, jnp.int32)
    for lv in range(C.bit_length() - 1):
        level = jnp.where((row > col) & (differ >= (1 << lv)), lv, level)

    a_all = la_ref[...]
    a1 = a_all.astype(BF16)
    r1 = a_all - a1.astype(F32)
    a2 = r1.astype(BF16)
    a3 = (r1 - a2.astype(F32)).astype(BF16)
    tri = (row >= col).astype(BF16)
    ball_ref[...] = _dot(tri, a1) + _dot(tri, a2) + _dot(tri, a3)

    for hd in range(nh):
        qb = p_ref[:, hd * dk:(hd + 1) * dk]
        kb = p_ref[:, k0 + hd * dk:k0 + (hd + 1) * dk]
        q = qb.astype(F32)
        k = kb.astype(F32)
        v = p_ref[:, v0 + hd * dv:v0 + (hd + 1) * dv]
        sg = p_ref[:, g0 + hd * dv:g0 + (hd + 1) * dv]
        b_ref = ball_ref.at[:, hd * dk:(hd + 1) * dk]
        b = b_ref[...]

        scores = jnp.where(col == row, _dot_nt(qb, kb), 0.0)
        for lv in range(C.bit_length() - 1):
            s = 1 << lv
            right = (rowl & s) != 0
            if s == 1:
                w = jnp.where(right, jnp.exp(la_ref[:, hd * dk:(hd + 1) * dk]), 1.0)
            else:
                w = jnp.exp(-jnp.abs(b - _left_block_end_rows(b_ref, s)))
            z = (jnp.where(right, q, k) * w).astype(BF16)
            scores = jnp.where(level == lv, _dot_nt(z, z), scores)

        btot = b_ref[C - 1:C, :]
        st = st_ref[hd]
        o = _dot(scores.astype(BF16), v) + _dot_nt((q * jnp.exp(b)).astype(BF16), st.astype(BF16))
        kt = (k * jnp.exp(btot - b)).astype(BF16)
        st_ref[hd] = st * jnp.exp(btot) + _dot_tn(v, kt)
        on = _rms(o, hn_ref[hd:hd + 1, :]) * sg.astype(F32)
        obuf_ref[:, hd * dv:(hd + 1) * dv] = on.astype(BF16)


def _gla_core(p, la, h, head_norm, w_out, *, batch, seq, casts=()):
    rows, d = h.shape
    n = p.shape[1]
    nh, dv = head_norm.shape
    hk = la.shape[1]
    dk = hk // nh
    per_step = GLA_CHUNKS_PER_STEP
    C = per_step * GLA_CHUNK
    grid, (p_spec, la_spec, h_spec), o_spec = _recurrence_specs(C, (n, hk, d), d, batch=batch, seq=seq)
    (out,), cast = _call(
        functools.partial(_gla_core_kernel, dk=dk, dv=dv), name="gla_core", grid=grid,
        in_specs=[p_spec, la_spec, h_spec, _const_spec((nh, dv)), _const_spec((nh * dv, d))],
        args=[p, la, h, head_norm, w_out],
        out_specs=[o_spec], out_shapes=[jax.ShapeDtypeStruct((rows, d), F32)],
        aliases={2: 0},
        scratch_shapes=[
            pltpu.VMEM((nh, dv, dk), F32),
            pltpu.VMEM((nh, dv, dk), F32),
            pltpu.VMEM((per_step, GLA_CHUNK, hk), F32),
            pltpu.VMEM((C, nh * dv), BF16),
        ], casts=casts)
    return out, cast


def kernel(x, meta_tokens, norm_ffn1, ffn1_w_in, ffn1_w_out, norm_mix, norm_ffn2, ffn2_w_in, ffn2_w_out, ret_w_in, ret_head_norm, ret_w_out, gla_w_in, gla_w_gate, gla_b_gate, gla_head_norm, gla_w_out, final_norm):
    batch, seq, d = x.shape
    depth = norm_ffn1.shape[0]
    tm = ROW_TILE
    assert meta_tokens.shape == (N_META, d) and seq % tm == 0 and tm >= LEAD and tm % ROW_PASS == 0

    h = x.reshape(batch * seq, d)

    gla_w_in_t = jnp.swapaxes(gla_w_in, 1, 2)

    stages = []
    for i in range(depth):
        j = i // 2
        stages.append([(ffn1_w_in, i), (ffn1_w_out, i)])
        if i % 2 == 0:
            stages.append([(ret_w_in, j), (ret_w_out, j)])
        else:
            stages += [[(gla_w_in_t, j, "transposed"), (gla_w_gate, j)], [(gla_w_out, j)]]
        stages.append([(ffn2_w_in, i), (ffn2_w_out, i)])
    stages.append([])
    nxt = iter(stages[1:])

    w = _cast_weights(stages[0])
    for i in range(depth):
        j = i // 2
        h, w = _ffn(h, norm_ffn1, i, *w, tm=tm, meta=meta_tokens.astype(x.dtype) if i == 0 else None,
                    casts=next(nxt))
        if i % 2 == 0:
            h, w = _ret_mixer(h, norm_mix, i, *w, ret_head_norm[j], batch=batch, seq=seq, casts=next(nxt))
        else:
            p, la, w = _gla_proj(h, norm_mix, i, *w, gla_b_gate[j], tm=tm, casts=next(nxt))
            h, w = _gla_core(p, la, h, gla_head_norm[j], *w, batch=batch, seq=seq, casts=next(nxt))
        h, w = _ffn(h, norm_ffn2, i, *w, tm=tm, final_w=final_norm if i == depth - 1 else None,
                    casts=next(nxt))
    return h.reshape(batch, seq, d)
```

```python
import functools
import math

import jax
import jax.numpy as jnp
from jax import lax
from jax.experimental import pallas as pl
from jax.experimental.pallas import tpu as pltpu

F32 = jnp.float32
BF16 = jnp.bfloat16

EPS = 1e-6
N_META = 16
LEAD = 256
LEAD_ZERO = LEAD - N_META
ROW_TILE = 1024
ROW_PASS = 512
ROPE_BASE = 10000.0
RET_HEADS = 4
GLA_HEADS = 4
GLA_RANK = 16
GLA_TAU = 16.0
RET_CHUNK = 256
RET_CHUNKS_PER_STEP = 2
GLA_CHUNK = 128
GLA_CHUNKS_PER_STEP = 4
LANES = 128
BF16_SUBLANES = 16
VMEM_LIMIT = 56 * 1024 * 1024

RET_LOG_GAMMA = tuple(math.log1p(-2.0 ** (-5.0 - h)) for h in range(RET_HEADS))


def _const_spec(shape):
    nd = len(shape)
    return pl.BlockSpec(shape, lambda *_: (0,) * nd, pipeline_mode=pl.Buffered(1))


def _rms(x, w):
    ms = jnp.mean(x * x, axis=-1, keepdims=True)
    return x * lax.rsqrt(ms + EPS) * w


def _silu(x):
    return x * jax.nn.sigmoid(x)


def _dot(a, b):
    return jnp.dot(a, b, preferred_element_type=F32)


def _dot_nt(a, b):
    return lax.dot_general(a, b, (((1,), (1,)), ((), ())), preferred_element_type=F32)


def _dot_tn(a, b):
    return lax.dot_general(a, b, (((0,), (0,)), ((), ())), preferred_element_type=F32)


def _cast_blocks(rows, n_steps):
    units = rows // BF16_SUBLANES
    assert rows % BF16_SUBLANES == 0
    return max(n for n in range(1, min(units, n_steps) + 1) if units % n == 0)


def _call(body, *, name, grid, in_specs, args, out_specs, out_shapes, scratch_shapes=(), aliases=None,
          casts=()):
    n_in, n_out, n_cast = len(args), len(out_shapes), len(casts)
    n_steps = math.prod(grid)
    linear = (lambda i: i) if len(grid) == 1 else (lambda b, c: b * grid[1] + c)
    in_specs, out_specs, out_shapes, args = list(in_specs), list(out_specs), list(out_shapes), list(args)
    transposed = []
    for stack, layer, *flags in casts:
        transposed.append(bool(flags))
        if flags:
            _, cols, rows = stack.shape
            cb = 2 * LANES
            nb = pl.cdiv(cols, cb)
            assert nb <= n_steps
            blk = lambda *g, nb=nb: jnp.minimum(linear(*g), nb - 1)
            in_specs.append(pl.BlockSpec((None, cb, rows), lambda *g, blk=blk, layer=layer: (layer, blk(*g), 0)))
            out_specs.append(pl.BlockSpec((rows, cb), lambda *g, blk=blk: (0, blk(*g))))
        else:
            _, rows, cols = stack.shape
            nb = _cast_blocks(rows, n_steps)
            rb = rows // nb
            blk = lambda *g, nb=nb: jnp.minimum(linear(*g), nb - 1)
            in_specs.append(pl.BlockSpec((None, rb, cols), lambda *g, blk=blk, layer=layer: (layer, blk(*g), 0)))
            out_specs.append(pl.BlockSpec((rb, cols), lambda *g, blk=blk: (blk(*g), 0)))
        out_shapes.append(jax.ShapeDtypeStruct((rows, cols), BF16))
        args.append(stack)

    def kern(*refs):
        ins, rest = refs[:n_in], refs[n_in:]
        cast_in, rest = rest[:n_cast], rest[n_cast:]
        outs, rest = rest[:n_out], rest[n_out:]
        cast_out, scratch = rest[:n_cast], rest[n_cast:]
        for src, dst, t in zip(cast_in, cast_out, transposed):
            dst[...] = (src[...].T if t else src[...]).astype(BF16)
        body(*ins, *outs, *scratch)

    res = pl.pallas_call(
        kern, grid=grid, in_specs=in_specs, out_specs=out_specs, out_shape=out_shapes,
        scratch_shapes=list(scratch_shapes), input_output_aliases=aliases or {},
        compiler_params=pltpu.CompilerParams(
            dimension_semantics=("arbitrary",) * len(grid), vmem_limit_bytes=VMEM_LIMIT),
        name=name,
    )(*args)
    return res[:n_out], res[n_out:]


def _cast_weights(casts, *, n_steps=8):
    _, out = _call(lambda: None, name="cast_weights", grid=(n_steps,), in_specs=[], args=[],
                   out_specs=[], out_shapes=[], casts=casts)
    return out


def _ffn_kernel(x_ref, *rest, n_x_tiles, layer, first, final):
    rest = list(rest)
    meta_ref = rest.pop(0) if first else None
    nw_ref, wg_ref, wu_ref, wo_ref = rest[:4]
    fw_ref = rest[4] if final else None
    o_ref = rest[-1]
    i = pl.program_id(0)

    def ffn(x):
        xn = _rms(x, nw_ref[layer:layer + 1, :]).astype(BF16)
        g = _dot(xn, wg_ref[...])
        u = _dot(xn, wu_ref[...])
        hid = (_silu(g) * u).astype(BF16)
        y = x + 0.5 * _dot(hid, wo_ref[...])
        return _rms(y, fw_ref[...]) if final else y

    @pl.when(i < n_x_tiles)
    def _():
        for lo in range(0, x_ref.shape[0], ROW_PASS):
            o_ref[lo:lo + ROW_PASS, :] = ffn(x_ref[lo:lo + ROW_PASS, :])

    if not final:
        @pl.when(i == n_x_tiles)
        def _():
            o_ref[...] = jnp.zeros_like(o_ref)
            o_ref[LEAD_ZERO:LEAD, :] = ffn(meta_ref[...] if first else x_ref[LEAD_ZERO:LEAD, :])


def _ffn(h, norm, layer, w_in, w_out, *, tm, meta=None, final_w=None, casts=()):
    d = h.shape[1]
    dff = w_out.shape[0]
    first, final = meta is not None, final_w is not None
    n_x_tiles = (h.shape[0] if first else h.shape[0] - tm) // tm
    n_tiles = n_x_tiles if final else n_x_tiles + 1
    in_specs = [pl.BlockSpec((tm, d), lambda i: (jnp.minimum(i, n_x_tiles - 1), 0) if first else (i, 0))]
    args = [h]
    if first:
        in_specs.append(_const_spec((N_META, d)))
        args.append(meta)
    in_specs += [
        _const_spec(norm.shape),
        pl.BlockSpec((d, dff), lambda i: (0, 0), pipeline_mode=pl.Buffered(1)),
        pl.BlockSpec((d, dff), lambda i: (0, 1), pipeline_mode=pl.Buffered(1)),
        _const_spec((dff, d)),
    ]
    args += [norm, w_in, w_in, w_out]
    if final:
        in_specs.append(_const_spec((1, d)))
        args.append(final_w.reshape(1, d))
    (out,), cast = _call(
        functools.partial(_ffn_kernel, n_x_tiles=n_x_tiles, layer=layer, first=first, final=final),
        name="ffn_first" if first else "ffn_final" if final else "ffn",
        grid=(n_tiles,), in_specs=in_specs, args=args,
        out_specs=[pl.BlockSpec((tm, d), lambda i: (i, 0))],
        out_shapes=[jax.ShapeDtypeStruct((n_tiles * tm, d), F32)], casts=casts)
    return out, cast


def _chunk_block(b, c, *, chunks_per_batch, lead_block):
    first = jnp.where(b == 0, lead_block, b * chunks_per_batch)
    return jnp.where(c == 0, first, b * chunks_per_batch + c - 1)


def _recurrence_specs(C, widths, d, *, batch, seq):
    cpb = seq // C
    assert seq % C == 0 and LEAD_ZERO // C == (LEAD - 1) // C
    lead_block = batch * cpb + LEAD_ZERO // C
    idx = lambda b, c: (_chunk_block(b, c, chunks_per_batch=cpb, lead_block=lead_block), 0)
    return (batch, cpb + 1), [pl.BlockSpec((C, w), idx) for w in widths], pl.BlockSpec((C, d), idx)


def _ret_mixer_kernel(h_ref, nw_ref, w_ref, hn_ref, wo_ref, o_ref,
                      s_ref, slead_ref, dec_ref, dq_ref, dk_ref, cosr_ref, sinr_ref, obuf_ref, inv_ref,
                      *, layer, dk, dv):
    b, c = pl.program_id(0), pl.program_id(1)
    C = RET_CHUNK
    block_rows = h_ref.shape[0]
    nh = RET_HEADS
    half = dk // 2
    k0, v0, g0 = nh * dk, 2 * nh * dk, 2 * nh * dk + nh * dv

    @pl.when((b == 0) & (c == 0))
    def _():
        s_ref[...] = jnp.zeros_like(s_ref)
        row = lax.broadcasted_iota(jnp.int32, (C, C), 0)
        col = lax.broadcasted_iota(jnp.int32, (C, C), 1)
        rel = (row - col).astype(F32)
        rowl = lax.broadcasted_iota(jnp.int32, (C, half), 0).astype(F32)
        lane = lax.broadcasted_iota(jnp.int32, (1, half), 1).astype(F32)
        inv_ref[...] = jnp.exp(lane * (-math.log(ROPE_BASE) / (half - 1)))
        ang = rowl * inv_ref[...]
        cosr_ref[...] = jnp.cos(ang)
        sinr_ref[...] = jnp.sin(ang)
        for hd in range(nh):
            lg = RET_LOG_GAMMA[hd]
            dec_ref[hd] = jnp.where(rel >= 0, jnp.exp(lg * jnp.maximum(rel, 0.0)), 0.0)
            dq_ref[hd] = jnp.exp(lg * (rowl + 1.0))
            dk_ref[hd] = jnp.exp(lg * (C - 1.0 - rowl))

    def chunk(rows, n, first_pos):
        ang0 = jnp.asarray(first_pos, F32) * inv_ref[...]
        cb, sb = jnp.cos(ang0), jnp.sin(ang0)
        cos = cb * cosr_ref[:n] - sb * sinr_ref[:n]
        sin = sb * cosr_ref[:n] + cb * sinr_ref[:n]

        def rotary(y):
            t1, t2 = y[:, :half], y[:, half:]
            return jnp.concatenate([t1 * cos - t2 * sin, t1 * sin + t2 * cos], axis=1)

        x = h_ref[rows, :]
        xn = _rms(x, nw_ref[layer:layer + 1, :]).astype(BF16)
        for hd in range(nh):
            q = rotary(_dot(xn, w_ref[:, hd * dk:(hd + 1) * dk]))
            k = rotary(_dot(xn, w_ref[:, k0 + hd * dk:k0 + (hd + 1) * dk])) * dk ** -0.5
            v = _dot(xn, w_ref[:, v0 + hd * dv:v0 + (hd + 1) * dv]).astype(BF16)
            g = _dot(xn, w_ref[:, g0 + hd * dv:g0 + (hd + 1) * dv])
            qd = (q * jnp.concatenate([dq_ref[hd, :n]] * 2, axis=1)).astype(BF16)
            kd = (k * jnp.concatenate([dk_ref[hd, C - n:]] * 2, axis=1)).astype(BF16)
            s = (_dot_nt(q.astype(BF16), k.astype(BF16)) * dec_ref[hd, :n, :n]).astype(BF16)
            st = s_ref[hd]
            o = _dot(s, v) + _dot(qd, st.astype(BF16))
            s_ref[hd] = st * math.exp(RET_LOG_GAMMA[hd] * n) + _dot_tn(kd, v)
            on = _rms(o, hn_ref[hd:hd + 1, :]) * _silu(g)
            obuf_ref[:n, hd * dv:(hd + 1) * dv] = on.astype(BF16)
        o_ref[rows, :] = x + _dot(obuf_ref[:n], wo_ref[...])

    @pl.when((b == 0) & (c == 0))
    def _():
        o_ref[...] = jnp.zeros_like(o_ref)
        lead = slice(LEAD_ZERO % block_rows, LEAD_ZERO % block_rows + N_META)
        chunk(lead, N_META, 0)
        slead_ref[...] = s_ref[...]

    @pl.when((b > 0) & (c == 0))
    def _():
        s_ref[...] = slead_ref[...]

    @pl.when(c > 0)
    def _():
        for lo in range(0, block_rows, C):
            chunk(slice(lo, lo + C), C, (c - 1) * block_rows + lo + N_META)


def _ret_mixer(h, norm, layer, w_in, w_out, head_norm, *, batch, seq, casts=()):
    rows, d = h.shape
    n = w_in.shape[1]
    nh, dv = head_norm.shape
    dk = (n - 2 * nh * dv) // (2 * nh)
    assert dk == 2 * LANES
    C = RET_CHUNK
    grid, (h_spec,), o_spec = _recurrence_specs(RET_CHUNKS_PER_STEP * C, (d,), d, batch=batch, seq=seq)
    (out,), cast = _call(
        functools.partial(_ret_mixer_kernel, layer=layer, dk=dk, dv=dv), name="ret_mixer", grid=grid,
        in_specs=[h_spec, _const_spec(norm.shape), _const_spec((d, n)), _const_spec((nh, dv)),
                  _const_spec((nh * dv, d))],
        args=[h, norm, w_in, head_norm, w_out],
        out_specs=[o_spec], out_shapes=[jax.ShapeDtypeStruct((rows, d), F32)],
        aliases={0: 0},
        scratch_shapes=[
            pltpu.VMEM((nh, dk, dv), F32),
            pltpu.VMEM((nh, dk, dv), F32),
            pltpu.VMEM((nh, C, C), F32),
            pltpu.VMEM((nh, C, dk // 2), F32),
            pltpu.VMEM((nh, C, dk // 2), F32),
            pltpu.VMEM((C, dk // 2), F32),
            pltpu.VMEM((C, dk // 2), F32),
            pltpu.VMEM((C, nh * dv), BF16),
            pltpu.VMEM((1, dk // 2), F32),
        ], casts=casts)
    return out, cast


def _gla_proj_kernel(h_ref, nw_ref, w_ref, wg_ref, bg_ref, o_ref, la_ref, *, n_x_tiles, layer, n_main):
    i = pl.program_id(0)
    hk = wg_ref.shape[1]
    dk = hk // GLA_HEADS
    step = 512
    assert n_main // step >= GLA_HEADS

    def project(rows):
        hn = _rms(h_ref[rows, :], nw_ref[layer:layer + 1, :]).astype(BF16)
        n_blocks = n_main // step
        for n, lo in enumerate(range(0, n_main, step)):
            y = _dot(hn, w_ref[:, lo:lo + step])
            if lo < hk:
                y = y * dk ** -0.5
            elif lo >= n_main - (n_main - 2 * hk) // 2:
                y = _silu(y)
            o_ref[rows, lo:lo + step] = y.astype(BF16)
            if n == 0:
                z = _dot(hn, w_ref[:, n_main:n_main + GLA_RANK])
            elif n == 1:
                xg = _dot(z.astype(BF16), wg_ref[...]) + bg_ref[...]
            if n >= n_blocks - GLA_HEADS:
                hd = n - (n_blocks - GLA_HEADS)
                xh = xg[:, hd * dk:(hd + 1) * dk]
                ls = jnp.minimum(xh, 0.0) - jnp.log(1.0 + jnp.exp(-jnp.abs(xh)))
                la_ref[rows, hd * dk:(hd + 1) * dk] = ls * (1.0 / GLA_TAU)

    @pl.when(i < n_x_tiles)
    def _():
        for lo in range(0, h_ref.shape[0], ROW_PASS):
            project(slice(lo, lo + ROW_PASS))

    @pl.when(i == n_x_tiles)
    def _():
        o_ref[...] = jnp.zeros_like(o_ref)
        la_ref[...] = jnp.zeros_like(la_ref)
        project(slice(LEAD_ZERO, LEAD))


def _gla_proj(h, norm, layer, w_in, w_gate, b_gate, *, tm, casts=()):
    rows, d = h.shape
    n = w_in.shape[1]
    n_main = n - GLA_RANK
    hk = w_gate.shape[1]
    (p, la), cast = _call(
        functools.partial(_gla_proj_kernel, n_x_tiles=rows // tm - 1, layer=layer, n_main=n_main),
        name="gla_proj", grid=(rows // tm,),
        in_specs=[pl.BlockSpec((tm, d), lambda i: (i, 0)), _const_spec(norm.shape), _const_spec((d, n)),
                  _const_spec((GLA_RANK, hk)), _const_spec((1, hk))],
        args=[h, norm, w_in, w_gate, b_gate.reshape(1, hk)],
        out_specs=[pl.BlockSpec((tm, n_main), lambda i: (i, 0)), pl.BlockSpec((tm, hk), lambda i: (i, 0))],
        out_shapes=[jax.ShapeDtypeStruct((rows, n_main), BF16), jax.ShapeDtypeStruct((rows, hk), F32)],
        casts=casts)
    return p, la, cast


def _gla_core_kernel(p_ref, la_ref, h_ref, hn_ref, wo_ref, o_ref,
                     st_ref, stlead_ref, ball_ref, obuf_ref, *, dk, dv):
    bi, c = pl.program_id(0), pl.program_id(1)

    @pl.when((bi == 0) & (c == 0))
    def _():
        st_ref[...] = jnp.zeros_like(st_ref)

    def chunk(start, n, slot):
        rows = pl.ds(start, n)
        _gla_chunk(p_ref.at[rows, :], la_ref.at[rows, :], hn_ref, st_ref, ball_ref.at[slot, pl.ds(0, n), :],
                   obuf_ref.at[rows, :], dk=dk, dv=dv)

    @pl.when((bi == 0) & (c == 0))
    def _():
        o_ref[...] = jnp.zeros_like(o_ref)
        lead = LEAD_ZERO % p_ref.shape[0]
        chunk(lead, N_META, 0)
        o_ref[lead:lead + N_META, :] = (h_ref[lead:lead + N_META, :]
                                        + _dot(obuf_ref[lead:lead + N_META, :], wo_ref[...]))
        stlead_ref[...] = st_ref[...]

    @pl.when((bi > 0) & (c == 0))
    def _():
        st_ref[...] = stlead_ref[...]

    @pl.when(c > 0)
    def _():
        for n in range(p_ref.shape[0] // GLA_CHUNK):
            chunk(n * GLA_CHUNK, GLA_CHUNK, n)
        o_ref[...] = h_ref[...] + _dot(obuf_ref[...], wo_ref[...])


def _left_block_end_rows(b_ref, s):
    C, dk = b_ref.shape
    sub = 8
    bcast = lambda e, n: jnp.broadcast_to(b_ref[e:e + 1, :], (n, dk))
    if 2 * s >= sub:
        n = max(2 * s, sub)
        return jnp.concatenate([bcast(e, n) for e in range(s - 1, C, n)], axis=0)
    r = lax.broadcasted_iota(jnp.int32, (sub, dk), 0)
    tiles = []
    for t0 in range(0, C, sub):
        tile = bcast(t0 + s - 1, sub)
        for blk in range(2 * s, sub, 2 * s):
            tile = jnp.where(r >= blk, bcast(t0 + blk + s - 1, sub), tile)
        tiles.append(tile)
    return jnp.concatenate(tiles, axis=0)


def _gla_chunk(p_ref, la_ref, hn_ref, st_ref, ball_ref, obuf_ref, *, dk, dv):
    C = p_ref.shape[0]
    nh = GLA_HEADS
    k0, v0, g0 = nh * dk, 2 * nh * dk, 2 * nh * dk + nh * dv

    rowl = lax.broadcasted_iota(jnp.int32, (C, dk), 0)
    row = lax.broadcasted_iota(jnp.int32, (C, C), 0)
    col = lax.broadcasted_iota(jnp.int32, (C, C), 1)

    differ = jnp.bitwise_xor(row, col)
    level = jnp.full((C, C), -1, jnp.int32)
    for lv in range(C.bit_length() - 1):
        level = jnp.where((row > col) & (differ >= (1 << lv)), lv, level)

    a_all = la_ref[...]
    a1 = a_all.astype(BF16)
    r1 = a_all - a1.astype(F32)
    a2 = r1.astype(BF16)
    a3 = (r1 - a2.astype(F32)).astype(BF16)
    tri = (row >= col).astype(BF16)
    ball_ref[...] = _dot(tri, a1) + _dot(tri, a2) + _dot(tri, a3)

    for hd in range(nh):
        qb = p_ref[:, hd * dk:(hd + 1) * dk]
        kb = p_ref[:, k0 + hd * dk:k0 + (hd + 1) * dk]
        q = qb.astype(F32)
        k = kb.astype(F32)
        v = p_ref[:, v0 + hd * dv:v0 + (hd + 1) * dv]
        sg = p_ref[:, g0 + hd * dv:g0 + (hd + 1) * dv]
        b_ref = ball_ref.at[:, hd * dk:(hd + 1) * dk]
        b = b_ref[...]

        scores = jnp.where(col == row, _dot_nt(qb, kb), 0.0)
        for lv in range(C.bit_length() - 1):
            s = 1 << lv
            right = (rowl & s) != 0
            if s == 1:
                w = jnp.where(right, jnp.exp(la_ref[:, hd * dk:(hd + 1) * dk]), 1.0)
            else:
                w = jnp.exp(-jnp.abs(b - _left_block_end_rows(b_ref, s)))
            z = (jnp.where(right, q, k) * w).astype(BF16)
            scores = jnp.where(level == lv, _dot_nt(z, z), scores)

        btot = b_ref[C - 1:C, :]
        st = st_ref[hd]
        o = _dot(scores.astype(BF16), v) + _dot_nt((q * jnp.exp(b)).astype(BF16), st.astype(BF16))
        kt = (k * jnp.exp(btot - b)).astype(BF16)
        st_ref[hd] = st * jnp.exp(btot) + _dot_tn(v, kt)
        on = _rms(o, hn_ref[hd:hd + 1, :]) * sg.astype(F32)
        obuf_ref[:, hd * dv:(hd + 1) * dv] = on.astype(BF16)


def _gla_core(p, la, h, head_norm, w_out, *, batch, seq, casts=()):
    rows, d = h.shape
    n = p.shape[1]
    nh, dv = head_norm.shape
    hk = la.shape[1]
    dk = hk // nh
    per_step = GLA_CHUNKS_PER_STEP
    C = per_step * GLA_CHUNK
    grid, (p_spec, la_spec, h_spec), o_spec = _recurrence_specs(C, (n, hk, d), d, batch=batch, seq=seq)
    (out,), cast = _call(
        functools.partial(_gla_core_kernel, dk=dk, dv=dv), name="gla_core", grid=grid,
        in_specs=[p_spec, la_spec, h_spec, _const_spec((nh, dv)), _const_spec((nh * dv, d))],
        args=[p, la, h, head_norm, w_out],
        out_specs=[o_spec], out_shapes=[jax.ShapeDtypeStruct((rows, d), F32)],
        aliases={2: 0},
        scratch_shapes=[
            pltpu.VMEM((nh, dv, dk), F32),
            pltpu.VMEM((nh, dv, dk), F32),
            pltpu.VMEM((per_step, GLA_CHUNK, hk), F32),
            pltpu.VMEM((C, nh * dv), BF16),
        ], casts=casts)
    return out, cast


def kernel(x, meta_tokens, norm_ffn1, ffn1_w_in, ffn1_w_out, norm_mix, norm_ffn2, ffn2_w_in, ffn2_w_out, ret_w_in, ret_head_norm, ret_w_out, gla_w_in, gla_w_gate, gla_b_gate, gla_head_norm, gla_w_out, final_norm):
    batch, seq, d = x.shape
    depth = norm_ffn1.shape[0]
    tm = ROW_TILE
    assert meta_tokens.shape == (N_META, d) and seq % tm == 0 and tm >= LEAD and tm % ROW_PASS == 0

    h = x.reshape(batch * seq, d)

    gla_w_in_t = jnp.swapaxes(gla_w_in, 1, 2)

    stages = []
    for i in range(depth):
        j = i // 2
        stages.append([(ffn1_w_in, i), (ffn1_w_out, i)])
        if i % 2 == 0:
            stages.append([(ret_w_in, j), (ret_w_out, j)])
        else:
            stages += [[(gla_w_in_t, j, "transposed"), (gla_w_gate, j)], [(gla_w_out, j)]]
        stages.append([(ffn2_w_in, i), (ffn2_w_out, i)])
    stages.append([])
    nxt = iter(stages[1:])

    w = _cast_weights(stages[0])
    for i in range(depth):
        j = i // 2
        h, w = _ffn(h, norm_ffn1, i, *w, tm=tm, meta=meta_tokens.astype(x.dtype) if i == 0 else None,
                    casts=next(nxt))
        if i % 2 == 0:
            h, w = _ret_mixer(h, norm_mix, i, *w, ret_head_norm[j], batch=batch, seq=seq, casts=next(nxt))
        else:
            p, la, w = _gla_proj(h, norm_mix, i, *w, gla_b_gate[j], tm=tm, casts=next(nxt))
            h, w = _gla_core(p, la, h, gla_head_norm[j], *w, batch=batch, seq=seq, casts=next(nxt))
        h, w = _ffn(h, norm_ffn2, i, *w, tm=tm, final_w=final_norm if i == depth - 1 else None,
                    casts=next(nxt))
    return h.reshape(batch, seq, d)
```

```python
import functools
import math

import jax
import jax.numpy as jnp
from jax import lax
from jax.experimental import pallas as pl
from jax.experimental.pallas import tpu as pltpu

F32 = jnp.float32
BF16 = jnp.bfloat16

EPS = 1e-6
N_META = 16
LEAD = 256
LEAD_ZERO = LEAD - N_META
ROW_TILE = 1024
ROW_PASS = 512
ROPE_BASE = 10000.0
RET_HEADS = 4
GLA_HEADS = 4
GLA_RANK = 16
GLA_TAU = 16.0
RET_CHUNK = 256
RET_CHUNKS_PER_STEP = 2
GLA_CHUNK = 128
GLA_CHUNKS_PER_STEP = 4
GLA_HEADS_TOGETHER = 4
LANES = 128
BF16_SUBLANES = 16
VMEM_LIMIT = 56 * 1024 * 1024

RET_LOG_GAMMA = tuple(math.log1p(-2.0 ** (-5.0 - h)) for h in range(RET_HEADS))


def _const_spec(shape):
    nd = len(shape)
    return pl.BlockSpec(shape, lambda *_: (0,) * nd, pipeline_mode=pl.Buffered(1))


def _rms(x, w):
    ms = jnp.mean(x * x, axis=-1, keepdims=True)
    return x * lax.rsqrt(ms + EPS) * w


def _silu(x):
    return x * jax.nn.sigmoid(x)


def _dot(a, b):
    return jnp.dot(a, b, preferred_element_type=F32)


def _dot_nt(a, b):
    return lax.dot_general(a, b, (((1,), (1,)), ((), ())), preferred_element_type=F32)


def _dot_tn(a, b):
    return lax.dot_general(a, b, (((0,), (0,)), ((), ())), preferred_element_type=F32)


def _cast_blocks(rows, n_steps):
    units = rows // BF16_SUBLANES
    assert rows % BF16_SUBLANES == 0
    return max(n for n in range(1, min(units, n_steps) + 1) if units % n == 0)


def _call(body, *, name, grid, in_specs, args, out_specs, out_shapes, scratch_shapes=(), aliases=None,
          casts=()):
    n_in, n_out, n_cast = len(args), len(out_shapes), len(casts)
    n_steps = math.prod(grid)
    linear = (lambda i: i) if len(grid) == 1 else (lambda b, c: b * grid[1] + c)
    in_specs, out_specs, out_shapes, args = list(in_specs), list(out_specs), list(out_shapes), list(args)
    transposed = []
    for stack, layer, *flags in casts:
        transposed.append(bool(flags))
        if flags:
            _, cols, rows = stack.shape
            cb = 2 * LANES
            nb = pl.cdiv(cols, cb)
            assert nb <= n_steps
            blk = lambda *g, nb=nb: jnp.minimum(linear(*g), nb - 1)
            in_specs.append(pl.BlockSpec((None, cb, rows), lambda *g, blk=blk, layer=layer: (layer, blk(*g), 0)))
            out_specs.append(pl.BlockSpec((rows, cb), lambda *g, blk=blk: (0, blk(*g))))
        else:
            _, rows, cols = stack.shape
            nb = _cast_blocks(rows, n_steps)
            rb = rows // nb
            blk = lambda *g, nb=nb: jnp.minimum(linear(*g), nb - 1)
            in_specs.append(pl.BlockSpec((None, rb, cols), lambda *g, blk=blk, layer=layer: (layer, blk(*g), 0)))
            out_specs.append(pl.BlockSpec((rb, cols), lambda *g, blk=blk: (blk(*g), 0)))
        out_shapes.append(jax.ShapeDtypeStruct((rows, cols), BF16))
        args.append(stack)

    def kern(*refs):
        ins, rest = refs[:n_in], refs[n_in:]
        cast_in, rest = rest[:n_cast], rest[n_cast:]
        outs, rest = rest[:n_out], rest[n_out:]
        cast_out, scratch = rest[:n_cast], rest[n_cast:]
        for src, dst, t in zip(cast_in, cast_out, transposed):
            dst[...] = (src[...].T if t else src[...]).astype(BF16)
        body(*ins, *outs, *scratch)

    res = pl.pallas_call(
        kern, grid=grid, in_specs=in_specs, out_specs=out_specs, out_shape=out_shapes,
        scratch_shapes=list(scratch_shapes), input_output_aliases=aliases or {},
        compiler_params=pltpu.CompilerParams(
            dimension_semantics=("arbitrary",) * len(grid), vmem_limit_bytes=VMEM_LIMIT),
        name=name,
    )(*args)
    return res[:n_out], res[n_out:]


def _cast_weights(casts, *, n_steps=8):
    _, out = _call(lambda: None, name="cast_weights", grid=(n_steps,), in_specs=[], args=[],
                   out_specs=[], out_shapes=[], casts=casts)
    return out


def _ffn_kernel(x_ref, *rest, n_x_tiles, layer, first, final):
    rest = list(rest)
    meta_ref = rest.pop(0) if first else None
    nw_ref, wg_ref, wu_ref, wo_ref = rest[:4]
    fw_ref = rest[4] if final else None
    o_ref = rest[-1]
    i = pl.program_id(0)

    def ffn(x):
        xn = _rms(x, nw_ref[layer:layer + 1, :]).astype(BF16)
        g = _dot(xn, wg_ref[...])
        u = _dot(xn, wu_ref[...])
        hid = (_silu(g) * u).astype(BF16)
        y = x + 0.5 * _dot(hid, wo_ref[...])
        return _rms(y, fw_ref[...]) if final else y

    @pl.when(i < n_x_tiles)
    def _():
        for lo in range(0, x_ref.shape[0], ROW_PASS):
            o_ref[lo:lo + ROW_PASS, :] = ffn(x_ref[lo:lo + ROW_PASS, :])

    if not final:
        @pl.when(i == n_x_tiles)
        def _():
            o_ref[...] = jnp.zeros_like(o_ref)
            o_ref[LEAD_ZERO:LEAD, :] = ffn(meta_ref[...] if first else x_ref[LEAD_ZERO:LEAD, :])


def _ffn(h, norm, layer, w_in, w_out, *, tm, meta=None, final_w=None, casts=()):
    d = h.shape[1]
    dff = w_out.shape[0]
    first, final = meta is not None, final_w is not None
    n_x_tiles = (h.shape[0] if first else h.shape[0] - tm) // tm
    n_tiles = n_x_tiles if final else n_x_tiles + 1
    in_specs = [pl.BlockSpec((tm, d), lambda i: (jnp.minimum(i, n_x_tiles - 1), 0) if first else (i, 0))]
    args = [h]
    if first:
        in_specs.append(_const_spec((N_META, d)))
        args.append(meta)
    in_specs += [
        _const_spec(norm.shape),
        pl.BlockSpec((d, dff), lambda i: (0, 0), pipeline_mode=pl.Buffered(1)),
        pl.BlockSpec((d, dff), lambda i: (0, 1), pipeline_mode=pl.Buffered(1)),
        _const_spec((dff, d)),
    ]
    args += [norm, w_in, w_in, w_out]
    if final:
        in_specs.append(_const_spec((1, d)))
        args.append(final_w.reshape(1, d))
    (out,), cast = _call(
        functools.partial(_ffn_kernel, n_x_tiles=n_x_tiles, layer=layer, first=first, final=final),
        name="ffn_first" if first else "ffn_final" if final else "ffn",
        grid=(n_tiles,), in_specs=in_specs, args=args,
        out_specs=[pl.BlockSpec((tm, d), lambda i: (i, 0))],
        out_shapes=[jax.ShapeDtypeStruct((n_tiles * tm, d), F32)], casts=casts)
    return out, cast


def _chunk_block(b, c, *, chunks_per_batch, lead_block):
    first = jnp.where(b == 0, lead_block, b * chunks_per_batch)
    return jnp.where(c == 0, first, b * chunks_per_batch + c - 1)


def _recurrence_specs(C, widths, d, *, batch, seq):
    cpb = seq // C
    assert seq % C == 0 and LEAD_ZERO // C == (LEAD - 1) // C
    lead_block = batch * cpb + LEAD_ZERO // C
    idx = lambda b, c: (_chunk_block(b, c, chunks_per_batch=cpb, lead_block=lead_block), 0)
    return (batch, cpb + 1), [pl.BlockSpec((C, w), idx) for w in widths], pl.BlockSpec((C, d), idx)


def _ret_mixer_kernel(h_ref, nw_ref, w_ref, hn_ref, wo_ref, o_ref,
                      s_ref, slead_ref, dec_ref, dq_ref, dk_ref, cosr_ref, sinr_ref, obuf_ref, inv_ref,
                      *, layer, dk, dv):
    b, c = pl.program_id(0), pl.program_id(1)
    C = RET_CHUNK
    block_rows = h_ref.shape[0]
    nh = RET_HEADS
    half = dk // 2
    k0, v0, g0 = nh * dk, 2 * nh * dk, 2 * nh * dk + nh * dv

    @pl.when((b == 0) & (c == 0))
    def _():
        s_ref[...] = jnp.zeros_like(s_ref)
        row = lax.broadcasted_iota(jnp.int32, (C, C), 0)
        col = lax.broadcasted_iota(jnp.int32, (C, C), 1)
        rel = (row - col).astype(F32)
        rowl = lax.broadcasted_iota(jnp.int32, (C, half), 0).astype(F32)
        lane = lax.broadcasted_iota(jnp.int32, (1, half), 1).astype(F32)
        inv_ref[...] = jnp.exp(lane * (-math.log(ROPE_BASE) / (half - 1)))
        ang = rowl * inv_ref[...]
        cosr_ref[...] = jnp.cos(ang)
        sinr_ref[...] = jnp.sin(ang)
        for hd in range(nh):
            lg = RET_LOG_GAMMA[hd]
            dec_ref[hd] = jnp.where(rel >= 0, jnp.exp(lg * jnp.maximum(rel, 0.0)), 0.0)
            dq_ref[hd] = jnp.exp(lg * (rowl + 1.0))
            dk_ref[hd] = jnp.exp(lg * (C - 1.0 - rowl))

    def chunk(rows, n, first_pos):
        ang0 = jnp.asarray(first_pos, F32) * inv_ref[...]
        cb, sb = jnp.cos(ang0), jnp.sin(ang0)
        cos = cb * cosr_ref[:n] - sb * sinr_ref[:n]
        sin = sb * cosr_ref[:n] + cb * sinr_ref[:n]

        def rotary(y):
            t1, t2 = y[:, :half], y[:, half:]
            return jnp.concatenate([t1 * cos - t2 * sin, t1 * sin + t2 * cos], axis=1)

        x = h_ref[rows, :]
        xn = _rms(x, nw_ref[layer:layer + 1, :]).astype(BF16)
        for hd in range(nh):
            q = rotary(_dot(xn, w_ref[:, hd * dk:(hd + 1) * dk]))
            k = rotary(_dot(xn, w_ref[:, k0 + hd * dk:k0 + (hd + 1) * dk])) * dk ** -0.5
            v = _dot(xn, w_ref[:, v0 + hd * dv:v0 + (hd + 1) * dv]).astype(BF16)
            g = _dot(xn, w_ref[:, g0 + hd * dv:g0 + (hd + 1) * dv])
            qd = (q * jnp.concatenate([dq_ref[hd, :n]] * 2, axis=1)).astype(BF16)
            kd = (k * jnp.concatenate([dk_ref[hd, C - n:]] * 2, axis=1)).astype(BF16)
            s = (_dot_nt(q.astype(BF16), k.astype(BF16)) * dec_ref[hd, :n, :n]).astype(BF16)
            st = s_ref[hd]
            o = _dot(s, v) + _dot(qd, st.astype(BF16))
            s_ref[hd] = st * math.exp(RET_LOG_GAMMA[hd] * n) + _dot_tn(kd, v)
            on = _rms(o, hn_ref[hd:hd + 1, :]) * _silu(g)
            obuf_ref[:n, hd * dv:(hd + 1) * dv] = on.astype(BF16)
        o_ref[rows, :] = x + _dot(obuf_ref[:n], wo_ref[...])

    @pl.when((b == 0) & (c == 0))
    def _():
        o_ref[...] = jnp.zeros_like(o_ref)
        lead = slice(LEAD_ZERO % block_rows, LEAD_ZERO % block_rows + N_META)
        chunk(lead, N_META, 0)
        slead_ref[...] = s_ref[...]

    @pl.when((b > 0) & (c == 0))
    def _():
        s_ref[...] = slead_ref[...]

    @pl.when(c > 0)
    def _():
        for lo in range(0, block_rows, C):
            chunk(slice(lo, lo + C), C, (c - 1) * block_rows + lo + N_META)


def _ret_mixer(h, norm, layer, w_in, w_out, head_norm, *, batch, seq, casts=()):
    rows, d = h.shape
    n = w_in.shape[1]
    nh, dv = head_norm.shape
    dk = (n - 2 * nh * dv) // (2 * nh)
    assert dk == 2 * LANES
    C = RET_CHUNK
    grid, (h_spec,), o_spec = _recurrence_specs(RET_CHUNKS_PER_STEP * C, (d,), d, batch=batch, seq=seq)
    (out,), cast = _call(
        functools.partial(_ret_mixer_kernel, layer=layer, dk=dk, dv=dv), name="ret_mixer", grid=grid,
        in_specs=[h_spec, _const_spec(norm.shape), _const_spec((d, n)), _const_spec((nh, dv)),
                  _const_spec((nh * dv, d))],
        args=[h, norm, w_in, head_norm, w_out],
        out_specs=[o_spec], out_shapes=[jax.ShapeDtypeStruct((rows, d), F32)],
        aliases={0: 0},
        scratch_shapes=[
            pltpu.VMEM((nh, dk, dv), F32),
            pltpu.VMEM((nh, dk, dv), F32),
            pltpu.VMEM((nh, C, C), F32),
            pltpu.VMEM((nh, C, dk // 2), F32),
            pltpu.VMEM((nh, C, dk // 2), F32),
            pltpu.VMEM((C, dk // 2), F32),
            pltpu.VMEM((C, dk // 2), F32),
            pltpu.VMEM((C, nh * dv), BF16),
            pltpu.VMEM((1, dk // 2), F32),
        ], casts=casts)
    return out, cast


def _gla_proj_kernel(h_ref, nw_ref, w_ref, wg_ref, bg_ref, o_ref, la_ref, *, n_x_tiles, layer, n_main):
    i = pl.program_id(0)
    hk = wg_ref.shape[1]
    dk = hk // GLA_HEADS
    step = 512
    assert n_main // step >= GLA_HEADS

    def project(rows):
        hn = _rms(h_ref[rows, :], nw_ref[layer:layer + 1, :]).astype(BF16)
        n_blocks = n_main // step
        for n, lo in enumerate(range(0, n_main, step)):
            y = _dot(hn, w_ref[:, lo:lo + step])
            if lo < hk:
                y = y * dk ** -0.5
            elif lo >= n_main - (n_main - 2 * hk) // 2:
                y = _silu(y)
            o_ref[rows, lo:lo + step] = y.astype(BF16)
            if n == 0:
                z = _dot(hn, w_ref[:, n_main:n_main + GLA_RANK])
            elif n == 1:
                xg = _dot(z.astype(BF16), wg_ref[...]) + bg_ref[...]
            if n >= n_blocks - GLA_HEADS:
                hd = n - (n_blocks - GLA_HEADS)
                xh = xg[:, hd * dk:(hd + 1) * dk]
                ls = jnp.minimum(xh, 0.0) - jnp.log(1.0 + jnp.exp(-jnp.abs(xh)))
                la_ref[rows, hd * dk:(hd + 1) * dk] = ls * (1.0 / GLA_TAU)

    @pl.when(i < n_x_tiles)
    def _():
        for lo in range(0, h_ref.shape[0], ROW_PASS):
            project(slice(lo, lo + ROW_PASS))

    @pl.when(i == n_x_tiles)
    def _():
        o_ref[...] = jnp.zeros_like(o_ref)
        la_ref[...] = jnp.zeros_like(la_ref)
        project(slice(LEAD_ZERO, LEAD))


def _gla_proj(h, norm, layer, w_in, w_gate, b_gate, *, tm, casts=()):
    rows, d = h.shape
    n = w_in.shape[1]
    n_main = n - GLA_RANK
    hk = w_gate.shape[1]
    (p, la), cast = _call(
        functools.partial(_gla_proj_kernel, n_x_tiles=rows // tm - 1, layer=layer, n_main=n_main),
        name="gla_proj", grid=(rows // tm,),
        in_specs=[pl.BlockSpec((tm, d), lambda i: (i, 0)), _const_spec(norm.shape), _const_spec((d, n)),
                  _const_spec((GLA_RANK, hk)), _const_spec((1, hk))],
        args=[h, norm, w_in, w_gate, b_gate.reshape(1, hk)],
        out_specs=[pl.BlockSpec((tm, n_main), lambda i: (i, 0)), pl.BlockSpec((tm, hk), lambda i: (i, 0))],
        out_shapes=[jax.ShapeDtypeStruct((rows, n_main), BF16), jax.ShapeDtypeStruct((rows, hk), F32)],
        casts=casts)
    return p, la, cast


def _gla_core_kernel(p_ref, la_ref, h_ref, hn_ref, wo_ref, o_ref,
                     st_ref, stlead_ref, ball_ref, obuf_ref, *, dk, dv):
    bi, c = pl.program_id(0), pl.program_id(1)

    @pl.when((bi == 0) & (c == 0))
    def _():
        st_ref[...] = jnp.zeros_like(st_ref)

    def chunk(start, n, slot):
        rows = pl.ds(start, n)
        _gla_chunk(p_ref.at[rows, :], la_ref.at[rows, :], hn_ref, st_ref, ball_ref.at[slot, pl.ds(0, n), :],
                   obuf_ref.at[rows, :], dk=dk, dv=dv)

    @pl.when((bi == 0) & (c == 0))
    def _():
        o_ref[...] = jnp.zeros_like(o_ref)
        lead = LEAD_ZERO % p_ref.shape[0]
        chunk(lead, N_META, 0)
        o_ref[lead:lead + N_META, :] = (h_ref[lead:lead + N_META, :]
                                        + _dot(obuf_ref[lead:lead + N_META, :], wo_ref[...]))
        stlead_ref[...] = st_ref[...]

    @pl.when((bi > 0) & (c == 0))
    def _():
        st_ref[...] = stlead_ref[...]

    @pl.when(c > 0)
    def _():
        for n in range(p_ref.shape[0] // GLA_CHUNK):
            chunk(n * GLA_CHUNK, GLA_CHUNK, n)
        o_ref[...] = h_ref[...] + _dot(obuf_ref[...], wo_ref[...])


def _left_block_end_rows(b_ref, s):
    C, dk = b_ref.shape
    sub = 8
    bcast = lambda e, n: jnp.broadcast_to(b_ref[e:e + 1, :], (n, dk))
    if 2 * s >= sub:
        n = max(2 * s, sub)
        return jnp.concatenate([bcast(e, n) for e in range(s - 1, C, n)], axis=0)
    r = lax.broadcasted_iota(jnp.int32, (sub, dk), 0)
    tiles = []
    for t0 in range(0, C, sub):
        tile = bcast(t0 + s - 1, sub)
        for blk in range(2 * s, sub, 2 * s):
            tile = jnp.where(r >= blk, bcast(t0 + blk + s - 1, sub), tile)
        tiles.append(tile)
    return jnp.concatenate(tiles, axis=0)


def _gla_chunk(p_ref, la_ref, hn_ref, st_ref, ball_ref, obuf_ref, *, dk, dv):
    C = p_ref.shape[0]
    nh = GLA_HEADS
    k0, v0, g0 = nh * dk, 2 * nh * dk, 2 * nh * dk + nh * dv

    rowl = lax.broadcasted_iota(jnp.int32, (C, dk), 0)
    row = lax.broadcasted_iota(jnp.int32, (C, C), 0)
    col = lax.broadcasted_iota(jnp.int32, (C, C), 1)

    differ = jnp.bitwise_xor(row, col)
    level = jnp.full((C, C), -1, jnp.int32)
    for lv in range(C.bit_length() - 1):
        level = jnp.where((row > col) & (differ >= (1 << lv)), lv, level)

    a_all = la_ref[...]
    a1 = a_all.astype(BF16)
    r1 = a_all - a1.astype(F32)
    a2 = r1.astype(BF16)
    a3 = (r1 - a2.astype(F32)).astype(BF16)
    tri = (row >= col).astype(BF16)
    ball_ref[...] = _dot(tri, a1) + _dot(tri, a2) + _dot(tri, a3)

    def load(hd):
        qb = p_ref[:, hd * dk:(hd + 1) * dk]
        kb = p_ref[:, k0 + hd * dk:k0 + (hd + 1) * dk]
        b_ref = ball_ref.at[:, hd * dk:(hd + 1) * dk]
        return dict(hd=hd, qb=qb, kb=kb, q=qb.astype(F32), k=kb.astype(F32), b_ref=b_ref, b=b_ref[...])

    for first in range(0, nh, GLA_HEADS_TOGETHER):
        heads = [load(hd) for hd in range(first, min(first + GLA_HEADS_TOGETHER, nh))]
        for t in heads:
            t["scores"] = jnp.where(col == row, _dot_nt(t["qb"], t["kb"]), 0.0)
        for lv in range(C.bit_length() - 1):
            s = 1 << lv
            right = (rowl & s) != 0
            for t in heads:
                hd = t["hd"]
                if s == 1:
                    w = jnp.where(right, jnp.exp(la_ref[:, hd * dk:(hd + 1) * dk]), 1.0)
                else:
                    w = jnp.exp(-jnp.abs(t["b"] - _left_block_end_rows(t["b_ref"], s)))
                z = (jnp.where(right, t["q"], t["k"]) * w).astype(BF16)
                t["scores"] = jnp.where(level == lv, _dot_nt(z, z), t["scores"])

        for t in heads:
            hd, q, k, b, b_ref = t["hd"], t["q"], t["k"], t["b"], t["b_ref"]
            v = p_ref[:, v0 + hd * dv:v0 + (hd + 1) * dv]
            sg = p_ref[:, g0 + hd * dv:g0 + (hd + 1) * dv]
            btot = b_ref[C - 1:C, :]
            st = st_ref[hd]
            o = (_dot(t["scores"].astype(BF16), v)
                 + _dot_nt((q * jnp.exp(b)).astype(BF16), st.astype(BF16)))
            kt = (k * jnp.exp(btot - b)).astype(BF16)
            st_ref[hd] = st * jnp.exp(btot) + _dot_tn(v, kt)
            on = _rms(o, hn_ref[hd:hd + 1, :]) * sg.astype(F32)
            obuf_ref[:, hd * dv:(hd + 1) * dv] = on.astype(BF16)


def _gla_core(p, la, h, head_norm, w_out, *, batch, seq, casts=()):
    rows, d = h.shape
    n = p.shape[1]
    nh, dv = head_norm.shape
    hk = la.shape[1]
    dk = hk // nh
    per_step = GLA_CHUNKS_PER_STEP
    C = per_step * GLA_CHUNK
    grid, (p_spec, la_spec, h_spec), o_spec = _recurrence_specs(C, (n, hk, d), d, batch=batch, seq=seq)
    (out,), cast = _call(
        functools.partial(_gla_core_kernel, dk=dk, dv=dv), name="gla_core", grid=grid,
        in_specs=[p_spec, la_spec, h_spec, _const_spec((nh, dv)), _const_spec((nh * dv, d))],
        args=[p, la, h, head_norm, w_out],
        out_specs=[o_spec], out_shapes=[jax.ShapeDtypeStruct((rows, d), F32)],
        aliases={2: 0},
        scratch_shapes=[
            pltpu.VMEM((nh, dv, dk), F32),
            pltpu.VMEM((nh, dv, dk), F32),
            pltpu.VMEM((per_step, GLA_CHUNK, hk), F32),
            pltpu.VMEM((C, nh * dv), BF16),
        ], casts=casts)
    return out, cast


def kernel(x, meta_tokens, norm_ffn1, ffn1_w_in, ffn1_w_out, norm_mix, norm_ffn2, ffn2_w_in, ffn2_w_out, ret_w_in, ret_head_norm, ret_w_out, gla_w_in, gla_w_gate, gla_b_gate, gla_head_norm, gla_w_out, final_norm):
    batch, seq, d = x.shape
    depth = norm_ffn1.shape[0]
    tm = ROW_TILE
    assert meta_tokens.shape == (N_META, d) and seq % tm == 0 and tm >= LEAD and tm % ROW_PASS == 0

    h = x.reshape(batch * seq, d)

    gla_w_in_t = jnp.swapaxes(gla_w_in, 1, 2)

    stages = []
    for i in range(depth):
        j = i // 2
        stages.append([(ffn1_w_in, i), (ffn1_w_out, i)])
        if i % 2 == 0:
            stages.append([(ret_w_in, j), (ret_w_out, j)])
        else:
            stages += [[(gla_w_in_t, j, "transposed"), (gla_w_gate, j)], [(gla_w_out, j)]]
        stages.append([(ffn2_w_in, i), (ffn2_w_out, i)])
    stages.append([])
    nxt = iter(stages[1:])

    w = _cast_weights(stages[0])
    for i in range(depth):
        j = i // 2
        h, w = _ffn(h, norm_ffn1, i, *w, tm=tm, meta=meta_tokens.astype(x.dtype) if i == 0 else None,
                    casts=next(nxt))
        if i % 2 == 0:
            h, w = _ret_mixer(h, norm_mix, i, *w, ret_head_norm[j], batch=batch, seq=seq, casts=next(nxt))
        else:
            p, la, w = _gla_proj(h, norm_mix, i, *w, gla_b_gate[j], tm=tm, casts=next(nxt))
            h, w = _gla_core(p, la, h, gla_head_norm[j], *w, batch=batch, seq=seq, casts=next(nxt))
        h, w = _ffn(h, norm_ffn2, i, *w, tm=tm, final_w=final_norm if i == depth - 1 else None,
                    casts=next(nxt))
    return h.reshape(batch, seq, d)
```

```python
import functools
import math

import jax
import jax.numpy as jnp
from jax import lax
from jax.experimental import pallas as pl
from jax.experimental.pallas import tpu as pltpu

F32 = jnp.float32
BF16 = jnp.bfloat16

EPS = 1e-6
N_META = 16
LEAD = 256
LEAD_ZERO = LEAD - N_META
ROW_TILE = 1024
ROW_PASS = 512
ROPE_BASE = 10000.0
RET_HEADS = 4
GLA_HEADS = 4
GLA_RANK = 16
GLA_TAU = 16.0
RET_CHUNK = 256
RET_CHUNKS_PER_STEP = 2
GLA_CHUNK = 128
GLA_CHUNKS_PER_STEP = 4
GLA_CHUNKS_TOGETHER = 4
LANES = 128
BF16_SUBLANES = 16
VMEM_LIMIT = 56 * 1024 * 1024

RET_LOG_GAMMA = tuple(math.log1p(-2.0 ** (-5.0 - h)) for h in range(RET_HEADS))


def _const_spec(shape):
    nd = len(shape)
    return pl.BlockSpec(shape, lambda *_: (0,) * nd, pipeline_mode=pl.Buffered(1))


def _rms(x, w):
    ms = jnp.mean(x * x, axis=-1, keepdims=True)
    return x * lax.rsqrt(ms + EPS) * w


def _silu(x):
    return x * jax.nn.sigmoid(x)


def _dot(a, b):
    return jnp.dot(a, b, preferred_element_type=F32)


def _dot_nt(a, b):
    return lax.dot_general(a, b, (((1,), (1,)), ((), ())), preferred_element_type=F32)


def _dot_tn(a, b):
    return lax.dot_general(a, b, (((0,), (0,)), ((), ())), preferred_element_type=F32)


def _cast_blocks(rows, n_steps):
    units = rows // BF16_SUBLANES
    assert rows % BF16_SUBLANES == 0
    return max(n for n in range(1, min(units, n_steps) + 1) if units % n == 0)


def _call(body, *, name, grid, in_specs, args, out_specs, out_shapes, scratch_shapes=(), aliases=None,
          casts=()):
    n_in, n_out, n_cast = len(args), len(out_shapes), len(casts)
    n_steps = math.prod(grid)
    linear = (lambda i: i) if len(grid) == 1 else (lambda b, c: b * grid[1] + c)
    in_specs, out_specs, out_shapes, args = list(in_specs), list(out_specs), list(out_shapes), list(args)
    transposed = []
    for stack, layer, *flags in casts:
        transposed.append(bool(flags))
        if flags:
            _, cols, rows = stack.shape
            cb = 2 * LANES
            nb = pl.cdiv(cols, cb)
            assert nb <= n_steps
            blk = lambda *g, nb=nb: jnp.minimum(linear(*g), nb - 1)
            in_specs.append(pl.BlockSpec((None, cb, rows), lambda *g, blk=blk, layer=layer: (layer, blk(*g), 0)))
            out_specs.append(pl.BlockSpec((rows, cb), lambda *g, blk=blk: (0, blk(*g))))
        else:
            _, rows, cols = stack.shape
            nb = _cast_blocks(rows, n_steps)
            rb = rows // nb
            blk = lambda *g, nb=nb: jnp.minimum(linear(*g), nb - 1)
            in_specs.append(pl.BlockSpec((None, rb, cols), lambda *g, blk=blk, layer=layer: (layer, blk(*g), 0)))
            out_specs.append(pl.BlockSpec((rb, cols), lambda *g, blk=blk: (blk(*g), 0)))
        out_shapes.append(jax.ShapeDtypeStruct((rows, cols), BF16))
        args.append(stack)

    def kern(*refs):
        ins, rest = refs[:n_in], refs[n_in:]
        cast_in, rest = rest[:n_cast], rest[n_cast:]
        outs, rest = rest[:n_out], rest[n_out:]
        cast_out, scratch = rest[:n_cast], rest[n_cast:]
        for src, dst, t in zip(cast_in, cast_out, transposed):
            dst[...] = (src[...].T if t else src[...]).astype(BF16)
        body(*ins, *outs, *scratch)

    res = pl.pallas_call(
        kern, grid=grid, in_specs=in_specs, out_specs=out_specs, out_shape=out_shapes,
        scratch_shapes=list(scratch_shapes), input_output_aliases=aliases or {},
        compiler_params=pltpu.CompilerParams(
            dimension_semantics=("arbitrary",) * len(grid), vmem_limit_bytes=VMEM_LIMIT),
        name=name,
    )(*args)
    return res[:n_out], res[n_out:]


def _cast_weights(casts, *, n_steps=8):
    _, out = _call(lambda: None, name="cast_weights", grid=(n_steps,), in_specs=[], args=[],
                   out_specs=[], out_shapes=[], casts=casts)
    return out


def _ffn_kernel(x_ref, *rest, n_x_tiles, layer, first, final):
    rest = list(rest)
    meta_ref = rest.pop(0) if first else None
    nw_ref, wg_ref, wu_ref, wo_ref = rest[:4]
    fw_ref = rest[4] if final else None
    o_ref = rest[-1]
    i = pl.program_id(0)

    def ffn(x):
        xn = _rms(x, nw_ref[layer:layer + 1, :]).astype(BF16)
        g = _dot(xn, wg_ref[...])
        u = _dot(xn, wu_ref[...])
        hid = (_silu(g) * u).astype(BF16)
        y = x + 0.5 * _dot(hid, wo_ref[...])
        return _rms(y, fw_ref[...]) if final else y

    @pl.when(i < n_x_tiles)
    def _():
        for lo in range(0, x_ref.shape[0], ROW_PASS):
            o_ref[lo:lo + ROW_PASS, :] = ffn(x_ref[lo:lo + ROW_PASS, :])

    if not final:
        @pl.when(i == n_x_tiles)
        def _():
            o_ref[...] = jnp.zeros_like(o_ref)
            o_ref[LEAD_ZERO:LEAD, :] = ffn(meta_ref[...] if first else x_ref[LEAD_ZERO:LEAD, :])


def _ffn(h, norm, layer, w_in, w_out, *, tm, meta=None, final_w=None, casts=()):
    d = h.shape[1]
    dff = w_out.shape[0]
    first, final = meta is not None, final_w is not None
    n_x_tiles = (h.shape[0] if first else h.shape[0] - tm) // tm
    n_tiles = n_x_tiles if final else n_x_tiles + 1
    in_specs = [pl.BlockSpec((tm, d), lambda i: (jnp.minimum(i, n_x_tiles - 1), 0) if first else (i, 0))]
    args = [h]
    if first:
        in_specs.append(_const_spec((N_META, d)))
        args.append(meta)
    in_specs += [
        _const_spec(norm.shape),
        pl.BlockSpec((d, dff), lambda i: (0, 0), pipeline_mode=pl.Buffered(1)),
        pl.BlockSpec((d, dff), lambda i: (0, 1), pipeline_mode=pl.Buffered(1)),
        _const_spec((dff, d)),
    ]
    args += [norm, w_in, w_in, w_out]
    if final:
        in_specs.append(_const_spec((1, d)))
        args.append(final_w.reshape(1, d))
    (out,), cast = _call(
        functools.partial(_ffn_kernel, n_x_tiles=n_x_tiles, layer=layer, first=first, final=final),
        name="ffn_first" if first else "ffn_final" if final else "ffn",
        grid=(n_tiles,), in_specs=in_specs, args=args,
        out_specs=[pl.BlockSpec((tm, d), lambda i: (i, 0))],
        out_shapes=[jax.ShapeDtypeStruct((n_tiles * tm, d), F32)], casts=casts)
    return out, cast


def _chunk_block(b, c, *, chunks_per_batch, lead_block):
    first = jnp.where(b == 0, lead_block, b * chunks_per_batch)
    return jnp.where(c == 0, first, b * chunks_per_batch + c - 1)


def _recurrence_specs(C, widths, d, *, batch, seq):
    cpb = seq // C
    assert seq % C == 0 and LEAD_ZERO // C == (LEAD - 1) // C
    lead_block = batch * cpb + LEAD_ZERO // C
    idx = lambda b, c: (_chunk_block(b, c, chunks_per_batch=cpb, lead_block=lead_block), 0)
    return (batch, cpb + 1), [pl.BlockSpec((C, w), idx) for w in widths], pl.BlockSpec((C, d), idx)


def _ret_mixer_kernel(h_ref, nw_ref, w_ref, hn_ref, wo_ref, o_ref,
                      s_ref, slead_ref, dec_ref, dq_ref, dk_ref, cosr_ref, sinr_ref, obuf_ref, inv_ref,
                      *, layer, dk, dv):
    b, c = pl.program_id(0), pl.program_id(1)
    C = RET_CHUNK
    block_rows = h_ref.shape[0]
    nh = RET_HEADS
    half = dk // 2
    k0, v0, g0 = nh * dk, 2 * nh * dk, 2 * nh * dk + nh * dv

    @pl.when((b == 0) & (c == 0))
    def _():
        s_ref[...] = jnp.zeros_like(s_ref)
        row = lax.broadcasted_iota(jnp.int32, (C, C), 0)
        col = lax.broadcasted_iota(jnp.int32, (C, C), 1)
        rel = (row - col).astype(F32)
        rowl = lax.broadcasted_iota(jnp.int32, (C, half), 0).astype(F32)
        lane = lax.broadcasted_iota(jnp.int32, (1, half), 1).astype(F32)
        inv_ref[...] = jnp.exp(lane * (-math.log(ROPE_BASE) / (half - 1)))
        ang = rowl * inv_ref[...]
        cosr_ref[...] = jnp.cos(ang)
        sinr_ref[...] = jnp.sin(ang)
        for hd in range(nh):
            lg = RET_LOG_GAMMA[hd]
            dec_ref[hd] = jnp.where(rel >= 0, jnp.exp(lg * jnp.maximum(rel, 0.0)), 0.0)
            dq_ref[hd] = jnp.exp(lg * (rowl + 1.0))
            dk_ref[hd] = jnp.exp(lg * (C - 1.0 - rowl))

    def chunk(rows, n, first_pos):
        ang0 = jnp.asarray(first_pos, F32) * inv_ref[...]
        cb, sb = jnp.cos(ang0), jnp.sin(ang0)
        cos = cb * cosr_ref[:n] - sb * sinr_ref[:n]
        sin = sb * cosr_ref[:n] + cb * sinr_ref[:n]

        def rotary(y):
            t1, t2 = y[:, :half], y[:, half:]
            return jnp.concatenate([t1 * cos - t2 * sin, t1 * sin + t2 * cos], axis=1)

        x = h_ref[rows, :]
        xn = _rms(x, nw_ref[layer:layer + 1, :]).astype(BF16)
        for hd in range(nh):
            q = rotary(_dot(xn, w_ref[:, hd * dk:(hd + 1) * dk]))
            k = rotary(_dot(xn, w_ref[:, k0 + hd * dk:k0 + (hd + 1) * dk])) * dk ** -0.5
            v = _dot(xn, w_ref[:, v0 + hd * dv:v0 + (hd + 1) * dv]).astype(BF16)
            g = _dot(xn, w_ref[:, g0 + hd * dv:g0 + (hd + 1) * dv])
            qd = (q * jnp.concatenate([dq_ref[hd, :n]] * 2, axis=1)).astype(BF16)
            kd = (k * jnp.concatenate([dk_ref[hd, C - n:]] * 2, axis=1)).astype(BF16)
            s = (_dot_nt(q.astype(BF16), k.astype(BF16)) * dec_ref[hd, :n, :n]).astype(BF16)
            st = s_ref[hd]
            o = _dot(s, v) + _dot(qd, st.astype(BF16))
            s_ref[hd] = st * math.exp(RET_LOG_GAMMA[hd] * n) + _dot_tn(kd, v)
            on = _rms(o, hn_ref[hd:hd + 1, :]) * _silu(g)
            obuf_ref[:n, hd * dv:(hd + 1) * dv] = on.astype(BF16)
        o_ref[rows, :] = x + _dot(obuf_ref[:n], wo_ref[...])

    @pl.when((b == 0) & (c == 0))
    def _():
        o_ref[...] = jnp.zeros_like(o_ref)
        lead = slice(LEAD_ZERO % block_rows, LEAD_ZERO % block_rows + N_META)
        chunk(lead, N_META, 0)
        slead_ref[...] = s_ref[...]

    @pl.when((b > 0) & (c == 0))
    def _():
        s_ref[...] = slead_ref[...]

    @pl.when(c > 0)
    def _():
        for lo in range(0, block_rows, C):
            chunk(slice(lo, lo + C), C, (c - 1) * block_rows + lo + N_META)


def _ret_mixer(h, norm, layer, w_in, w_out, head_norm, *, batch, seq, casts=()):
    rows, d = h.shape
    n = w_in.shape[1]
    nh, dv = head_norm.shape
    dk = (n - 2 * nh * dv) // (2 * nh)
    assert dk == 2 * LANES
    C = RET_CHUNK
    grid, (h_spec,), o_spec = _recurrence_specs(RET_CHUNKS_PER_STEP * C, (d,), d, batch=batch, seq=seq)
    (out,), cast = _call(
        functools.partial(_ret_mixer_kernel, layer=layer, dk=dk, dv=dv), name="ret_mixer", grid=grid,
        in_specs=[h_spec, _const_spec(norm.shape), _const_spec((d, n)), _const_spec((nh, dv)),
                  _const_spec((nh * dv, d))],
        args=[h, norm, w_in, head_norm, w_out],
        out_specs=[o_spec], out_shapes=[jax.ShapeDtypeStruct((rows, d), F32)],
        aliases={0: 0},
        scratch_shapes=[
            pltpu.VMEM((nh, dk, dv), F32),
            pltpu.VMEM((nh, dk, dv), F32),
            pltpu.VMEM((nh, C, C), F32),
            pltpu.VMEM((nh, C, dk // 2), F32),
            pltpu.VMEM((nh, C, dk // 2), F32),
            pltpu.VMEM((C, dk // 2), F32),
            pltpu.VMEM((C, dk // 2), F32),
            pltpu.VMEM((C, nh * dv), BF16),
            pltpu.VMEM((1, dk // 2), F32),
        ], casts=casts)
    return out, cast


def _gla_proj_kernel(h_ref, nw_ref, w_ref, wg_ref, bg_ref, o_ref, la_ref, *, n_x_tiles, layer, n_main):
    i = pl.program_id(0)
    hk = wg_ref.shape[1]
    dk = hk // GLA_HEADS
    step = 512
    assert n_main // step >= GLA_HEADS

    def project(rows):
        hn = _rms(h_ref[rows, :], nw_ref[layer:layer + 1, :]).astype(BF16)
        n_blocks = n_main // step
        for n, lo in enumerate(range(0, n_main, step)):
            y = _dot(hn, w_ref[:, lo:lo + step])
            if lo < hk:
                y = y * dk ** -0.5
            elif lo >= n_main - (n_main - 2 * hk) // 2:
                y = _silu(y)
            o_ref[rows, lo:lo + step] = y.astype(BF16)
            if n == 0:
                z = _dot(hn, w_ref[:, n_main:n_main + GLA_RANK])
            elif n == 1:
                xg = _dot(z.astype(BF16), wg_ref[...]) + bg_ref[...]
            if n >= n_blocks - GLA_HEADS:
                hd = n - (n_blocks - GLA_HEADS)
                xh = xg[:, hd * dk:(hd + 1) * dk]
                ls = jnp.minimum(xh, 0.0) - jnp.log(1.0 + jnp.exp(-jnp.abs(xh)))
                la_ref[rows, hd * dk:(hd + 1) * dk] = ls * (1.0 / GLA_TAU)

    @pl.when(i < n_x_tiles)
    def _():
        for lo in range(0, h_ref.shape[0], ROW_PASS):
            project(slice(lo, lo + ROW_PASS))

    @pl.when(i == n_x_tiles)
    def _():
        o_ref[...] = jnp.zeros_like(o_ref)
        la_ref[...] = jnp.zeros_like(la_ref)
        project(slice(LEAD_ZERO, LEAD))


def _gla_proj(h, norm, layer, w_in, w_gate, b_gate, *, tm, casts=()):
    rows, d = h.shape
    n = w_in.shape[1]
    n_main = n - GLA_RANK
    hk = w_gate.shape[1]
    (p, la), cast = _call(
        functools.partial(_gla_proj_kernel, n_x_tiles=rows // tm - 1, layer=layer, n_main=n_main),
        name="gla_proj", grid=(rows // tm,),
        in_specs=[pl.BlockSpec((tm, d), lambda i: (i, 0)), _const_spec(norm.shape), _const_spec((d, n)),
                  _const_spec((GLA_RANK, hk)), _const_spec((1, hk))],
        args=[h, norm, w_in, w_gate, b_gate.reshape(1, hk)],
        out_specs=[pl.BlockSpec((tm, n_main), lambda i: (i, 0)), pl.BlockSpec((tm, hk), lambda i: (i, 0))],
        out_shapes=[jax.ShapeDtypeStruct((rows, n_main), BF16), jax.ShapeDtypeStruct((rows, hk), F32)],
        casts=casts)
    return p, la, cast


def _gla_core_kernel(p_ref, la_ref, h_ref, hn_ref, wo_ref, o_ref,
                     st_ref, stlead_ref, ball_ref, obuf_ref, *, dk, dv):
    bi, c = pl.program_id(0), pl.program_id(1)

    @pl.when((bi == 0) & (c == 0))
    def _():
        st_ref[...] = jnp.zeros_like(st_ref)

    def view(start, n, slot):
        rows = pl.ds(start, n)
        return p_ref.at[rows, :], la_ref.at[rows, :], ball_ref.at[slot, pl.ds(0, n), :], obuf_ref.at[rows, :]

    @pl.when((bi == 0) & (c == 0))
    def _():
        o_ref[...] = jnp.zeros_like(o_ref)
        lead = LEAD_ZERO % p_ref.shape[0]
        _gla_chunks([view(lead, N_META, 0)], hn_ref, st_ref, dk=dk, dv=dv)
        o_ref[lead:lead + N_META, :] = (h_ref[lead:lead + N_META, :]
                                        + _dot(obuf_ref[lead:lead + N_META, :], wo_ref[...]))
        stlead_ref[...] = st_ref[...]

    @pl.when((bi > 0) & (c == 0))
    def _():
        st_ref[...] = stlead_ref[...]

    @pl.when(c > 0)
    def _():
        for n in range(0, p_ref.shape[0] // GLA_CHUNK, GLA_CHUNKS_TOGETHER):
            _gla_chunks([view(m * GLA_CHUNK, GLA_CHUNK, m) for m in range(n, n + GLA_CHUNKS_TOGETHER)],
                        hn_ref, st_ref, dk=dk, dv=dv)
        o_ref[...] = h_ref[...] + _dot(obuf_ref[...], wo_ref[...])


def _left_block_end_rows(b_ref, s):
    C, dk = b_ref.shape
    sub = 8
    bcast = lambda e, n: jnp.broadcast_to(b_ref[e:e + 1, :], (n, dk))
    if 2 * s >= sub:
        n = max(2 * s, sub)
        return jnp.concatenate([bcast(e, n) for e in range(s - 1, C, n)], axis=0)
    r = lax.broadcasted_iota(jnp.int32, (sub, dk), 0)
    tiles = []
    for t0 in range(0, C, sub):
        tile = bcast(t0 + s - 1, sub)
        for blk in range(2 * s, sub, 2 * s):
            tile = jnp.where(r >= blk, bcast(t0 + blk + s - 1, sub), tile)
        tiles.append(tile)
    return jnp.concatenate(tiles, axis=0)


def _gla_chunks(views, hn_ref, st_ref, *, dk, dv):
    C = views[0][0].shape[0]
    nh = GLA_HEADS
    k0, v0, g0 = nh * dk, 2 * nh * dk, 2 * nh * dk + nh * dv

    rowl = lax.broadcasted_iota(jnp.int32, (C, dk), 0)
    row = lax.broadcasted_iota(jnp.int32, (C, C), 0)
    col = lax.broadcasted_iota(jnp.int32, (C, C), 1)

    differ = jnp.bitwise_xor(row, col)
    level = jnp.full((C, C), -1, jnp.int32)
    for lv in range(C.bit_length() - 1):
        level = jnp.where((row > col) & (differ >= (1 << lv)), lv, level)

    tri = (row >= col).astype(BF16)
    for _, la_ref, ball_ref, _ in views:
        a_all = la_ref[...]
        a1 = a_all.astype(BF16)
        r1 = a_all - a1.astype(F32)
        a2 = r1.astype(BF16)
        a3 = (r1 - a2.astype(F32)).astype(BF16)
        ball_ref[...] = _dot(tri, a1) + _dot(tri, a2) + _dot(tri, a3)

    def load(view, hd):
        p_ref, la_ref, ball_ref, obuf_ref = view
        qb = p_ref[:, hd * dk:(hd + 1) * dk]
        kb = p_ref[:, k0 + hd * dk:k0 + (hd + 1) * dk]
        b_ref = ball_ref.at[:, hd * dk:(hd + 1) * dk]
        return dict(hd=hd, p_ref=p_ref, la_ref=la_ref, obuf_ref=obuf_ref, qb=qb, kb=kb,
                    q=qb.astype(F32), k=kb.astype(F32), b_ref=b_ref, b=b_ref[...])

    tasks = [load(view, hd) for view in views for hd in range(nh)]
    for t in tasks:
        t["scores"] = jnp.where(col == row, _dot_nt(t["qb"], t["kb"]), 0.0)
    for lv in range(C.bit_length() - 1):
        s = 1 << lv
        right = (rowl & s) != 0
        for t in tasks:
            hd = t["hd"]
            if s == 1:
                w = jnp.where(right, jnp.exp(t["la_ref"][:, hd * dk:(hd + 1) * dk]), 1.0)
            else:
                w = jnp.exp(-jnp.abs(t["b"] - _left_block_end_rows(t["b_ref"], s)))
            z = (jnp.where(right, t["q"], t["k"]) * w).astype(BF16)
            t["scores"] = jnp.where(level == lv, _dot_nt(z, z), t["scores"])

    for t in tasks:
        hd, q, k, b, b_ref, p_ref = t["hd"], t["q"], t["k"], t["b"], t["b_ref"], t["p_ref"]
        v = p_ref[:, v0 + hd * dv:v0 + (hd + 1) * dv]
        sg = p_ref[:, g0 + hd * dv:g0 + (hd + 1) * dv]
        btot = b_ref[C - 1:C, :]
        st = st_ref[hd]
        o = (_dot(t["scores"].astype(BF16), v)
             + _dot_nt((q * jnp.exp(b)).astype(BF16), st.astype(BF16)))
        kt = (k * jnp.exp(btot - b)).astype(BF16)
        st_ref[hd] = st * jnp.exp(btot) + _dot_tn(v, kt)
        on = _rms(o, hn_ref[hd:hd + 1, :]) * sg.astype(F32)
        t["obuf_ref"][:, hd * dv:(hd + 1) * dv] = on.astype(BF16)


def _gla_core(p, la, h, head_norm, w_out, *, batch, seq, casts=()):
    rows, d = h.shape
    n = p.shape[1]
    nh, dv = head_norm.shape
    hk = la.shape[1]
    dk = hk // nh
    per_step = GLA_CHUNKS_PER_STEP
    C = per_step * GLA_CHUNK
    grid, (p_spec, la_spec, h_spec), o_spec = _recurrence_specs(C, (n, hk, d), d, batch=batch, seq=seq)
    (out,), cast = _call(
        functools.partial(_gla_core_kernel, dk=dk, dv=dv), name="gla_core", grid=grid,
        in_specs=[p_spec, la_spec, h_spec, _const_spec((nh, dv)), _const_spec((nh * dv, d))],
        args=[p, la, h, head_norm, w_out],
        out_specs=[o_spec], out_shapes=[jax.ShapeDtypeStruct((rows, d), F32)],
        aliases={2: 0},
        scratch_shapes=[
            pltpu.VMEM((nh, dv, dk), F32),
            pltpu.VMEM((nh, dv, dk), F32),
            pltpu.VMEM((per_step, GLA_CHUNK, hk), F32),
            pltpu.VMEM((C, nh * dv), BF16),
        ], casts=casts)
    return out, cast


def kernel(x, meta_tokens, norm_ffn1, ffn1_w_in, ffn1_w_out, norm_mix, norm_ffn2, ffn2_w_in, ffn2_w_out, ret_w_in, ret_head_norm, ret_w_out, gla_w_in, gla_w_gate, gla_b_gate, gla_head_norm, gla_w_out, final_norm):
    batch, seq, d = x.shape
    depth = norm_ffn1.shape[0]
    tm = ROW_TILE
    assert meta_tokens.shape == (N_META, d) and seq % tm == 0 and tm >= LEAD and tm % ROW_PASS == 0

    h = x.reshape(batch * seq, d)

    gla_w_in_t = jnp.swapaxes(gla_w_in, 1, 2)

    stages = []
    for i in range(depth):
        j = i // 2
        stages.append([(ffn1_w_in, i), (ffn1_w_out, i)])
        if i % 2 == 0:
            stages.append([(ret_w_in, j), (ret_w_out, j)])
        else:
            stages += [[(gla_w_in_t, j, "transposed"), (gla_w_gate, j)], [(gla_w_out, j)]]
        stages.append([(ffn2_w_in, i), (ffn2_w_out, i)])
    stages.append([])
    nxt = iter(stages[1:])

    w = _cast_weights(stages[0])
    for i in range(depth):
        j = i // 2
        h, w = _ffn(h, norm_ffn1, i, *w, tm=tm, meta=meta_tokens.astype(x.dtype) if i == 0 else None,
                    casts=next(nxt))
        if i % 2 == 0:
            h, w = _ret_mixer(h, norm_mix, i, *w, ret_head_norm[j], batch=batch, seq=seq, casts=next(nxt))
        else:
            p, la, w = _gla_proj(h, norm_mix, i, *w, gla_b_gate[j], tm=tm, casts=next(nxt))
            h, w = _gla_core(p, la, h, gla_head_norm[j], *w, batch=batch, seq=seq, casts=next(nxt))
        h, w = _ffn(h, norm_ffn2, i, *w, tm=tm, final_w=final_norm if i == depth - 1 else None,
                    casts=next(nxt))
    return h.reshape(batch, seq, d)
```

```python
import functools
import math

import jax
import jax.numpy as jnp
from jax import lax
from jax.experimental import pallas as pl
from jax.experimental.pallas import tpu as pltpu

F32 = jnp.float32
BF16 = jnp.bfloat16

EPS = 1e-6
N_META = 16
LEAD = 256
LEAD_ZERO = LEAD - N_META
ROW_TILE = 1024
ROW_PASS = 512
ROPE_BASE = 10000.0
RET_HEADS = 4
GLA_HEADS = 4
GLA_RANK = 16
GLA_TAU = 16.0
RET_CHUNK = 256
RET_CHUNKS_PER_STEP = 2
GLA_CHUNK = 128
GLA_CHUNKS_PER_STEP = 4
GLA_CHUNKS_TOGETHER = 4
LANES = 128
BF16_SUBLANES = 16
VMEM_LIMIT = 56 * 1024 * 1024

RET_LOG_GAMMA = tuple(math.log1p(-2.0 ** (-5.0 - h)) for h in range(RET_HEADS))


def _const_spec(shape):
    nd = len(shape)
    return pl.BlockSpec(shape, lambda *_: (0,) * nd, pipeline_mode=pl.Buffered(1))


def _rms(x, w):
    ms = jnp.mean(x * x, axis=-1, keepdims=True)
    return x * lax.rsqrt(ms + EPS) * w


def _silu(x):
    return x * jax.nn.sigmoid(x)


def _dot(a, b):
    return jnp.dot(a, b, preferred_element_type=F32)


def _dot_nt(a, b):
    return lax.dot_general(a, b, (((1,), (1,)), ((), ())), preferred_element_type=F32)


def _dot_tn(a, b):
    return lax.dot_general(a, b, (((0,), (0,)), ((), ())), preferred_element_type=F32)


def _cast_blocks(rows, n_steps):
    units = rows // BF16_SUBLANES
    assert rows % BF16_SUBLANES == 0
    return max(n for n in range(1, min(units, n_steps) + 1) if units % n == 0)


def _call(body, *, name, grid, in_specs, args, out_specs, out_shapes, scratch_shapes=(), aliases=None,
          casts=()):
    n_in, n_out, n_cast = len(args), len(out_shapes), len(casts)
    n_steps = math.prod(grid)
    linear = (lambda i: i) if len(grid) == 1 else (lambda b, c: b * grid[1] + c)
    in_specs, out_specs, out_shapes, args = list(in_specs), list(out_specs), list(out_shapes), list(args)
    transposed = []
    for stack, layer, *flags in casts:
        transposed.append(bool(flags))
        if flags:
            _, cols, rows = stack.shape
            cb = 2 * LANES
            nb = pl.cdiv(cols, cb)
            assert nb <= n_steps
            blk = lambda *g, nb=nb: jnp.minimum(linear(*g), nb - 1)
            in_specs.append(pl.BlockSpec((None, cb, rows), lambda *g, blk=blk, layer=layer: (layer, blk(*g), 0)))
            out_specs.append(pl.BlockSpec((rows, cb), lambda *g, blk=blk: (0, blk(*g))))
        else:
            _, rows, cols = stack.shape
            nb = _cast_blocks(rows, n_steps)
            rb = rows // nb
            blk = lambda *g, nb=nb: jnp.minimum(linear(*g), nb - 1)
            in_specs.append(pl.BlockSpec((None, rb, cols), lambda *g, blk=blk, layer=layer: (layer, blk(*g), 0)))
            out_specs.append(pl.BlockSpec((rb, cols), lambda *g, blk=blk: (blk(*g), 0)))
        out_shapes.append(jax.ShapeDtypeStruct((rows, cols), BF16))
        args.append(stack)

    def kern(*refs):
        ins, rest = refs[:n_in], refs[n_in:]
        cast_in, rest = rest[:n_cast], rest[n_cast:]
        outs, rest = rest[:n_out], rest[n_out:]
        cast_out, scratch = rest[:n_cast], rest[n_cast:]
        for src, dst, t in zip(cast_in, cast_out, transposed):
            dst[...] = (src[...].T if t else src[...]).astype(BF16)
        body(*ins, *outs, *scratch)

    res = pl.pallas_call(
        kern, grid=grid, in_specs=in_specs, out_specs=out_specs, out_shape=out_shapes,
        scratch_shapes=list(scratch_shapes), input_output_aliases=aliases or {},
        compiler_params=pltpu.CompilerParams(
            dimension_semantics=("arbitrary",) * len(grid), vmem_limit_bytes=VMEM_LIMIT),
        name=name,
    )(*args)
    return res[:n_out], res[n_out:]


def _cast_weights(casts, *, n_steps=8):
    _, out = _call(lambda: None, name="cast_weights", grid=(n_steps,), in_specs=[], args=[],
                   out_specs=[], out_shapes=[], casts=casts)
    return out


def _ffn_kernel(x_ref, *rest, n_x_tiles, layer, first, final):
    rest = list(rest)
    meta_ref = rest.pop(0) if first else None
    nw_ref, wg_ref, wu_ref, wo_ref = rest[:4]
    fw_ref = rest[4] if final else None
    o_ref = rest[-1]
    i = pl.program_id(0)

    def ffn(x):
        xn = _rms(x, nw_ref[layer:layer + 1, :]).astype(BF16)
        g = _dot(xn, wg_ref[...])
        u = _dot(xn, wu_ref[...])
        hid = (_silu(g) * u).astype(BF16)
        y = x + 0.5 * _dot(hid, wo_ref[...])
        return _rms(y, fw_ref[...]) if final else y

    @pl.when(i < n_x_tiles)
    def _():
        for lo in range(0, x_ref.shape[0], ROW_PASS):
            o_ref[lo:lo + ROW_PASS, :] = ffn(x_ref[lo:lo + ROW_PASS, :])

    if not final:
        @pl.when(i == n_x_tiles)
        def _():
            o_ref[...] = jnp.zeros_like(o_ref)
            o_ref[LEAD_ZERO:LEAD, :] = ffn(meta_ref[...] if first else x_ref[LEAD_ZERO:LEAD, :])


def _ffn(h, norm, layer, w_in, w_out, *, tm, meta=None, final_w=None, casts=()):
    d = h.shape[1]
    dff = w_out.shape[0]
    first, final = meta is not None, final_w is not None
    n_x_tiles = (h.shape[0] if first else h.shape[0] - tm) // tm
    n_tiles = n_x_tiles if final else n_x_tiles + 1
    in_specs = [pl.BlockSpec((tm, d), lambda i: (jnp.minimum(i, n_x_tiles - 1), 0) if first else (i, 0))]
    args = [h]
    if first:
        in_specs.append(_const_spec((N_META, d)))
        args.append(meta)
    in_specs += [
        _const_spec(norm.shape),
        pl.BlockSpec((d, dff), lambda i: (0, 0), pipeline_mode=pl.Buffered(1)),
        pl.BlockSpec((d, dff), lambda i: (0, 1), pipeline_mode=pl.Buffered(1)),
        _const_spec((dff, d)),
    ]
    args += [norm, w_in, w_in, w_out]
    if final:
        in_specs.append(_const_spec((1, d)))
        args.append(final_w.reshape(1, d))
    (out,), cast = _call(
        functools.partial(_ffn_kernel, n_x_tiles=n_x_tiles, layer=layer, first=first, final=final),
        name="ffn_first" if first else "ffn_final" if final else "ffn",
        grid=(n_tiles,), in_specs=in_specs, args=args,
        out_specs=[pl.BlockSpec((tm, d), lambda i: (i, 0))],
        out_shapes=[jax.ShapeDtypeStruct((n_tiles * tm, d), F32)], casts=casts)
    return out, cast


def _chunk_block(b, c, *, chunks_per_batch, lead_block):
    first = jnp.where(b == 0, lead_block, b * chunks_per_batch)
    return jnp.where(c == 0, first, b * chunks_per_batch + c - 1)


def _recurrence_specs(C, widths, d, *, batch, seq):
    cpb = seq // C
    assert seq % C == 0 and LEAD_ZERO // C == (LEAD - 1) // C
    lead_block = batch * cpb + LEAD_ZERO // C
    idx = lambda b, c: (_chunk_block(b, c, chunks_per_batch=cpb, lead_block=lead_block), 0)
    return (batch, cpb + 1), [pl.BlockSpec((C, w), idx) for w in widths], pl.BlockSpec((C, d), idx)


def _ret_mixer_kernel(h_ref, nw_ref, w_ref, hn_ref, wo_ref, o_ref,
                      s_ref, slead_ref, dec_ref, dq_ref, dk_ref, cosr_ref, sinr_ref, obuf_ref, inv_ref,
                      *, layer, dk, dv):
    b, c = pl.program_id(0), pl.program_id(1)
    C = RET_CHUNK
    block_rows = h_ref.shape[0]
    nh = RET_HEADS
    half = dk // 2
    k0, v0, g0 = nh * dk, 2 * nh * dk, 2 * nh * dk + nh * dv

    @pl.when((b == 0) & (c == 0))
    def _():
        s_ref[...] = jnp.zeros_like(s_ref)
        row = lax.broadcasted_iota(jnp.int32, (C, C), 0)
        col = lax.broadcasted_iota(jnp.int32, (C, C), 1)
        rel = (row - col).astype(F32)
        rowl = lax.broadcasted_iota(jnp.int32, (C, half), 0).astype(F32)
        lane = lax.broadcasted_iota(jnp.int32, (1, half), 1).astype(F32)
        inv_ref[...] = jnp.exp(lane * (-math.log(ROPE_BASE) / (half - 1)))
        ang = rowl * inv_ref[...]
        cosr_ref[...] = jnp.cos(ang)
        sinr_ref[...] = jnp.sin(ang)
        for hd in range(nh):
            lg = RET_LOG_GAMMA[hd]
            dec_ref[hd] = jnp.where(rel >= 0, jnp.exp(lg * jnp.maximum(rel, 0.0)), 0.0)
            dq_ref[hd] = jnp.exp(lg * (rowl + 1.0))
            dk_ref[hd] = jnp.exp(lg * (C - 1.0 - rowl))

    def chunk(rows, n, first_pos):
        ang0 = jnp.asarray(first_pos, F32) * inv_ref[...]
        cb, sb = jnp.cos(ang0), jnp.sin(ang0)
        cos = cb * cosr_ref[:n] - sb * sinr_ref[:n]
        sin = sb * cosr_ref[:n] + cb * sinr_ref[:n]

        def rotary(y):
            t1, t2 = y[:, :half], y[:, half:]
            return jnp.concatenate([t1 * cos - t2 * sin, t1 * sin + t2 * cos], axis=1)

        x = h_ref[rows, :]
        xn = _rms(x, nw_ref[layer:layer + 1, :]).astype(BF16)
        def project(hd):
            q = rotary(_dot(xn, w_ref[:, hd * dk:(hd + 1) * dk]))
            k = rotary(_dot(xn, w_ref[:, k0 + hd * dk:k0 + (hd + 1) * dk])) * dk ** -0.5
            v = _dot(xn, w_ref[:, v0 + hd * dv:v0 + (hd + 1) * dv]).astype(BF16)
            g = _dot(xn, w_ref[:, g0 + hd * dv:g0 + (hd + 1) * dv])
            return q, k, v, g

        ahead = project(0)
        for hd in range(nh):
            q, k, v, g = ahead
            if hd + 1 < nh:
                ahead = project(hd + 1)
            qd = (q * jnp.concatenate([dq_ref[hd, :n]] * 2, axis=1)).astype(BF16)
            kd = (k * jnp.concatenate([dk_ref[hd, C - n:]] * 2, axis=1)).astype(BF16)
            kv = _dot_tn(kd, v)
            s = (_dot_nt(q.astype(BF16), k.astype(BF16)) * dec_ref[hd, :n, :n]).astype(BF16)
            st = s_ref[hd]
            o = _dot(s, v) + _dot(qd, st.astype(BF16))
            s_ref[hd] = st * math.exp(RET_LOG_GAMMA[hd] * n) + kv
            on = _rms(o, hn_ref[hd:hd + 1, :]) * _silu(g)
            obuf_ref[:n, hd * dv:(hd + 1) * dv] = on.astype(BF16)
        o_ref[rows, :] = x + _dot(obuf_ref[:n], wo_ref[...])

    @pl.when((b == 0) & (c == 0))
    def _():
        o_ref[...] = jnp.zeros_like(o_ref)
        lead = slice(LEAD_ZERO % block_rows, LEAD_ZERO % block_rows + N_META)
        chunk(lead, N_META, 0)
        slead_ref[...] = s_ref[...]

    @pl.when((b > 0) & (c == 0))
    def _():
        s_ref[...] = slead_ref[...]

    @pl.when(c > 0)
    def _():
        for lo in range(0, block_rows, C):
            chunk(slice(lo, lo + C), C, (c - 1) * block_rows + lo + N_META)


def _ret_mixer(h, norm, layer, w_in, w_out, head_norm, *, batch, seq, casts=()):
    rows, d = h.shape
    n = w_in.shape[1]
    nh, dv = head_norm.shape
    dk = (n - 2 * nh * dv) // (2 * nh)
    assert dk == 2 * LANES
    C = RET_CHUNK
    grid, (h_spec,), o_spec = _recurrence_specs(RET_CHUNKS_PER_STEP * C, (d,), d, batch=batch, seq=seq)
    (out,), cast = _call(
        functools.partial(_ret_mixer_kernel, layer=layer, dk=dk, dv=dv), name="ret_mixer", grid=grid,
        in_specs=[h_spec, _const_spec(norm.shape), _const_spec((d, n)), _const_spec((nh, dv)),
                  _const_spec((nh * dv, d))],
        args=[h, norm, w_in, head_norm, w_out],
        out_specs=[o_spec], out_shapes=[jax.ShapeDtypeStruct((rows, d), F32)],
        aliases={0: 0},
        scratch_shapes=[
            pltpu.VMEM((nh, dk, dv), F32),
            pltpu.VMEM((nh, dk, dv), F32),
            pltpu.VMEM((nh, C, C), F32),
            pltpu.VMEM((nh, C, dk // 2), F32),
            pltpu.VMEM((nh, C, dk // 2), F32),
            pltpu.VMEM((C, dk // 2), F32),
            pltpu.VMEM((C, dk // 2), F32),
            pltpu.VMEM((C, nh * dv), BF16),
            pltpu.VMEM((1, dk // 2), F32),
        ], casts=casts)
    return out, cast


def _gla_proj_kernel(h_ref, nw_ref, w_ref, wg_ref, bg_ref, o_ref, la_ref, *, n_x_tiles, layer, n_main):
    i = pl.program_id(0)
    hk = wg_ref.shape[1]
    dk = hk // GLA_HEADS
    step = 512
    assert n_main // step >= GLA_HEADS

    def project(rows):
        hn = _rms(h_ref[rows, :], nw_ref[layer:layer + 1, :]).astype(BF16)
        n_blocks = n_main // step
        for n, lo in enumerate(range(0, n_main, step)):
            y = _dot(hn, w_ref[:, lo:lo + step])
            if lo < hk:
                y = y * dk ** -0.5
            elif lo >= n_main - (n_main - 2 * hk) // 2:
                y = _silu(y)
            o_ref[rows, lo:lo + step] = y.astype(BF16)
            if n == 0:
                z = _dot(hn, w_ref[:, n_main:n_main + GLA_RANK])
            elif n == 1:
                xg = _dot(z.astype(BF16), wg_ref[...]) + bg_ref[...]
            if n >= n_blocks - GLA_HEADS:
                hd = n - (n_blocks - GLA_HEADS)
                xh = xg[:, hd * dk:(hd + 1) * dk]
                ls = jnp.minimum(xh, 0.0) - jnp.log(1.0 + jnp.exp(-jnp.abs(xh)))
                la_ref[rows, hd * dk:(hd + 1) * dk] = ls * (1.0 / GLA_TAU)

    @pl.when(i < n_x_tiles)
    def _():
        for lo in range(0, h_ref.shape[0], ROW_PASS):
            project(slice(lo, lo + ROW_PASS))

    @pl.when(i == n_x_tiles)
    def _():
        o_ref[...] = jnp.zeros_like(o_ref)
        la_ref[...] = jnp.zeros_like(la_ref)
        project(slice(LEAD_ZERO, LEAD))


def _gla_proj(h, norm, layer, w_in, w_gate, b_gate, *, tm, casts=()):
    rows, d = h.shape
    n = w_in.shape[1]
    n_main = n - GLA_RANK
    hk = w_gate.shape[1]
    (p, la), cast = _call(
        functools.partial(_gla_proj_kernel, n_x_tiles=rows // tm - 1, layer=layer, n_main=n_main),
        name="gla_proj", grid=(rows // tm,),
        in_specs=[pl.BlockSpec((tm, d), lambda i: (i, 0)), _const_spec(norm.shape), _const_spec((d, n)),
                  _const_spec((GLA_RANK, hk)), _const_spec((1, hk))],
        args=[h, norm, w_in, w_gate, b_gate.reshape(1, hk)],
        out_specs=[pl.BlockSpec((tm, n_main), lambda i: (i, 0)), pl.BlockSpec((tm, hk), lambda i: (i, 0))],
        out_shapes=[jax.ShapeDtypeStruct((rows, n_main), BF16), jax.ShapeDtypeStruct((rows, hk), F32)],
        casts=casts)
    return p, la, cast


def _gla_core_kernel(p_ref, la_ref, h_ref, hn_ref, wo_ref, o_ref,
                     st_ref, stlead_ref, ball_ref, obuf_ref, *, dk, dv):
    bi, c = pl.program_id(0), pl.program_id(1)

    @pl.when((bi == 0) & (c == 0))
    def _():
        st_ref[...] = jnp.zeros_like(st_ref)

    def view(start, n, slot):
        rows = pl.ds(start, n)
        return p_ref.at[rows, :], la_ref.at[rows, :], ball_ref.at[slot, pl.ds(0, n), :], obuf_ref.at[rows, :]

    @pl.when((bi == 0) & (c == 0))
    def _():
        o_ref[...] = jnp.zeros_like(o_ref)
        lead = LEAD_ZERO % p_ref.shape[0]
        _gla_chunks([view(lead, N_META, 0)], hn_ref, st_ref, dk=dk, dv=dv)
        o_ref[lead:lead + N_META, :] = (h_ref[lead:lead + N_META, :]
                                        + _dot(obuf_ref[lead:lead + N_META, :], wo_ref[...]))
        stlead_ref[...] = st_ref[...]

    @pl.when((bi > 0) & (c == 0))
    def _():
        st_ref[...] = stlead_ref[...]

    @pl.when(c > 0)
    def _():
        for n in range(0, p_ref.shape[0] // GLA_CHUNK, GLA_CHUNKS_TOGETHER):
            _gla_chunks([view(m * GLA_CHUNK, GLA_CHUNK, m) for m in range(n, n + GLA_CHUNKS_TOGETHER)],
                        hn_ref, st_ref, dk=dk, dv=dv)
        o_ref[...] = h_ref[...] + _dot(obuf_ref[...], wo_ref[...])


def _left_block_end_rows(b_ref, s):
    C, dk = b_ref.shape
    sub = 8
    bcast = lambda e, n: jnp.broadcast_to(b_ref[e:e + 1, :], (n, dk))
    if 2 * s >= sub:
        n = max(2 * s, sub)
        return jnp.concatenate([bcast(e, n) for e in range(s - 1, C, n)], axis=0)
    r = lax.broadcasted_iota(jnp.int32, (sub, dk), 0)
    tiles = []
    for t0 in range(0, C, sub):
        tile = bcast(t0 + s - 1, sub)
        for blk in range(2 * s, sub, 2 * s):
            tile = jnp.where(r >= blk, bcast(t0 + blk + s - 1, sub), tile)
        tiles.append(tile)
    return jnp.concatenate(tiles, axis=0)


def _gla_chunks(views, hn_ref, st_ref, *, dk, dv):
    C = views[0][0].shape[0]
    nh = GLA_HEADS
    k0, v0, g0 = nh * dk, 2 * nh * dk, 2 * nh * dk + nh * dv

    rowl = lax.broadcasted_iota(jnp.int32, (C, dk), 0)
    row = lax.broadcasted_iota(jnp.int32, (C, C), 0)
    col = lax.broadcasted_iota(jnp.int32, (C, C), 1)

    differ = jnp.bitwise_xor(row, col)
    level = jnp.full((C, C), -1, jnp.int32)
    for lv in range(C.bit_length() - 1):
        level = jnp.where((row > col) & (differ >= (1 << lv)), lv, level)

    tri = (row >= col).astype(BF16)
    for _, la_ref, ball_ref, _ in views:
        a_all = la_ref[...]
        a1 = a_all.astype(BF16)
        r1 = a_all - a1.astype(F32)
        a2 = r1.astype(BF16)
        a3 = (r1 - a2.astype(F32)).astype(BF16)
        ball_ref[...] = _dot(tri, a1) + _dot(tri, a2) + _dot(tri, a3)

    def load(view, hd):
        p_ref, la_ref, ball_ref, obuf_ref = view
        qb = p_ref[:, hd * dk:(hd + 1) * dk]
        kb = p_ref[:, k0 + hd * dk:k0 + (hd + 1) * dk]
        b_ref = ball_ref.at[:, hd * dk:(hd + 1) * dk]
        return dict(hd=hd, p_ref=p_ref, la_ref=la_ref, obuf_ref=obuf_ref, qb=qb, kb=kb,
                    q=qb.astype(F32), k=kb.astype(F32), b_ref=b_ref, b=b_ref[...])

    tasks = [load(view, hd) for view in views for hd in range(nh)]
    for t in tasks:
        t["scores"] = jnp.where(col == row, _dot_nt(t["qb"], t["kb"]), 0.0)
    for lv in range(C.bit_length() - 1):
        s = 1 << lv
        right = (rowl & s) != 0
        for t in tasks:
            hd = t["hd"]
            if s == 1:
                w = jnp.where(right, jnp.exp(t["la_ref"][:, hd * dk:(hd + 1) * dk]), 1.0)
            else:
                w = jnp.exp(-jnp.abs(t["b"] - _left_block_end_rows(t["b_ref"], s)))
            z = (jnp.where(right, t["q"], t["k"]) * w).astype(BF16)
            t["scores"] = jnp.where(level == lv, _dot_nt(z, z), t["scores"])

    for t in tasks:
        hd, q, k, b, b_ref, p_ref = t["hd"], t["q"], t["k"], t["b"], t["b_ref"], t["p_ref"]
        v = p_ref[:, v0 + hd * dv:v0 + (hd + 1) * dv]
        sg = p_ref[:, g0 + hd * dv:g0 + (hd + 1) * dv]
        btot = b_ref[C - 1:C, :]
        st = st_ref[hd]
        o = (_dot(t["scores"].astype(BF16), v)
             + _dot_nt((q * jnp.exp(b)).astype(BF16), st.astype(BF16)))
        kt = (k * jnp.exp(btot - b)).astype(BF16)
        st_ref[hd] = st * jnp.exp(btot) + _dot_tn(v, kt)
        on = _rms(o, hn_ref[hd:hd + 1, :]) * sg.astype(F32)
        t["obuf_ref"][:, hd * dv:(hd + 1) * dv] = on.astype(BF16)


def _gla_core(p, la, h, head_norm, w_out, *, batch, seq, casts=()):
    rows, d = h.shape
    n = p.shape[1]
    nh, dv = head_norm.shape
    hk = la.shape[1]
    dk = hk // nh
    per_step = GLA_CHUNKS_PER_STEP
    C = per_step * GLA_CHUNK
    grid, (p_spec, la_spec, h_spec), o_spec = _recurrence_specs(C, (n, hk, d), d, batch=batch, seq=seq)
    (out,), cast = _call(
        functools.partial(_gla_core_kernel, dk=dk, dv=dv), name="gla_core", grid=grid,
        in_specs=[p_spec, la_spec, h_spec, _const_spec((nh, dv)), _const_spec((nh * dv, d))],
        args=[p, la, h, head_norm, w_out],
        out_specs=[o_spec], out_shapes=[jax.ShapeDtypeStruct((rows, d), F32)],
        aliases={2: 0},
        scratch_shapes=[
            pltpu.VMEM((nh, dv, dk), F32),
            pltpu.VMEM((nh, dv, dk), F32),
            pltpu.VMEM((per_step, GLA_CHUNK, hk), F32),
            pltpu.VMEM((C, nh * dv), BF16),
        ], casts=casts)
    return out, cast


def kernel(x, meta_tokens, norm_ffn1, ffn1_w_in, ffn1_w_out, norm_mix, norm_ffn2, ffn2_w_in, ffn2_w_out, ret_w_in, ret_head_norm, ret_w_out, gla_w_in, gla_w_gate, gla_b_gate, gla_head_norm, gla_w_out, final_norm):
    batch, seq, d = x.shape
    depth = norm_ffn1.shape[0]
    tm = ROW_TILE
    assert meta_tokens.shape == (N_META, d) and seq % tm == 0 and tm >= LEAD and tm % ROW_PASS == 0

    h = x.reshape(batch * seq, d)

    gla_w_in_t = jnp.swapaxes(gla_w_in, 1, 2)

    stages = []
    for i in range(depth):
        j = i // 2
        stages.append([(ffn1_w_in, i), (ffn1_w_out, i)])
        if i % 2 == 0:
            stages.append([(ret_w_in, j), (ret_w_out, j)])
        else:
            stages += [[(gla_w_in_t, j, "transposed"), (gla_w_gate, j)], [(gla_w_out, j)]]
        stages.append([(ffn2_w_in, i), (ffn2_w_out, i)])
    stages.append([])
    nxt = iter(stages[1:])

    w = _cast_weights(stages[0])
    for i in range(depth):
        j = i // 2
        h, w = _ffn(h, norm_ffn1, i, *w, tm=tm, meta=meta_tokens.astype(x.dtype) if i == 0 else None,
                    casts=next(nxt))
        if i % 2 == 0:
            h, w = _ret_mixer(h, norm_mix, i, *w, ret_head_norm[j], batch=batch, seq=seq, casts=next(nxt))
        else:
            p, la, w = _gla_proj(h, norm_mix, i, *w, gla_b_gate[j], tm=tm, casts=next(nxt))
            h, w = _gla_core(p, la, h, gla_head_norm[j], *w, batch=batch, seq=seq, casts=next(nxt))
        h, w = _ffn(h, norm_ffn2, i, *w, tm=tm, final_w=final_norm if i == depth - 1 else None,
                    casts=next(nxt))
    return h.reshape(batch, seq, d)
```
